```python
import jax
import jax.numpy as jnp
from jax import lax
import numpy as np

D_MODEL = 2048
BATCH = 4
SEQ = 8192
DEPTH = 1
DEC_BATCH = 8
DEC_SEQ = 32
PAST_LEN = 4096

CHUNK = 64
EPS = 1e-6
N_MOD = 6
D_POOL = D_MODEL // 2
POOL_WINDOWS = (2, 4, 8, 16)
POOL_GROUP = D_POOL // len(POOL_WINDOWS)
POOL_HIST = max(POOL_WINDOWS) - 1
D_ATT = D_MODEL // 2
HEAD_DIM = 128
N_HEADS = D_ATT // HEAD_DIM
Q_BLOCK = 128
IN_SPLITS = (D_POOL, D_POOL + D_ATT, D_POOL + 2 * D_ATT, D_POOL + 3 * D_ATT,
             D_POOL + 3 * D_ATT + N_HEADS, D_POOL + 3 * D_ATT + N_HEADS + D_MODEL)
N_IN = D_POOL + 3 * D_ATT + N_HEADS + 2 * D_MODEL
N_EXPERTS = 64
TOP_K = 8
D_EXPERT = 512
D_SHARED = 512
ROUTED_SCALE = 2.5
ROW_BLOCK = 128

kernel_name = 'pool_fox_moe_stream_step'


def rms_norm(x, g):
    xf = x.astype(jnp.float32)
    y = xf * lax.rsqrt(jnp.mean(xf * xf, axis=-1, keepdims=True) + EPS)
    return (y * g.astype(jnp.float32)).astype(x.dtype)


def swiglu(h, w_gate, w_up, w_down):
    return jnp.dot(jax.nn.silu(jnp.dot(h, w_gate)) * jnp.dot(h, w_up), w_down)


def pool_mixer(u, hist, start, w_pool, s_pool):
    B, L, _ = u.shape
    full = jnp.concatenate([hist, u], axis=1)
    fullf = full.astype(jnp.float32)
    csum = jnp.pad(jnp.cumsum(fullf, axis=1), ((0, 0), (1, 0), (0, 0)))
    pos = start + jnp.arange(L)
    outs = []
    for g, w in enumerate(POOL_WINDOWS):
        sl = slice(g * POOL_GROUP, (g + 1) * POOL_GROUP)
        win = csum[:, POOL_HIST + 1:POOL_HIST + 1 + L, sl] - csum[:, POOL_HIST + 1 - w:POOL_HIST + 1 - w + L, sl]
        cnt = jnp.minimum(pos + 1, w).astype(jnp.float32)[None, :, None]
        outs.append(win / cnt - fullf[:, POOL_HIST:, sl])
    d = jnp.stack(outs, axis=2).astype(u.dtype)
    y = jnp.einsum('blgc,gcd->blgd', d, w_pool).reshape(B, L, D_POOL) * s_pool
    return y, full[:, -POOL_HIST:]


def fox_block(q, k, v, fq, fk, qpos, kpos):
    s = jnp.einsum('bqhd,bshd->bhqs', q, k).astype(jnp.float32) * (HEAD_DIM ** -0.5)
    bias = jnp.swapaxes(fq, 1, 2)[:, :, :, None] - jnp.swapaxes(fk, 1, 2)[:, :, None, :]
    mask = kpos[None, :] <= qpos[:, None]
    p = jax.nn.softmax(jnp.where(mask, s + bias, -jnp.inf), axis=-1)
    return jnp.einsum('bhqs,bshd->bqhd', p.astype(v.dtype), v)


def fox_prompt(q, k, v, logf):
    B, L, H, Dh = q.shape
    F = jnp.cumsum(logf, axis=1)
    nb = L // Q_BLOCK
    pos = jnp.arange(L)
    qb = jnp.swapaxes(q.reshape(B, nb, Q_BLOCK, H, Dh), 0, 1)
    fb = jnp.swapaxes(F.reshape(B, nb, Q_BLOCK, H), 0, 1)
    pb = pos.reshape(nb, Q_BLOCK)
    ob = lax.map(lambda blk: fox_block(blk[0], k, v, blk[1], F, blk[2], pos), (qb, fb, pb))
    return jnp.swapaxes(ob, 0, 1).reshape(B, L, H, Dh)


def fox_sample(q, k, v, logf, past_k, past_v, past_logf):
    P = past_k.shape[1]
    L = q.shape[1]
    kk = jnp.concatenate([past_k, k], axis=1)
    vv = jnp.concatenate([past_v, v], axis=1)
    F = jnp.cumsum(jnp.concatenate([past_logf.astype(jnp.float32), logf], axis=1), axis=1)
    return fox_block(q, kk, vv, F[:, P:], F, P + jnp.arange(L), jnp.arange(P + L))


def routed_experts(h, w_router, b_router, w_e_gate, w_e_up, w_e_down):
    T, D = h.shape
    scores = jax.nn.sigmoid(jnp.dot(h, w_router).astype(jnp.float32))
    _, idx = lax.top_k(scores + b_router.astype(jnp.float32), TOP_K)
    gate = jnp.take_along_axis(scores, idx, axis=1)
    gate = gate / jnp.sum(gate, axis=1, keepdims=True) * ROUTED_SCALE
    n_slots = T * TOP_K
    flat_e = idx.reshape(n_slots)
    order = jnp.argsort(flat_e)
    e_sorted = flat_e[order]
    tok_sorted = (order // TOP_K).astype(jnp.int32)
    gate_sorted = gate.reshape(n_slots)[order]
    counts = jnp.bincount(flat_e, length=N_EXPERTS)
    padded = (counts + ROW_BLOCK - 1) // ROW_BLOCK * ROW_BLOCK
    pad_end = jnp.cumsum(padded)
    pad_start = pad_end - padded
    grp_start = jnp.cumsum(counts) - counts
    dest = pad_start[e_sorted] + jnp.arange(n_slots) - grp_start[e_sorted]
    n_blocks = -(-n_slots // ROW_BLOCK) + N_EXPERTS
    cap = n_blocks * ROW_BLOCK
    row_tok = jnp.full((cap,), T, jnp.int32).at[dest].set(tok_sorted)
    row_gate = jnp.zeros((cap,), jnp.float32).at[dest].set(gate_sorted)
    blk_e = jnp.minimum(jnp.searchsorted(pad_end, jnp.arange(n_blocks) * ROW_BLOCK, side='right'), N_EXPERTS - 1)
    h_pad = jnp.concatenate([h, jnp.zeros((1, D), h.dtype)], axis=0)

    def block_step(acc, blk):
        toks, g, e = blk
        yb = swiglu(h_pad[toks], w_e_gate[e], w_e_up[e], w_e_down[e]).astype(jnp.float32)
        return acc.at[toks].add(yb * g[:, None]), None

    acc, _ = lax.scan(block_step, jnp.zeros((T + 1, D), jnp.float32),
                      (row_tok.reshape(n_blocks, ROW_BLOCK), row_gate.reshape(n_blocks, ROW_BLOCK), blk_e))
    return acc[:T]


def trunk_layer(x, c, pool_hist, past_k, past_v, past_logf, start,
                w_ada, b_ada, g_mix, w_in, b_f, w_pool, s_pool, w_up_a, w_up_b, w_out,
                g_ffn, w_router, b_router, w_e_gate, w_e_up, w_e_down, w_s_gate, w_s_up, w_s_down):
    B, L, D = x.shape
    mod = jnp.dot(jax.nn.silu(c), w_ada) + b_ada
    sh1, sc1, gt1, sh2, sc2, gt2 = jnp.split(mod[:, None, :], N_MOD, axis=-1)
    h = rms_norm(x, g_mix) * (1 + sc1) + sh1
    z = jnp.dot(h, w_in)
    u, q, k, v, fl, ga, gb = jnp.split(z, IN_SPLITS, axis=-1)
    y_a, new_pool = pool_mixer(u, pool_hist, start, w_pool, s_pool)
    q = q.reshape(B, L, N_HEADS, HEAD_DIM)
    k = k.reshape(B, L, N_HEADS, HEAD_DIM)
    v = v.reshape(B, L, N_HEADS, HEAD_DIM)
    logf = jax.nn.log_sigmoid((fl + b_f).astype(jnp.float32))
    if past_k is None:
        o = fox_prompt(q, k, v, logf)
    else:
        o = fox_sample(q, k, v, logf, past_k, past_v, past_logf)
    y_b = o.reshape(B, L, D_ATT)
    merged = jax.nn.sigmoid(ga) * jnp.dot(y_a, w_up_a) + jax.nn.sigmoid(gb) * jnp.dot(y_b, w_up_b)
    x = x + gt1 * jnp.dot(merged, w_out)
    h2 = (rms_norm(x, g_ffn) * (1 + sc2) + sh2).reshape(B * L, D)
    f = routed_experts(h2, w_router, b_router, w_e_gate, w_e_up, w_e_down).astype(x.dtype) \
        + swiglu(h2, w_s_gate, w_s_up, w_s_down)
    x = x + gt2 * f.reshape(B, L, D)
    return x, new_pool, k, v, logf


def setup_inputs(seed: int = 0) -> dict:
    key = jax.random.key(seed)
    ks = jax.random.split(key, 32)
    f32 = jnp.float32
    D = D_MODEL
    NL = DEPTH

    def nrm(k, shape, scale):
        return jax.random.normal(k, shape, f32) * scale

    return {
        'x_prompt': nrm(ks[0], (BATCH, SEQ, D), 1.0),
        'x_sample': nrm(ks[1], (DEC_BATCH, DEC_SEQ, D), 1.0),
        'cache_pool': nrm(ks[2], (NL, DEC_BATCH, POOL_HIST, D_POOL), 1.0),
        'cache_k': nrm(ks[3], (NL, DEC_BATCH, PAST_LEN, N_HEADS, HEAD_DIM), 1.0),
        'cache_v': nrm(ks[4], (NL, DEC_BATCH, PAST_LEN, N_HEADS, HEAD_DIM), 1.0),
        'cache_logf': jax.nn.log_sigmoid(2.5 + nrm(ks[5], (NL, DEC_BATCH, PAST_LEN, N_HEADS), 1.0)),
        'c_prompt': nrm(ks[6], (BATCH, D), 1.0),
        'c_sample': nrm(ks[7], (DEC_BATCH, D), 1.0),
        'w_ada': nrm(ks[8], (NL, D, N_MOD * D), 0.5 * D ** -0.5),
        'b_ada': nrm(ks[9], (NL, N_MOD * D), 0.01),
        'g_mix': 1.0 + nrm(ks[10], (NL, D), 0.01),
        'w_in': nrm(ks[11], (NL, D, N_IN), D ** -0.5),
        'b_f': jax.random.uniform(ks[12], (NL, N_HEADS), f32, 1.0, 4.0),
        'w_pool': nrm(ks[13], (NL, len(POOL_WINDOWS), POOL_GROUP, POOL_GROUP), POOL_GROUP ** -0.5),
        's_pool': 1.0 + nrm(ks[14], (NL, D_POOL), 0.02),
        'w_up_a': nrm(ks[15], (NL, D_POOL, D), D_POOL ** -0.5),
        'w_up_b': nrm(ks[16], (NL, D_ATT, D), D_ATT ** -0.5),
        'w_out': nrm(ks[17], (NL, D, D), D ** -0.5),
        'g_ffn': 1.0 + nrm(ks[18], (NL, D), 0.01),
        'w_router': nrm(ks[19], (NL, D, N_EXPERTS), D ** -0.5),
        'b_router': nrm(ks[20], (NL, N_EXPERTS), 0.01),
        'w_e_gate': nrm(ks[21], (NL, N_EXPERTS, D, D_EXPERT), D ** -0.5),
        'w_e_up': nrm(ks[22], (NL, N_EXPERTS, D, D_EXPERT), D ** -0.5),
        'w_e_down': nrm(ks[23], (NL, N_EXPERTS, D_EXPERT, D), D_EXPERT ** -0.5),
        'w_s_gate': nrm(ks[24], (NL, D, D_SHARED), D ** -0.5),
        'w_s_up': nrm(ks[25], (NL, D, D_SHARED), D ** -0.5),
        'w_s_down': nrm(ks[26], (NL, D_SHARED, D), D_SHARED ** -0.5),
        'g_final': 1.0 + nrm(ks[27], (D,), 0.01),
    }


def reference(x_prompt, x_sample, cache_pool, cache_k, cache_v, cache_logf, c_prompt, c_sample,
              w_ada, b_ada, g_mix, w_in, b_f, w_pool, s_pool, w_up_a, w_up_b, w_out,
              g_ffn, w_router, b_router, w_e_gate, w_e_up, w_e_down, w_s_gate, w_s_up, w_s_down,
              g_final):
    xp, xs = x_prompt, x_sample
    B = x_prompt.shape[0]
    pool_p, k_p, v_p, f_p = [], [], [], []
    pool_s, k_s, v_s, f_s = [], [], [], []
    for l in range(DEPTH):
        lw = (w_ada[l], b_ada[l], g_mix[l], w_in[l], b_f[l], w_pool[l], s_pool[l], w_up_a[l], w_up_b[l],
              w_out[l], g_ffn[l], w_router[l], b_router[l], w_e_gate[l], w_e_up[l], w_e_down[l],
              w_s_gate[l], w_s_up[l], w_s_down[l])
        hist0 = jnp.zeros((B, POOL_HIST, D_POOL), xp.dtype)
        xp, pp, kp, vp, fp = trunk_layer(xp, c_prompt, hist0, None, None, None, 0, *lw)
        xs, ps, kq, vq, fq = trunk_layer(xs, c_sample, cache_pool[l], cache_k[l], cache_v[l], cache_logf[l],
                                         cache_k.shape[2], *lw)
        pool_p.append(pp); k_p.append(kp); v_p.append(vp); f_p.append(fp)
        pool_s.append(ps); k_s.append(kq); v_s.append(vq); f_s.append(fq)
    y_prompt = rms_norm(xp, g_final)
    y_sample = rms_norm(xs, g_final)
    return (y_prompt, y_sample,
            jnp.stack(pool_p), jnp.stack(k_p), jnp.stack(v_p), jnp.stack(f_p),
            jnp.stack(pool_s), jnp.stack(k_s), jnp.stack(v_s), jnp.stack(f_s))
```

```python
import functools

import jax
import jax.numpy as jnp
from jax import lax
from jax.experimental import pallas as pl
from jax.experimental.pallas import tpu as pltpu

F32 = jnp.float32
BF16 = jnp.bfloat16
I32 = jnp.int32

EPS = 1e-6
N_MOD = 6
POOL_WINDOWS = (2, 4, 8, 16)
POOL_HIST = max(POOL_WINDOWS) - 1
HALO = 16
HEAD_DIM = 128
TOP_K = 8
ROUTED_SCALE = 2.5
MOE_ROWS = 256
MOD_ROWS_PROMPT = 0
MOD_ROWS_SAMPLE = 8
VMEM_LIMIT_BYTES = 52 * 1024 * 1024
NEG_INF = float("-inf")
NT_DIMS = (((1,), (1,)), ((), ()))


def _params(*semantics):
    return pltpu.CompilerParams(dimension_semantics=semantics, vmem_limit_bytes=VMEM_LIMIT_BYTES)


def _silu(x):
    return x * jax.nn.sigmoid(x)


def _ada_kernel(c_ref, w_ref, b_ref, o_ref):
    a = _silu(c_ref[...]).astype(BF16)
    o_ref[...] = jnp.dot(a, w_ref[...].astype(BF16), preferred_element_type=F32) + b_ref[...]


def _ada_mod(c_all, w_ada, b_ada):
    rows, d = c_all.shape
    n = w_ada.shape[1]
    tn = 1024
    return pl.pallas_call(
        _ada_kernel,
        grid=(n // tn,),
        in_specs=[
            pl.BlockSpec((rows, d), lambda j: (0, 0)),
            pl.BlockSpec((d, tn), lambda j: (0, j)),
            pl.BlockSpec((1, tn), lambda j: (0, j)),
        ],
        out_specs=pl.BlockSpec((rows, tn), lambda j: (0, j)),
        out_shape=jax.ShapeDtypeStruct((rows, n), F32),
        compiler_params=_params("parallel"),
        name="ada_mod",
    )(c_all, w_ada, b_ada.reshape(1, n))


def _norm_kernel(x_ref, g_ref, mod_ref, *rest, shift_row, scale_row, with_logf):
    x = x_ref[...]
    ms = jnp.mean(x * x, axis=-1, keepdims=True)
    m = mod_ref[...]
    h = x * lax.rsqrt(ms + EPS) * g_ref[...] * (1.0 + m[scale_row:scale_row + 1, :]) + m[shift_row:shift_row + 1, :]
    hb = h.astype(BF16)
    if with_logf:
        wfl_ref, bf_ref, h_ref, lf_ref = rest
        fl = lax.dot_general(wfl_ref[...], hb, NT_DIMS, preferred_element_type=F32) + bf_ref[...]
        lf_ref[...] = jnp.minimum(fl, 0.0) - jnp.log(1.0 + jnp.exp(-jnp.abs(fl)))
    else:
        (h_ref,) = rest
    h_ref[...] = hb


def _norm_mod(x3, g, mod3, mod_row0, shift_row, scale_row, tl, w_flt=None, b_f=None):
    b, l, d = x3.shape
    with_logf = w_flt is not None
    in_specs = [
        pl.BlockSpec((None, tl, d), lambda bi, li: (bi, li, 0)),
        pl.BlockSpec((1, d), lambda bi, li: (0, 0)),
        pl.BlockSpec((None, N_MOD, d), lambda bi, li: (mod_row0 + bi, 0, 0)),
    ]
    args = [x3, g.reshape(1, d), mod3]
    out_specs = [pl.BlockSpec((None, tl, d), lambda bi, li: (bi, li, 0))]
    out_shape = [jax.ShapeDtypeStruct((b, l, d), BF16)]
    if with_logf:
        nh = w_flt.shape[0]
        in_specs += [pl.BlockSpec((nh, d), lambda bi, li: (0, 0)), pl.BlockSpec((nh, 1), lambda bi, li: (0, 0))]
        args += [w_flt, b_f.reshape(nh, 1)]
        out_specs.append(pl.BlockSpec((None, nh, tl), lambda bi, li: (bi, 0, li)))
        out_shape.append(jax.ShapeDtypeStruct((b, nh, l), F32))
    out = pl.pallas_call(
        functools.partial(_norm_kernel, shift_row=shift_row, scale_row=scale_row, with_logf=with_logf),
        grid=(b, l // tl),
        in_specs=in_specs,
        out_specs=out_specs,
        out_shape=out_shape,
        compiler_params=_params("parallel", "parallel"),
        name="norm_mod_logf" if with_logf else "norm_mod",
    )(*args)
    return out if with_logf else out[0]


def _mm_kernel(*refs, n_extra, epilogue):
    x_ref, w_ref = refs[0], refs[1]
    extras = refs[2:2 + n_extra]
    outs = refs[2 + n_extra:]
    acc = jnp.dot(x_ref[...], w_ref[...], preferred_element_type=F32)
    res = epilogue(acc, *[e[...] for e in extras])
    for o_ref, r in zip(outs, res):
        o_ref[...] = r.astype(o_ref.dtype)


def _mm(x3, w, *, n, w_col0=0, extras=(), epilogue, out_dtypes, tl, tn, w_outer, name):
    b, l, k = x3.shape
    nl, nj = l // tl, n // tn
    j0 = w_col0 // tn
    if w_outer:
        grid = (nj, b, nl)
        wrap = lambda fn: (lambda j, bi, li: fn(bi, li, j))
    else:
        grid = (b, nl, nj)
        wrap = lambda fn: (lambda bi, li, j: fn(bi, li, j))
    in_specs = [
        pl.BlockSpec((None, tl, k), wrap(lambda bi, li, j: (bi, li, 0))),
        pl.BlockSpec((k, tn), wrap(lambda bi, li, j: (0, j0 + j))),
    ]
    args = [x3, w]
    for arr, shape, fn in extras:
        in_specs.append(pl.BlockSpec(shape, wrap(fn)))
        args.append(arr)
    out = pl.pallas_call(
        functools.partial(_mm_kernel, n_extra=len(extras), epilogue=epilogue),
        grid=grid,
        in_specs=in_specs,
        out_specs=[pl.BlockSpec((None, tl, tn), wrap(lambda bi, li, j: (bi, li, j))) for _ in out_dtypes],
        out_shape=[jax.ShapeDtypeStruct((b, l, n), dt) for dt in out_dtypes],
        compiler_params=_params("parallel", "parallel", "parallel"),
        name=name,
    )(*args)
    return out


def _pool_kernel(u_ref, halo_ref, hist_ref, wp_ref, sp_ref, o_ref, full_ref, *, tl, start, gw):
    i = pl.program_id(1)
    full_ref[0:HALO, :] = jnp.where(i == 0, hist_ref[...], halo_ref[...])
    full_ref[HALO:HALO + tl, :] = u_ref[...]
    pos = start + i * tl + lax.broadcasted_iota(I32, (tl, 1), 0)
    for g, w in enumerate(POOL_WINDOWS):
        c0, c1 = g * gw, (g + 1) * gw
        cur = full_ref[HALO:HALO + tl, c0:c1]
        win = cur
        for d in range(1, w):
            win = win + full_ref[HALO - d:HALO - d + tl, c0:c1]
        cnt = jnp.minimum(pos + 1, w).astype(F32)
        dmean = win / cnt - cur
        y = jnp.dot(dmean.astype(BF16), wp_ref[g], preferred_element_type=F32) * sp_ref[:, c0:c1]
        o_ref[:, c0:c1] = y.astype(o_ref.dtype)


def _pool_mixer(u3, hist, w_pool, s_pool, start, tl):
    b, l, dp = u3.shape
    gw = dp // len(POOL_WINDOWS)
    halo_blocks = tl // HALO
    return pl.pallas_call(
        functools.partial(_pool_kernel, tl=tl, start=start, gw=gw),
        grid=(b, l // tl),
        in_specs=[
            pl.BlockSpec((None, tl, dp), lambda bi, li: (bi, li, 0)),
            pl.BlockSpec((None, HALO, dp), lambda bi, li: (bi, jnp.maximum(li * halo_blocks - 1, 0), 0)),
            pl.BlockSpec((None, HALO, dp), lambda bi, li: (bi, 0, 0)),
            pl.BlockSpec((len(POOL_WINDOWS), gw, gw), lambda bi, li: (0, 0, 0)),
            pl.BlockSpec((1, dp), lambda bi, li: (0, 0)),
        ],
        out_specs=pl.BlockSpec((None, tl, dp), lambda bi, li: (bi, li, 0)),
        out_shape=jax.ShapeDtypeStruct((b, l, dp), BF16),
        scratch_shapes=[pltpu.VMEM((HALO + tl, dp), F32)],
        compiler_params=_params("parallel", "parallel"),
        name="pool_mixer",
    )(u3, u3, hist, w_pool, s_pool.reshape(1, dp))


def _cumsum_kernel(x_ref, o_ref):
    x = x_ref[...]
    n = x.shape[1]
    idx = lax.broadcasted_iota(I32, x.shape, 1)
    s = 1
    while s < n:
        x = x + jnp.where(idx >= s, pltpu.roll(x, s, 1), 0.0)
        s *= 2
    o_ref[...] = x


def _cumsum_last(x3):
    b, h, n = x3.shape
    return pl.pallas_call(
        _cumsum_kernel,
        grid=(b,),
        in_specs=[pl.BlockSpec((None, h, n), lambda bi: (bi, 0, 0))],
        out_specs=pl.BlockSpec((None, h, n), lambda bi: (bi, 0, 0)),
        out_shape=jax.ShapeDtypeStruct((b, h, n), F32),
        compiler_params=_params("parallel"),
        name="cumsum_logf",
    )(x3)


def _softmax_step(s, v, carry):
    m, l, acc = carry
    m_new = jnp.maximum(m, jnp.max(s, axis=1, keepdims=True))
    alpha = jnp.exp(m - m_new)
    p = jnp.exp(s - m_new)
    l = alpha * l + jnp.sum(p, axis=1, keepdims=True)
    acc = alpha * acc + jnp.dot(p.astype(BF16), v, preferred_element_type=F32)
    return m_new, l, acc


def _attn_prompt_kernel(q_ref, k_ref, v_ref, fk_ref, o_ref, *, tq):
    qi = pl.program_id(2)
    q = q_ref[...]

    def scores(j):
        off = pl.multiple_of(j * tq, tq)
        s = lax.dot_general(q, k_ref[pl.ds(off, tq), :], NT_DIMS, preferred_element_type=F32)
        return s - fk_ref[j], v_ref[pl.ds(off, tq), :]

    def body(j, carry):
        s, v = scores(j)
        return _softmax_step(s, v, carry)

    init = (jnp.full((tq, 1), NEG_INF, F32), jnp.zeros((tq, 1), F32), jnp.zeros((tq, HEAD_DIM), F32))
    carry = lax.fori_loop(0, qi, body, init)
    s, v = scores(qi)
    row = lax.broadcasted_iota(I32, (tq, tq), 0)
    col = lax.broadcasted_iota(I32, (tq, tq), 1)
    _, l, acc = _softmax_step(jnp.where(col <= row, s, NEG_INF), v, carry)
    o_ref[...] = (acc / l).astype(o_ref.dtype)


def _attn_prompt(q3, k3, v3, f3, tq):
    b, l, da = q3.shape
    nh = da // HEAD_DIM
    nk = l // tq
    fk = f3.reshape(b * nh, nk, 1, tq)
    return pl.pallas_call(
        functools.partial(_attn_prompt_kernel, tq=tq),
        grid=(b, nh, nk),
        in_specs=[
            pl.BlockSpec((None, tq, HEAD_DIM), lambda bi, hi, qi: (bi, qi, hi)),
            pl.BlockSpec((None, l, HEAD_DIM), lambda bi, hi, qi: (bi, 0, hi)),
            pl.BlockSpec((None, l, HEAD_DIM), lambda bi, hi, qi: (bi, 0, hi)),
            pl.BlockSpec((None, nk, 1, tq), lambda bi, hi, qi: (bi * nh + hi, 0, 0, 0)),
        ],
        out_specs=pl.BlockSpec((None, tq, HEAD_DIM), lambda bi, hi, qi: (bi, qi, hi)),
        out_shape=jax.ShapeDtypeStruct((b, l, da), BF16),
        compiler_params=_params("parallel", "parallel", "parallel"),
        name="fox_prompt",
    )(q3, k3, v3, fk)


def _attn_sample_kernel(q_ref, kp_ref, vp_ref, kn_ref, vn_ref, f_ref, o_ref, *, past, lq):
    q = q_ref[...]
    f = f_ref[...]
    s1 = lax.dot_general(q, kp_ref[...].astype(BF16), NT_DIMS, preferred_element_type=F32) - f[:, :past]
    s2 = lax.dot_general(q, kn_ref[...], NT_DIMS, preferred_element_type=F32) - f[:, past:past + lq]
    row = lax.broadcasted_iota(I32, (lq, lq), 0)
    col = lax.broadcasted_iota(I32, (lq, lq), 1)
    s2 = jnp.where(col <= row, s2, NEG_INF)
    m = jnp.maximum(jnp.max(s1, axis=1, keepdims=True), jnp.max(s2, axis=1, keepdims=True))
    p1 = jnp.exp(s1 - m)
    p2 = jnp.exp(s2 - m)
    l = jnp.sum(p1, axis=1, keepdims=True) + jnp.sum(p2, axis=1, keepdims=True)
    acc = jnp.dot(p1.astype(BF16), vp_ref[...].astype(BF16), preferred_element_type=F32)
    acc = acc + jnp.dot(p2.astype(BF16), vn_ref[...], preferred_element_type=F32)
    o_ref[...] = (acc / l).astype(o_ref.dtype)


def _attn_sample(q3, kn3, vn3, kp3, vp3, f3):
    b, lq, da = q3.shape
    past = kp3.shape[1]
    nh = da // HEAD_DIM
    nf = f3.shape[2]
    new_spec = pl.BlockSpec((None, lq, HEAD_DIM), lambda bi, hi: (bi, 0, hi))
    past_spec = pl.BlockSpec((None, past, HEAD_DIM), lambda bi, hi: (bi, 0, hi))
    return pl.pallas_call(
        functools.partial(_attn_sample_kernel, past=past, lq=lq),
        grid=(b, nh),
        in_specs=[new_spec, past_spec, past_spec, new_spec, new_spec,
                  pl.BlockSpec((None, 1, nf), lambda bi, hi: (bi * nh + hi, 0, 0))],
        out_specs=new_spec,
        out_shape=jax.ShapeDtypeStruct((b, lq, da), BF16),
        compiler_params=_params("parallel", "parallel"),
        name="fox_sample",
    )(q3, kp3, vp3, kn3, vn3, f3.reshape(b * nh, 1, nf))


def _router_kernel(h_ref, wr_ref, br_ref, tri_ref, idx_ref, gate_ref, rank_ref, cnt_ref, carry_ref):
    @pl.when(pl.program_id(0) == 0)
    def _():
        carry_ref[...] = jnp.zeros_like(carry_ref)

    n_exp, tm = wr_ref.shape[0], h_ref.shape[0]
    scores = jax.nn.sigmoid(lax.dot_general(wr_ref[...], h_ref[...], NT_DIMS, preferred_element_type=F32))
    work = scores + br_ref[...]
    eidx = lax.broadcasted_iota(I32, (n_exp, tm), 0)
    picked = jnp.zeros((n_exp, tm), F32)
    onehots, gates = [], []
    for k in range(TOP_K):
        best = jnp.max(work, axis=0, keepdims=True)
        first = jnp.min(jnp.where(work == best, eidx, n_exp), axis=0, keepdims=True)
        onehot = eidx == first
        idx_ref[k:k + 1, :] = first
        gates.append(jnp.sum(jnp.where(onehot, scores, 0.0), axis=0, keepdims=True))
        onehots.append(onehot)
        picked = picked + jnp.where(onehot, 1.0, 0.0)
        work = jnp.where(onehot, NEG_INF, work)
    norm = ROUTED_SCALE / functools.reduce(lambda a, c: a + c, gates)
    cum = jnp.dot(picked.astype(BF16), tri_ref[...], preferred_element_type=F32)
    before = cum - picked + carry_ref[:, 0:1]
    for k in range(TOP_K):
        gate_ref[k:k + 1, :] = gates[k] * norm
        rank_ref[k:k + 1, :] = jnp.sum(jnp.where(onehots[k], before, 0.0), axis=0, keepdims=True).astype(I32)
    carry_ref[...] = carry_ref[...] + cum[:, tm - 1:tm]
    cnt_ref[...] = carry_ref[...]


def _router(h2, w_router_t, b_router, tm):
    t, d = h2.shape
    n_exp = w_router_t.shape[0]
    tri = jnp.triu(jnp.ones((tm, tm), BF16))
    tok_spec = pl.BlockSpec((TOP_K, tm), lambda i: (0, i))
    idx_t, gate_t, rank_t, cnt = pl.pallas_call(
        _router_kernel,
        grid=(t // tm,),
        in_specs=[
            pl.BlockSpec((tm, d), lambda i: (i, 0)),
            pl.BlockSpec((n_exp, d), lambda i: (0, 0)),
            pl.BlockSpec((n_exp, 1), lambda i: (0, 0)),
            pl.BlockSpec((tm, tm), lambda i: (0, 0)),
        ],
        out_specs=[tok_spec, tok_spec, tok_spec, pl.BlockSpec((n_exp, 128), lambda i: (0, 0))],
        out_shape=[
            jax.ShapeDtypeStruct((TOP_K, t), I32),
            jax.ShapeDtypeStruct((TOP_K, t), F32),
            jax.ShapeDtypeStruct((TOP_K, t), I32),
            jax.ShapeDtypeStruct((n_exp, 128), F32),
        ],
        scratch_shapes=[pltpu.VMEM((n_exp, 128), F32)],
        compiler_params=_params("arbitrary"),
        name="moe_router",
    )(h2, w_router_t, b_router.reshape(n_exp, 1), tri)
    return idx_t, gate_t, rank_t, cnt[:, 0].astype(I32)


def _moe_kernel(be_ref, nu_ref, x_ref, wg_ref, wu_ref, wd_ref, y_ref):
    @pl.when(pl.program_id(0) < nu_ref[0])
    def _():
        x = x_ref[...]
        g = jnp.dot(x, wg_ref[...], preferred_element_type=F32)
        u = jnp.dot(x, wu_ref[...], preferred_element_type=F32)
        a = (_silu(g) * u).astype(BF16)
        y_ref[...] = jnp.dot(a, wd_ref[...], preferred_element_type=F32).astype(y_ref.dtype)


def _moe_experts(x_sorted, blk_e, n_used, w_e_gate, w_e_up, w_e_down):
    cap, d = x_sorted.shape
    de = w_e_gate.shape[2]
    n_blocks = cap // MOE_ROWS
    row_map = lambda i, be, nu: (jnp.minimum(i, nu[0] - 1), 0)
    w_map = lambda i, be, nu: (be[i], 0, 0)
    return pl.pallas_call(
        _moe_kernel,
        grid_spec=pltpu.PrefetchScalarGridSpec(
            num_scalar_prefetch=2,
            grid=(n_blocks,),
            in_specs=[
                pl.BlockSpec((MOE_ROWS, d), row_map),
                pl.BlockSpec((None, d, de), w_map),
                pl.BlockSpec((None, d, de), w_map),
                pl.BlockSpec((None, de, d), w_map),
            ],
            out_specs=pl.BlockSpec((MOE_ROWS, d), row_map),
        ),
        out_shape=jax.ShapeDtypeStruct((cap, d), BF16),
        compiler_params=_params("arbitrary"),
        name="moe_experts",
    )(blk_e, n_used, x_sorted, w_e_gate, w_e_up, w_e_down)


def _final_kernel(x1_ref, h2_ref, yt_ref, gate_ref, mod_ref, wsg_ref, wsu_ref, wsd_ref, gf_ref, o_ref):
    h2 = h2_ref[...]
    g = jnp.dot(h2, wsg_ref[...], preferred_element_type=F32)
    u = jnp.dot(h2, wsu_ref[...], preferred_element_type=F32)
    f = jnp.dot((_silu(g) * u).astype(BF16), wsd_ref[...], preferred_element_type=F32)
    gate = gate_ref[...]
    for k in range(TOP_K):
        f = f + gate[:, k:k + 1] * yt_ref[k].astype(F32)
    x2 = x1_ref[...] + mod_ref[N_MOD - 1:N_MOD, :] * f
    ms = jnp.mean(x2 * x2, axis=-1, keepdims=True)
    o_ref[...] = x2 * lax.rsqrt(ms + EPS) * gf_ref[...]


def _final(x1, h2, y_tok, gate, mod3, mod_row0, w_s_gate, w_s_up, w_s_down, g_final, tl):
    b, l, d = x1.shape
    ds = w_s_gate.shape[1]
    tile = pl.BlockSpec((None, tl, d), lambda bi, li: (bi, li, 0))
    const = lambda shape: pl.BlockSpec(shape, lambda bi, li: (0,) * len(shape))
    return pl.pallas_call(
        _final_kernel,
        grid=(b, l // tl),
        in_specs=[
            tile,
            tile,
            pl.BlockSpec((TOP_K, None, tl, d), lambda bi, li: (0, bi, li, 0)),
            pl.BlockSpec((None, tl, TOP_K), lambda bi, li: (bi, li, 0)),
            pl.BlockSpec((None, N_MOD, d), lambda bi, li: (mod_row0 + bi, 0, 0)),
            const((d, ds)),
            const((d, ds)),
            const((ds, d)),
            const((1, d)),
        ],
        out_specs=tile,
        out_shape=jax.ShapeDtypeStruct((b, l, d), F32),
        compiler_params=_params("parallel", "parallel"),
        name="combine_final",
    )(x1, h2, y_tok, gate, mod3, w_s_gate, w_s_up, w_s_down, g_final.reshape(1, d))


def _group_forward(x3, mod3, mod_row0, hist, start, past, wts, g_final, tiles):
    b, l, d = x3.shape
    tl, t_route, w_outer = tiles["tl"], tiles["t_route"], tiles["w_outer"]
    dp = wts["w_pool"].shape[0] * wts["w_pool"].shape[1]
    da = wts["w_up_b"].shape[0]
    nh = da // HEAD_DIM
    t = b * l

    h, logf_t = _norm_mod(x3, wts["g_mix"], mod3, mod_row0, 0, 1, tl, wts["w_fl_t"], wts["b_f"])

    ident = lambda acc: (acc,)
    mm = functools.partial(_mm, tl=tl, tn=1024, w_outer=w_outer)
    (u,) = mm(h, wts["w_main"], n=dp, w_col0=0, epilogue=ident, out_dtypes=(F32,), name="in_proj_u")
    scale = HEAD_DIM ** -0.5
    (qb,) = mm(h, wts["w_main"], n=da, w_col0=dp, epilogue=lambda acc: (acc * scale,), out_dtypes=(BF16,),
               name="in_proj_q")
    dup = lambda acc: (acc, acc)
    k, kb = mm(h, wts["w_main"], n=da, w_col0=dp + da, epilogue=dup, out_dtypes=(F32, BF16), name="in_proj_k")
    v, vb = mm(h, wts["w_main"], n=da, w_col0=dp + 2 * da, epilogue=dup, out_dtypes=(F32, BF16), name="in_proj_v")
    (gates,) = mm(h, wts["w_gates"], n=2 * d, epilogue=lambda acc: (jax.nn.sigmoid(acc),), out_dtypes=(BF16,),
                  name="in_proj_gates")

    y_a = _pool_mixer(u, hist, wts["w_pool"], wts["s_pool"], start, tl)

    if past is None:
        y_b = _attn_prompt(qb, kb, vb, _cumsum_last(logf_t), tiles["tq"])
    else:
        cache_k, cache_v, cache_logf = past
        p = cache_k.shape[1]
        nf = -(-(p + l) // 128) * 128
        lf_all = jnp.concatenate([jnp.swapaxes(cache_logf.astype(F32), 1, 2), logf_t,
                                  jnp.zeros((b, nh, nf - p - l), F32)], axis=2)
        y_b = _attn_sample(qb, kb, vb, cache_k.reshape(b, p, da), cache_v.reshape(b, p, da), _cumsum_last(lf_all))

    tile_n = lambda off: (lambda bi, li, j: (bi, li, off + j))
    (m_a,) = mm(y_a, wts["w_up_a"], n=d, extras=[(gates, (None, tl, 1024), tile_n(0))],
                epilogue=lambda acc, ga: (ga.astype(F32) * acc,), out_dtypes=(BF16,), name="up_a")
    (merged,) = mm(y_b, wts["w_up_b"], n=d,
                   extras=[(gates, (None, tl, 1024), tile_n(d // 1024)), (m_a, (None, tl, 1024), tile_n(0))],
                   epilogue=lambda acc, gb, ma: (gb.astype(F32) * acc + ma.astype(F32),), out_dtypes=(BF16,),
                   name="up_b")
    mod_spec = (mod3, (None, N_MOD, 1024), lambda bi, li, j: (mod_row0 + bi, 0, j))
    (x1,) = mm(merged, wts["w_out"], n=d, extras=[(x3, (None, tl, 1024), tile_n(0)), mod_spec],
               epilogue=lambda acc, xr, m: (xr + m[2:3, :] * acc,), out_dtypes=(F32,), name="out_proj")

    h2 = _norm_mod(x1, wts["g_ffn"], mod3, mod_row0, 3, 4, tl)
    h2_flat = h2.reshape(t, d)
    idx_t, gate_t, rank_t, counts = _router(h2_flat, wts["w_router_t"], wts["b_router"], t_route)

    n_exp = counts.shape[0]
    n_slots = t * TOP_K
    n_blocks = n_slots // MOE_ROWS + n_exp
    padded = (counts + MOE_ROWS - 1) // MOE_ROWS * MOE_ROWS
    pad_end = jnp.cumsum(padded)
    pad_start = pad_end - padded
    dest = pad_start[idx_t] + rank_t
    blk_e = jnp.minimum(jnp.searchsorted(pad_end, jnp.arange(n_blocks, dtype=I32) * MOE_ROWS, side="right"),
                        n_exp - 1).astype(I32)
    n_used = (pad_end[-1:] // MOE_ROWS).astype(I32)

    row_tok = jnp.zeros((n_blocks * MOE_ROWS,), I32).at[dest.reshape(-1)].set(
        jnp.tile(jnp.arange(t, dtype=I32), TOP_K))
    x_sorted = h2_flat[row_tok]
    y_sorted = _moe_experts(x_sorted, blk_e, n_used, wts["w_e_gate"], wts["w_e_up"], wts["w_e_down"])
    y_tok = y_sorted[dest].reshape(TOP_K, b, l, d)

    y = _final(x1, h2, y_tok, gate_t.T.reshape(b, l, TOP_K), mod3, mod_row0,
               wts["w_s_gate"], wts["w_s_up"], wts["w_s_down"], g_final, tiles["tl_final"])

    new_pool = u[:, l - POOL_HIST:, :]
    return y, new_pool, k.reshape(b, l, nh, HEAD_DIM), v.reshape(b, l, nh, HEAD_DIM), jnp.swapaxes(logf_t, 1, 2)


def kernel(x_prompt, x_sample, cache_pool, cache_k, cache_v, cache_logf, c_prompt, c_sample, w_ada, b_ada, g_mix, w_in, b_f, w_pool, s_pool, w_up_a, w_up_b, w_out, g_ffn, w_router, b_router, w_e_gate, w_e_up, w_e_down, w_s_gate, w_s_up, w_s_down, g_final):
    depth = w_ada.shape[0]
    assert depth == 1, "a single layer is supported"
    bp, lp, d = x_prompt.shape
    bs, ls, _ = x_sample.shape
    dp = w_pool.shape[1] * w_pool.shape[2]
    da = w_up_b.shape[1]
    nh = da // HEAD_DIM
    assert bp <= MOD_ROWS_SAMPLE

    lyr = 0
    c_all = jnp.zeros((MOD_ROWS_SAMPLE + bs, d), F32).at[:bp].set(c_prompt).at[MOD_ROWS_SAMPLE:].set(c_sample)
    mod3 = _ada_mod(c_all, w_ada[lyr], b_ada[lyr]).reshape(c_all.shape[0], N_MOD, d)

    n_main = dp + 3 * da
    wts = dict(
        g_mix=g_mix[lyr], g_ffn=g_ffn[lyr], b_f=b_f[lyr], s_pool=s_pool[lyr], b_router=b_router[lyr],
        w_main=w_in[lyr][:, :n_main].astype(BF16),
        w_fl_t=w_in[lyr][:, n_main:n_main + nh].T.astype(BF16),
        w_gates=w_in[lyr][:, n_main + nh:].astype(BF16),
        w_pool=w_pool[lyr].astype(BF16),
        w_up_a=w_up_a[lyr].astype(BF16), w_up_b=w_up_b[lyr].astype(BF16), w_out=w_out[lyr].astype(BF16),
        w_router_t=w_router[lyr].T.astype(BF16),
        w_e_gate=w_e_gate[lyr].astype(BF16), w_e_up=w_e_up[lyr].astype(BF16), w_e_down=w_e_down[lyr].astype(BF16),
        w_s_gate=w_s_gate[lyr].astype(BF16), w_s_up=w_s_up[lyr].astype(BF16), w_s_down=w_s_down[lyr].astype(BF16),
    )

    tiles_p = dict(tl=min(512, lp), tq=min(512, lp), t_route=min(512, bp * lp), w_outer=False,
                   tl_final=min(256, lp))
    tiles_s = dict(tl=ls, t_route=bs * ls, w_outer=True, tl_final=ls)

    hist_p = jnp.zeros((bp, HALO, dp), F32)
    y_p, pool_p, k_p, v_p, f_p = _group_forward(x_prompt, mod3, MOD_ROWS_PROMPT, hist_p, 0, None, wts, g_final,
                                                tiles_p)
    hist_s = jnp.pad(cache_pool[lyr], ((0, 0), (HALO - POOL_HIST, 0), (0, 0)))
    past = (cache_k[lyr], cache_v[lyr], cache_logf[lyr])
    y_s, pool_s, k_s, v_s, f_s = _group_forward(x_sample, mod3, MOD_ROWS_SAMPLE, hist_s, cache_k.shape[2], past,
                                                wts, g_final, tiles_s)
    stack = lambda a: a[None]
    return (y_p, y_s, stack(pool_p), stack(k_p), stack(v_p), stack(f_p),
            stack(pool_s), stack(k_s), stack(v_s), stack(f_s))
```

```python
import functools

import jax
import jax.numpy as jnp
from jax import lax
from jax.experimental import pallas as pl
from jax.experimental.pallas import tpu as pltpu
from jax.experimental.pallas import tpu_sc as plsc

F32 = jnp.float32
BF16 = jnp.bfloat16
I32 = jnp.int32
U32 = jnp.uint32

EPS = 1e-6
N_MOD = 6
POOL_WINDOWS = (2, 4, 8, 16)
POOL_HIST = max(POOL_WINDOWS) - 1
HALO = 16
HEAD_DIM = 128
TOP_K = 8
ROUTED_SCALE = 2.5
MOE_ROWS = 256
MOD_ROWS_PROMPT = 0
MOD_ROWS_SAMPLE = 8
VMEM_LIMIT_BYTES = 52 * 1024 * 1024
NEG_INF = float("-inf")
LANES = 128
PACK_CHUNKS = 8
SC_CORES = 2
SC_SUBCORES = 16
SC_WORKERS = SC_CORES * SC_SUBCORES
SC_WINDOW = 32
NT_DIMS = (((1,), (1,)), ((), ()))


def _params(*semantics):
    return pltpu.CompilerParams(dimension_semantics=semantics, vmem_limit_bytes=VMEM_LIMIT_BYTES)


def _silu(x):
    return x * jax.nn.sigmoid(x)


def _store_packed(ref, x):
    rows, width = x.shape
    half = width // 2
    bits = pltpu.bitcast(x.astype(BF16).astype(F32), U32)
    packed = (bits[:, :half] >> 16) | (bits[:, half:] & jnp.uint32(0xFFFF0000))
    for j in range(PACK_CHUNKS):
        ref[pl.ds(j, rows, stride=PACK_CHUNKS), :] = packed[:, j * LANES:(j + 1) * LANES]


def _load_packed(ref, rows, valid=None):
    lo, hi = [], []
    for j in range(PACK_CHUNKS):
        p = ref[pl.ds(j, rows, stride=PACK_CHUNKS), :]
        if valid is not None:
            p = jnp.where(lax.broadcasted_iota(I32, p.shape, 0) < valid, p, jnp.uint32(0))
        lo.append(pltpu.bitcast(p << 16, F32))
        hi.append(pltpu.bitcast(p & jnp.uint32(0xFFFF0000), F32))
    return jnp.concatenate(lo + hi, axis=1)


def _ada_kernel(c_ref, w_ref, b_ref, o_ref):
    a = _silu(c_ref[...]).astype(BF16)
    o_ref[...] = jnp.dot(a, w_ref[...].astype(BF16), preferred_element_type=F32) + b_ref[...]


def _ada_mod(c_all, w_ada, b_ada):
    rows, d = c_all.shape
    n = w_ada.shape[1]
    tn = 1024
    return pl.pallas_call(
        _ada_kernel,
        grid=(n // tn,),
        in_specs=[
            pl.BlockSpec((rows, d), lambda j: (0, 0)),
            pl.BlockSpec((d, tn), lambda j: (0, j)),
            pl.BlockSpec((1, tn), lambda j: (0, j)),
        ],
        out_specs=pl.BlockSpec((rows, tn), lambda j: (0, j)),
        out_shape=jax.ShapeDtypeStruct((rows, n), F32),
        compiler_params=_params("parallel"),
        name="ada_mod",
    )(c_all, w_ada, b_ada.reshape(1, n))


def _norm_kernel(x_ref, g_ref, mod_ref, *rest, shift_row, scale_row, with_logf, with_packed):
    x = x_ref[...]
    ms = jnp.mean(x * x, axis=-1, keepdims=True)
    m = mod_ref[...]
    h = x * lax.rsqrt(ms + EPS) * g_ref[...] * (1.0 + m[scale_row:scale_row + 1, :]) + m[shift_row:shift_row + 1, :]
    hb = h.astype(BF16)
    if with_logf:
        wfl_ref, bf_ref, h_ref, lf_ref = rest
        fl = lax.dot_general(wfl_ref[...], hb, NT_DIMS, preferred_element_type=F32) + bf_ref[...]
        lf_ref[...] = jnp.minimum(fl, 0.0) - jnp.log(1.0 + jnp.exp(-jnp.abs(fl)))
    elif with_packed:
        h_ref, hp_ref = rest
        _store_packed(hp_ref, h)
    else:
        (h_ref,) = rest
    h_ref[...] = hb


def _norm_mod(x3, g, mod3, mod_row0, shift_row, scale_row, tl, w_flt=None, b_f=None, with_packed=False):
    b, l, d = x3.shape
    with_logf = w_flt is not None
    in_specs = [
        pl.BlockSpec((None, tl, d), lambda bi, li: (bi, li, 0)),
        pl.BlockSpec((1, d), lambda bi, li: (0, 0)),
        pl.BlockSpec((None, N_MOD, d), lambda bi, li: (mod_row0 + bi, 0, 0)),
    ]
    args = [x3, g.reshape(1, d), mod3]
    out_specs = [pl.BlockSpec((None, tl, d), lambda bi, li: (bi, li, 0))]
    out_shape = [jax.ShapeDtypeStruct((b, l, d), BF16)]
    if with_logf:
        nh = w_flt.shape[0]
        in_specs += [pl.BlockSpec((nh, d), lambda bi, li: (0, 0)), pl.BlockSpec((nh, 1), lambda bi, li: (0, 0))]
        args += [w_flt, b_f.reshape(nh, 1)]
        out_specs.append(pl.BlockSpec((None, nh, tl), lambda bi, li: (bi, 0, li)))
        out_shape.append(jax.ShapeDtypeStruct((b, nh, l), F32))
    if with_packed:
        n_l = l // tl
        out_specs.append(pl.BlockSpec((tl * PACK_CHUNKS, LANES), lambda bi, li: (bi * n_l + li, 0)))
        out_shape.append(jax.ShapeDtypeStruct((b * l * PACK_CHUNKS, LANES), U32))
    out = pl.pallas_call(
        functools.partial(_norm_kernel, shift_row=shift_row, scale_row=scale_row, with_logf=with_logf,
                          with_packed=with_packed),
        grid=(b, l // tl),
        in_specs=in_specs,
        out_specs=out_specs,
        out_shape=out_shape,
        compiler_params=_params("parallel", "parallel"),
        name="norm_mod_logf" if with_logf else "norm_mod",
    )(*args)
    return out if (with_logf or with_packed) else out[0]


def _mm_kernel(*refs, n_extra, epilogue):
    x_ref, w_ref = refs[0], refs[1]
    extras = refs[2:2 + n_extra]
    outs = refs[2 + n_extra:]
    acc = jnp.dot(x_ref[...], w_ref[...], preferred_element_type=F32)
    res = epilogue(acc, *[e[...] for e in extras])
    for o_ref, r in zip(outs, res):
        o_ref[...] = r.astype(o_ref.dtype)


def _mm(x3, w, *, n, w_col0=0, extras=(), epilogue, out_dtypes, tl, tn, w_outer, name):
    b, l, k = x3.shape
    nl, nj = l // tl, n // tn
    j0 = w_col0 // tn
    if w_outer:
        grid = (nj, b, nl)
        wrap = lambda fn: (lambda j, bi, li: fn(bi, li, j))
    else:
        grid = (b, nl, nj)
        wrap = lambda fn: (lambda bi, li, j: fn(bi, li, j))
    in_specs = [
        pl.BlockSpec((None, tl, k), wrap(lambda bi, li, j: (bi, li, 0))),
        pl.BlockSpec((k, tn), wrap(lambda bi, li, j: (0, j0 + j))),
    ]
    args = [x3, w]
    for arr, shape, fn in extras:
        in_specs.append(pl.BlockSpec(shape, wrap(fn)))
        args.append(arr)
    out = pl.pallas_call(
        functools.partial(_mm_kernel, n_extra=len(extras), epilogue=epilogue),
        grid=grid,
        in_specs=in_specs,
        out_specs=[pl.BlockSpec((None, tl, tn), wrap(lambda bi, li, j: (bi, li, j))) for _ in out_dtypes],
        out_shape=[jax.ShapeDtypeStruct((b, l, n), dt) for dt in out_dtypes],
        compiler_params=_params("parallel", "parallel", "parallel"),
        name=name,
    )(*args)
    return out


def _pool_kernel(u_ref, halo_ref, hist_ref, wp_ref, sp_ref, o_ref, full_ref, *, tl, start, gw):
    i = pl.program_id(1)
    full_ref[0:HALO, :] = jnp.where(i == 0, hist_ref[...], halo_ref[...])
    full_ref[HALO:HALO + tl, :] = u_ref[...]
    pos = start + i * tl + lax.broadcasted_iota(I32, (tl, 1), 0)
    for g, w in enumerate(POOL_WINDOWS):
        c0, c1 = g * gw, (g + 1) * gw
        cur = full_ref[HALO:HALO + tl, c0:c1]
        win = cur
        for d in range(1, w):
            win = win + full_ref[HALO - d:HALO - d + tl, c0:c1]
        cnt = jnp.minimum(pos + 1, w).astype(F32)
        dmean = win / cnt - cur
        y = jnp.dot(dmean.astype(BF16), wp_ref[g], preferred_element_type=F32) * sp_ref[:, c0:c1]
        o_ref[:, c0:c1] = y.astype(o_ref.dtype)


def _pool_mixer(u3, hist, w_pool, s_pool, start, tl):
    b, l, dp = u3.shape
    gw = dp // len(POOL_WINDOWS)
    halo_blocks = tl // HALO
    return pl.pallas_call(
        functools.partial(_pool_kernel, tl=tl, start=start, gw=gw),
        grid=(b, l // tl),
        in_specs=[
            pl.BlockSpec((None, tl, dp), lambda bi, li: (bi, li, 0)),
            pl.BlockSpec((None, HALO, dp), lambda bi, li: (bi, jnp.maximum(li * halo_blocks - 1, 0), 0)),
            pl.BlockSpec((None, HALO, dp), lambda bi, li: (bi, 0, 0)),
            pl.BlockSpec((len(POOL_WINDOWS), gw, gw), lambda bi, li: (0, 0, 0)),
            pl.BlockSpec((1, dp), lambda bi, li: (0, 0)),
        ],
        out_specs=pl.BlockSpec((None, tl, dp), lambda bi, li: (bi, li, 0)),
        out_shape=jax.ShapeDtypeStruct((b, l, dp), BF16),
        scratch_shapes=[pltpu.VMEM((HALO + tl, dp), F32)],
        compiler_params=_params("parallel", "parallel"),
        name="pool_mixer",
    )(u3, u3, hist, w_pool, s_pool.reshape(1, dp))


def _cumsum_kernel(x_ref, o_ref):
    x = x_ref[...]
    n = x.shape[1]
    idx = lax.broadcasted_iota(I32, x.shape, 1)
    s = 1
    while s < n:
        x = x + jnp.where(idx >= s, pltpu.roll(x, s, 1), 0.0)
        s *= 2
    o_ref[...] = x


def _cumsum_last(x3):
    b, h, n = x3.shape
    return pl.pallas_call(
        _cumsum_kernel,
        grid=(b,),
        in_specs=[pl.BlockSpec((None, h, n), lambda bi: (bi, 0, 0))],
        out_specs=pl.BlockSpec((None, h, n), lambda bi: (bi, 0, 0)),
        out_shape=jax.ShapeDtypeStruct((b, h, n), F32),
        compiler_params=_params("parallel"),
        name="cumsum_logf",
    )(x3)


def _softmax_step(s, v, carry):
    m, l, acc = carry
    m_new = jnp.maximum(m, jnp.max(s, axis=1, keepdims=True))
    alpha = jnp.exp(m - m_new)
    p = jnp.exp(s - m_new)
    l = alpha * l + jnp.sum(p, axis=1, keepdims=True)
    acc = alpha * acc + jnp.dot(p.astype(BF16), v, preferred_element_type=F32)
    return m_new, l, acc


def _attn_prompt_kernel(q_ref, k_ref, v_ref, fk_ref, o_ref, *, tq):
    qi = pl.program_id(2)
    q = q_ref[...]

    def scores(j):
        off = pl.multiple_of(j * tq, tq)
        s = lax.dot_general(q, k_ref[pl.ds(off, tq), :], NT_DIMS, preferred_element_type=F32)
        return s - fk_ref[j], v_ref[pl.ds(off, tq), :]

    def body(j, carry):
        s, v = scores(j)
        return _softmax_step(s, v, carry)

    init = (jnp.full((tq, 1), NEG_INF, F32), jnp.zeros((tq, 1), F32), jnp.zeros((tq, HEAD_DIM), F32))
    carry = lax.fori_loop(0, qi, body, init)
    s, v = scores(qi)
    row = lax.broadcasted_iota(I32, (tq, tq), 0)
    col = lax.broadcasted_iota(I32, (tq, tq), 1)
    _, l, acc = _softmax_step(jnp.where(col <= row, s, NEG_INF), v, carry)
    o_ref[...] = (acc / l).astype(o_ref.dtype)


def _attn_prompt(q3, k3, v3, f3, tq):
    b, l, da = q3.shape
    nh = da // HEAD_DIM
    nk = l // tq
    fk = f3.reshape(b * nh, nk, 1, tq)
    return pl.pallas_call(
        functools.partial(_attn_prompt_kernel, tq=tq),
        grid=(b, nh, nk),
        in_specs=[
            pl.BlockSpec((None, tq, HEAD_DIM), lambda bi, hi, qi: (bi, qi, hi)),
            pl.BlockSpec((None, l, HEAD_DIM), lambda bi, hi, qi: (bi, 0, hi)),
            pl.BlockSpec((None, l, HEAD_DIM), lambda bi, hi, qi: (bi, 0, hi)),
            pl.BlockSpec((None, nk, 1, tq), lambda bi, hi, qi: (bi * nh + hi, 0, 0, 0)),
        ],
        out_specs=pl.BlockSpec((None, tq, HEAD_DIM), lambda bi, hi, qi: (bi, qi, hi)),
        out_shape=jax.ShapeDtypeStruct((b, l, da), BF16),
        compiler_params=_params("parallel", "parallel", "parallel"),
        name="fox_prompt",
    )(q3, k3, v3, fk)


def _attn_sample_kernel(q_ref, kp_ref, vp_ref, kn_ref, vn_ref, f_ref, o_ref, *, past, lq):
    q = q_ref[...]
    f = f_ref[...]
    s1 = lax.dot_general(q, kp_ref[...].astype(BF16), NT_DIMS, preferred_element_type=F32) - f[:, :past]
    s2 = lax.dot_general(q, kn_ref[...], NT_DIMS, preferred_element_type=F32) - f[:, past:past + lq]
    row = lax.broadcasted_iota(I32, (lq, lq), 0)
    col = lax.broadcasted_iota(I32, (lq, lq), 1)
    s2 = jnp.where(col <= row, s2, NEG_INF)
    m = jnp.maximum(jnp.max(s1, axis=1, keepdims=True), jnp.max(s2, axis=1, keepdims=True))
    p1 = jnp.exp(s1 - m)
    p2 = jnp.exp(s2 - m)
    l = jnp.sum(p1, axis=1, keepdims=True) + jnp.sum(p2, axis=1, keepdims=True)
    acc = jnp.dot(p1.astype(BF16), vp_ref[...].astype(BF16), preferred_element_type=F32)
    acc = acc + jnp.dot(p2.astype(BF16), vn_ref[...], preferred_element_type=F32)
    o_ref[...] = (acc / l).astype(o_ref.dtype)


def _attn_sample(q3, kn3, vn3, kp3, vp3, f3):
    b, lq, da = q3.shape
    past = kp3.shape[1]
    nh = da // HEAD_DIM
    nf = f3.shape[2]
    new_spec = pl.BlockSpec((None, lq, HEAD_DIM), lambda bi, hi: (bi, 0, hi))
    past_spec = pl.BlockSpec((None, past, HEAD_DIM), lambda bi, hi: (bi, 0, hi))
    return pl.pallas_call(
        functools.partial(_attn_sample_kernel, past=past, lq=lq),
        grid=(b, nh),
        in_specs=[new_spec, past_spec, past_spec, new_spec, new_spec,
                  pl.BlockSpec((None, 1, nf), lambda bi, hi: (bi * nh + hi, 0, 0))],
        out_specs=new_spec,
        out_shape=jax.ShapeDtypeStruct((b, lq, da), BF16),
        compiler_params=_params("parallel", "parallel"),
        name="fox_sample",
    )(q3, kp3, vp3, kn3, vn3, f3.reshape(b * nh, 1, nf))


def _router_kernel(h_ref, wr_ref, br_ref, tri_ref, idx_ref, gate_ref, rank_ref, cnt_ref, carry_ref):
    @pl.when(pl.program_id(0) == 0)
    def _():
        carry_ref[...] = jnp.zeros_like(carry_ref)

    n_exp, tm = wr_ref.shape[0], h_ref.shape[0]
    scores = jax.nn.sigmoid(lax.dot_general(wr_ref[...], h_ref[...], NT_DIMS, preferred_element_type=F32))
    work = scores + br_ref[...]
    eidx = lax.broadcasted_iota(I32, (n_exp, tm), 0)
    picked = jnp.zeros((n_exp, tm), F32)
    onehots, gates = [], []
    for k in range(TOP_K):
        best = jnp.max(work, axis=0, keepdims=True)
        first = jnp.min(jnp.where(work == best, eidx, n_exp), axis=0, keepdims=True)
        onehot = eidx == first
        idx_ref[k:k + 1, :] = first
        gates.append(jnp.sum(jnp.where(onehot, scores, 0.0), axis=0, keepdims=True))
        onehots.append(onehot)
        picked = picked + jnp.where(onehot, 1.0, 0.0)
        work = jnp.where(onehot, NEG_INF, work)
    norm = ROUTED_SCALE / functools.reduce(lambda a, c: a + c, gates)
    cum = jnp.dot(picked.astype(BF16), tri_ref[...], preferred_element_type=F32)
    before = cum - picked + carry_ref[:, 0:1]
    for k in range(TOP_K):
        gate_ref[k:k + 1, :] = gates[k] * norm
        rank_ref[k:k + 1, :] = jnp.sum(jnp.where(onehots[k], before, 0.0), axis=0, keepdims=True).astype(I32)
    carry_ref[...] = carry_ref[...] + cum[:, tm - 1:tm]
    cnt_ref[...] = carry_ref[...]


def _router(h2, w_router_t, b_router, tm):
    t, d = h2.shape
    n_exp = w_router_t.shape[0]
    tri = jnp.triu(jnp.ones((tm, tm), BF16))
    tok_spec = pl.BlockSpec((TOP_K, tm), lambda i: (0, i))
    idx_t, gate_t, rank_t, cnt = pl.pallas_call(
        _router_kernel,
        grid=(t // tm,),
        in_specs=[
            pl.BlockSpec((tm, d), lambda i: (i, 0)),
            pl.BlockSpec((n_exp, d), lambda i: (0, 0)),
            pl.BlockSpec((n_exp, 1), lambda i: (0, 0)),
            pl.BlockSpec((tm, tm), lambda i: (0, 0)),
        ],
        out_specs=[tok_spec, tok_spec, tok_spec, pl.BlockSpec((n_exp, 128), lambda i: (0, 0))],
        out_shape=[
            jax.ShapeDtypeStruct((TOP_K, t), I32),
            jax.ShapeDtypeStruct((TOP_K, t), F32),
            jax.ShapeDtypeStruct((TOP_K, t), I32),
            jax.ShapeDtypeStruct((n_exp, 128), F32),
        ],
        scratch_shapes=[pltpu.VMEM((n_exp, 128), F32)],
        compiler_params=_params("arbitrary"),
        name="moe_router",
    )(h2, w_router_t, b_router.reshape(n_exp, 1), tri)
    return idx_t, gate_t, rank_t, cnt[:, 0].astype(I32)


def _moe_kernel(be_ref, bv_ref, nu_ref, x_ref, wg_ref, wu_ref, wd_ref, y_ref):
    i = pl.program_id(0)

    @pl.when(i < nu_ref[0])
    def _():
        x = _load_packed(x_ref, MOE_ROWS, valid=bv_ref[i]).astype(BF16)
        g = jnp.dot(x, wg_ref[...], preferred_element_type=F32)
        u = jnp.dot(x, wu_ref[...], preferred_element_type=F32)
        a = (_silu(g) * u).astype(BF16)
        _store_packed(y_ref, jnp.dot(a, wd_ref[...], preferred_element_type=F32))


def _moe_experts(x_sorted, blk_e, blk_valid, n_used, w_e_gate, w_e_up, w_e_down):
    cap = x_sorted.shape[0] // PACK_CHUNKS
    d, de = w_e_gate.shape[1:]
    n_blocks = cap // MOE_ROWS
    row_map = lambda i, be, bv, nu: (jnp.minimum(i, nu[0] - 1), 0)
    w_map = lambda i, be, bv, nu: (be[i], 0, 0)
    return pl.pallas_call(
        _moe_kernel,
        grid_spec=pltpu.PrefetchScalarGridSpec(
            num_scalar_prefetch=3,
            grid=(n_blocks,),
            in_specs=[
                pl.BlockSpec((MOE_ROWS * PACK_CHUNKS, LANES), row_map),
                pl.BlockSpec((None, d, de), w_map),
                pl.BlockSpec((None, d, de), w_map),
                pl.BlockSpec((None, de, d), w_map),
            ],
            out_specs=pl.BlockSpec((MOE_ROWS * PACK_CHUNKS, LANES), row_map),
        ),
        out_shape=jax.ShapeDtypeStruct(x_sorted.shape, U32),
        compiler_params=_params("arbitrary"),
        name="moe_experts",
    )(blk_e, blk_valid, n_used, x_sorted, w_e_gate, w_e_up, w_e_down)


def _sc_mesh():
    return plsc.VectorSubcoreMesh(core_axis_name="c", subcore_axis_name="s")


def _sc_worker_id():
    return lax.axis_index("s") * SC_CORES + lax.axis_index("c")


def _sc_scatter_rows(rows, dest_win, n_out):
    t = rows.shape[0]
    n_all_win, _, win = dest_win.shape
    row_shape = rows.shape[1:]
    n_win = n_all_win // SC_WORKERS

    @functools.partial(
        pl.kernel, mesh=_sc_mesh(),
        out_type=jax.ShapeDtypeStruct((n_out,) + row_shape, rows.dtype),
        scratch_types=[
            pltpu.VMEM((n_win, TOP_K, win), I32),
            pltpu.VMEM((win,) + row_shape, rows.dtype),
            pltpu.SemaphoreType.DMA,
        ],
    )
    def scatter_kernel(rows_hbm, idx_hbm, out_hbm, idx_v, rows_v, sem):
        first = _sc_worker_id() * n_win
        pltpu.sync_copy(idx_hbm.at[pl.ds(first, n_win)], idx_v)

        @pl.loop(0, n_win)
        def _(w):
            pltpu.sync_copy(rows_hbm.at[pl.ds((first + w) * win, win)], rows_v)
            copies = [pltpu.make_async_copy(rows_v, out_hbm.at[idx_v.at[w, k]], sem) for k in range(TOP_K)]
            for c in copies:
                c.start()
            for c in copies:
                c.wait()

    return scatter_kernel(rows, dest_win)


def _sc_gather_rows(table, idx_win):
    n_all_win, win = idx_win.shape
    row_shape = table.shape[1:]
    n_win = n_all_win // SC_WORKERS

    @functools.partial(
        pl.kernel, mesh=_sc_mesh(),
        out_type=jax.ShapeDtypeStruct((n_all_win * win,) + row_shape, table.dtype),
        scratch_types=[
            pltpu.VMEM((n_win, win), I32),
            pltpu.VMEM((win,) + row_shape, table.dtype),
            pltpu.SemaphoreType.DMA,
        ],
    )
    def gather_kernel(table_hbm, idx_hbm, out_hbm, idx_v, rows_v, sem):
        first = _sc_worker_id() * n_win
        pltpu.sync_copy(idx_hbm.at[pl.ds(first, n_win)], idx_v)

        @pl.loop(0, n_win)
        def _(w):
            pltpu.async_copy(table_hbm.at[idx_v.at[w]], rows_v, sem).wait()
            pltpu.sync_copy(rows_v, out_hbm.at[pl.ds((first + w) * win, win)])

    return gather_kernel(table, idx_win)


def _final_kernel(x1_ref, h2_ref, yt_ref, gate_ref, mod_ref, wsg_ref, wsu_ref, wsd_ref, gf_ref, o_ref):
    h2 = h2_ref[...]
    g = jnp.dot(h2, wsg_ref[...], preferred_element_type=F32)
    u = jnp.dot(h2, wsu_ref[...], preferred_element_type=F32)
    f = jnp.dot((_silu(g) * u).astype(BF16), wsd_ref[...], preferred_element_type=F32)
    gate = gate_ref[...]
    tl = h2.shape[0]
    for k in range(TOP_K):
        f = f + gate[:, k:k + 1] * _load_packed(yt_ref.at[k], tl)
    x2 = x1_ref[...] + mod_ref[N_MOD - 1:N_MOD, :] * f
    ms = jnp.mean(x2 * x2, axis=-1, keepdims=True)
    o_ref[...] = x2 * lax.rsqrt(ms + EPS) * gf_ref[...]


def _final(x1, h2, y_tok, gate, mod3, mod_row0, w_s_gate, w_s_up, w_s_down, g_final, tl):
    b, l, d = x1.shape
    ds = w_s_gate.shape[1]
    tile = pl.BlockSpec((None, tl, d), lambda bi, li: (bi, li, 0))
    const = lambda shape: pl.BlockSpec(shape, lambda bi, li: (0,) * len(shape))
    return pl.pallas_call(
        _final_kernel,
        grid=(b, l // tl),
        in_specs=[
            tile,
            tile,
            pl.BlockSpec((TOP_K, None, tl * PACK_CHUNKS, LANES), lambda bi, li: (0, bi, li, 0)),
            pl.BlockSpec((None, tl, TOP_K), lambda bi, li: (bi, li, 0)),
            pl.BlockSpec((None, N_MOD, d), lambda bi, li: (mod_row0 + bi, 0, 0)),
            const((d, ds)),
            const((d, ds)),
            const((ds, d)),
            const((1, d)),
        ],
        out_specs=tile,
        out_shape=jax.ShapeDtypeStruct((b, l, d), F32),
        compiler_params=_params("parallel", "parallel"),
        name="combine_final",
    )(x1, h2, y_tok, gate, mod3, w_s_gate, w_s_up, w_s_down, g_final.reshape(1, d))


def _group_forward(x3, mod3, mod_row0, hist, start, past, wts, g_final, tiles):
    b, l, d = x3.shape
    tl, t_route, w_outer = tiles["tl"], tiles["t_route"], tiles["w_outer"]
    dp = wts["w_pool"].shape[0] * wts["w_pool"].shape[1]
    da = wts["w_up_b"].shape[0]
    nh = da // HEAD_DIM
    t = b * l

    h, logf_t = _norm_mod(x3, wts["g_mix"], mod3, mod_row0, 0, 1, tl, wts["w_fl_t"], wts["b_f"])

    ident = lambda acc: (acc,)
    mm = functools.partial(_mm, tl=tl, tn=1024, w_outer=w_outer)
    (u,) = mm(h, wts["w_main"], n=dp, w_col0=0, epilogue=ident, out_dtypes=(F32,), name="in_proj_u")
    scale = HEAD_DIM ** -0.5
    (qb,) = mm(h, wts["w_main"], n=da, w_col0=dp, epilogue=lambda acc: (acc * scale,), out_dtypes=(BF16,),
               name="in_proj_q")
    dup = lambda acc: (acc, acc)
    k, kb = mm(h, wts["w_main"], n=da, w_col0=dp + da, epilogue=dup, out_dtypes=(F32, BF16), name="in_proj_k")
    v, vb = mm(h, wts["w_main"], n=da, w_col0=dp + 2 * da, epilogue=dup, out_dtypes=(F32, BF16), name="in_proj_v")
    (gates,) = mm(h, wts["w_gates"], n=2 * d, epilogue=lambda acc: (jax.nn.sigmoid(acc),), out_dtypes=(BF16,),
                  name="in_proj_gates")

    y_a = _pool_mixer(u, hist, wts["w_pool"], wts["s_pool"], start, tl)

    if past is None:
        y_b = _attn_prompt(qb, kb, vb, _cumsum_last(logf_t), tiles["tq"])
    else:
        cache_k, cache_v, cache_logf = past
        p = cache_k.shape[1]
        nf = -(-(p + l) // 128) * 128
        lf_all = jnp.concatenate([jnp.swapaxes(cache_logf.astype(F32), 1, 2), logf_t,
                                  jnp.zeros((b, nh, nf - p - l), F32)], axis=2)
        y_b = _attn_sample(qb, kb, vb, cache_k.reshape(b, p, da), cache_v.reshape(b, p, da), _cumsum_last(lf_all))

    tile_n = lambda off: (lambda bi, li, j: (bi, li, off + j))
    (m_a,) = mm(y_a, wts["w_up_a"], n=d, extras=[(gates, (None, tl, 1024), tile_n(0))],
                epilogue=lambda acc, ga: (ga.astype(F32) * acc,), out_dtypes=(BF16,), name="up_a")
    (merged,) = mm(y_b, wts["w_up_b"], n=d,
                   extras=[(gates, (None, tl, 1024), tile_n(d // 1024)), (m_a, (None, tl, 1024), tile_n(0))],
                   epilogue=lambda acc, gb, ma: (gb.astype(F32) * acc + ma.astype(F32),), out_dtypes=(BF16,),
                   name="up_b")
    mod_spec = (mod3, (None, N_MOD, 1024), lambda bi, li, j: (mod_row0 + bi, 0, j))
    (x1,) = mm(merged, wts["w_out"], n=d, extras=[(x3, (None, tl, 1024), tile_n(0)), mod_spec],
               epilogue=lambda acc, xr, m: (xr + m[2:3, :] * acc,), out_dtypes=(F32,), name="out_proj")

    h2, h2_packed = _norm_mod(x1, wts["g_ffn"], mod3, mod_row0, 3, 4, tl, with_packed=True)
    idx_t, gate_t, rank_t, counts = _router(h2.reshape(t, d), wts["w_router_t"], wts["b_router"], t_route)

    n_exp = counts.shape[0]
    n_slots = t * TOP_K
    n_blocks = n_slots // MOE_ROWS + n_exp
    cap = n_blocks * MOE_ROWS
    padded = (counts + MOE_ROWS - 1) // MOE_ROWS * MOE_ROWS
    pad_end = jnp.cumsum(padded)
    pad_start = pad_end - padded
    blk_row0 = jnp.arange(n_blocks, dtype=I32) * MOE_ROWS
    blk_e = jnp.minimum(jnp.sum(pad_end[None, :] <= blk_row0[:, None], axis=1), n_exp - 1).astype(I32)
    blk_valid = jnp.clip(pad_start[blk_e] + counts[blk_e] - blk_row0, 0, MOE_ROWS).astype(I32)
    n_used = (pad_end[-1:] // MOE_ROWS).astype(I32)
    dest = rank_t + jnp.sum(jnp.where(idx_t[:, :, None] == jnp.arange(n_exp, dtype=I32), pad_start, 0), axis=2)

    win_d = min(SC_WINDOW, t // SC_WORKERS)
    dest_win = dest.reshape(TOP_K, t // win_d, win_d).transpose(1, 0, 2)
    row_shape = (PACK_CHUNKS, LANES)
    x_sorted = _sc_scatter_rows(h2_packed.reshape((t,) + row_shape), dest_win, cap)
    y_sorted = _moe_experts(x_sorted.reshape(cap * PACK_CHUNKS, LANES), blk_e, blk_valid, n_used,
                            wts["w_e_gate"], wts["w_e_up"], wts["w_e_down"])
    win_c = min(SC_WINDOW, n_slots // SC_WORKERS)
    y_tok = _sc_gather_rows(y_sorted.reshape((cap,) + row_shape), dest.reshape(n_slots // win_c, win_c))
    y_tok = y_tok.reshape(TOP_K, b, l * PACK_CHUNKS, LANES)

    y = _final(x1, h2, y_tok, gate_t.T.reshape(b, l, TOP_K), mod3, mod_row0,
               wts["w_s_gate"], wts["w_s_up"], wts["w_s_down"], g_final, tiles["tl_final"])

    new_pool = u[:, l - POOL_HIST:, :]
    return y, new_pool, k.reshape(b, l, nh, HEAD_DIM), v.reshape(b, l, nh, HEAD_DIM), jnp.swapaxes(logf_t, 1, 2)


def kernel(x_prompt, x_sample, cache_pool, cache_k, cache_v, cache_logf, c_prompt, c_sample, w_ada, b_ada, g_mix, w_in, b_f, w_pool, s_pool, w_up_a, w_up_b, w_out, g_ffn, w_router, b_router, w_e_gate, w_e_up, w_e_down, w_s_gate, w_s_up, w_s_down, g_final):
    depth = w_ada.shape[0]
    assert depth == 1, "a single layer is supported"
    bp, lp, d = x_prompt.shape
    bs, ls, _ = x_sample.shape
    dp = w_pool.shape[1] * w_pool.shape[2]
    da = w_up_b.shape[1]
    nh = da // HEAD_DIM
    assert bp <= MOD_ROWS_SAMPLE

    lyr = 0
    one = lambda a: a.reshape(a.shape[1:])
    c_all = jnp.zeros((MOD_ROWS_SAMPLE + bs, d), F32).at[:bp].set(c_prompt).at[MOD_ROWS_SAMPLE:].set(c_sample)
    mod3 = _ada_mod(c_all, one(w_ada), one(b_ada)).reshape(c_all.shape[0], N_MOD, d)

    n_main = dp + 3 * da
    w_in1 = one(w_in)
    wts = dict(
        g_mix=one(g_mix), g_ffn=one(g_ffn), b_f=one(b_f), s_pool=one(s_pool), b_router=one(b_router),
        w_main=w_in1[:, :n_main].astype(BF16),
        w_fl_t=w_in1[:, n_main:n_main + nh].T.astype(BF16),
        w_gates=w_in1[:, n_main + nh:].astype(BF16),
        w_pool=one(w_pool).astype(BF16),
        w_up_a=one(w_up_a).astype(BF16), w_up_b=one(w_up_b).astype(BF16), w_out=one(w_out).astype(BF16),
        w_router_t=one(w_router).T.astype(BF16),
        w_e_gate=one(w_e_gate).astype(BF16), w_e_up=one(w_e_up).astype(BF16), w_e_down=one(w_e_down).astype(BF16),
        w_s_gate=one(w_s_gate).astype(BF16), w_s_up=one(w_s_up).astype(BF16), w_s_down=one(w_s_down).astype(BF16),
    )

    tiles_p = dict(tl=min(512, lp), tq=min(512, lp), t_route=min(512, bp * lp), w_outer=False,
                   tl_final=min(256, lp))
    tiles_s = dict(tl=ls, t_route=bs * ls, w_outer=True, tl_final=ls)

    hist_p = jnp.zeros((bp, HALO, dp), F32)
    y_p, pool_p, k_p, v_p, f_p = _group_forward(x_prompt, mod3, MOD_ROWS_PROMPT, hist_p, 0, None, wts, g_final,
                                                tiles_p)
    hist_s = jnp.pad(one(cache_pool), ((0, 0), (HALO - POOL_HIST, 0), (0, 0)))
    past = (one(cache_k), one(cache_v), one(cache_logf))
    y_s, pool_s, k_s, v_s, f_s = _group_forward(x_sample, mod3, MOD_ROWS_SAMPLE, hist_s, cache_k.shape[2], past,
                                                wts, g_final, tiles_s)
    stack = lambda a: a[None]
    return (y_p, y_s, stack(pool_p), stack(k_p), stack(v_p), stack(f_p),
            stack(pool_s), stack(k_s), stack(v_s), stack(f_s))
```

```python
import functools

import jax
import jax.numpy as jnp
from jax import lax
from jax.experimental import pallas as pl
from jax.experimental.pallas import tpu as pltpu
from jax.experimental.pallas import tpu_sc as plsc

F32 = jnp.float32
BF16 = jnp.bfloat16
I32 = jnp.int32
U32 = jnp.uint32

EPS = 1e-6
N_MOD = 6
POOL_WINDOWS = (2, 4, 8, 16)
POOL_HIST = max(POOL_WINDOWS) - 1
HALO = 16
HEAD_DIM = 128
TOP_K = 8
ROUTED_SCALE = 2.5
MOE_ROWS = 256
MOD_ROWS_PROMPT = 0
MOD_ROWS_SAMPLE = 8
VMEM_LIMIT_BYTES = 52 * 1024 * 1024
NEG_INF = float("-inf")
LOG2_E = 1.4426950408889634
LANES = 128
PACK_CHUNKS = 8
SC_CORES = 2
SC_SUBCORES = 16
SC_WORKERS = SC_CORES * SC_SUBCORES
SC_WINDOW = 32
NT_DIMS = (((1,), (1,)), ((), ()))


def _params(*semantics):
    return pltpu.CompilerParams(dimension_semantics=semantics, vmem_limit_bytes=VMEM_LIMIT_BYTES)


def _silu(x):
    return x * jax.nn.sigmoid(x)


def _store_packed(ref, x):
    rows, width = x.shape
    half = width // 2
    bits = pltpu.bitcast(x.astype(BF16).astype(F32), U32)
    packed = (bits[:, :half] >> 16) | (bits[:, half:] & jnp.uint32(0xFFFF0000))
    for j in range(PACK_CHUNKS):
        ref[pl.ds(j, rows, stride=PACK_CHUNKS), :] = packed[:, j * LANES:(j + 1) * LANES]


def _load_packed(ref, rows, valid=None):
    lo, hi = [], []
    for j in range(PACK_CHUNKS):
        p = ref[pl.ds(j, rows, stride=PACK_CHUNKS), :]
        if valid is not None:
            p = jnp.where(lax.broadcasted_iota(I32, p.shape, 0) < valid, p, jnp.uint32(0))
        lo.append(pltpu.bitcast(p << 16, F32))
        hi.append(pltpu.bitcast(p & jnp.uint32(0xFFFF0000), F32))
    return jnp.concatenate(lo + hi, axis=1)


def _ada_kernel(c_ref, w_ref, b_ref, o_ref):
    a = _silu(c_ref[...]).astype(BF16)
    o_ref[...] = jnp.dot(a, w_ref[...].astype(BF16), preferred_element_type=F32) + b_ref[...]


def _ada_mod(c_all, w_ada, b_ada):
    rows, d = c_all.shape
    n = w_ada.shape[1]
    tn = 1024
    return pl.pallas_call(
        _ada_kernel,
        grid=(n // tn,),
        in_specs=[
            pl.BlockSpec((rows, d), lambda j: (0, 0)),
            pl.BlockSpec((d, tn), lambda j: (0, j)),
            pl.BlockSpec((1, tn), lambda j: (0, j)),
        ],
        out_specs=pl.BlockSpec((rows, tn), lambda j: (0, j)),
        out_shape=jax.ShapeDtypeStruct((rows, n), F32),
        compiler_params=_params("parallel"),
        name="ada_mod",
    )(c_all, w_ada, b_ada.reshape(1, n))


def _modulated_norm(x, g, m, shift_row, scale_row):
    ms = jnp.mean(x * x, axis=-1, keepdims=True)
    return x * lax.rsqrt(ms + EPS) * g * (1.0 + m[scale_row:scale_row + 1, :]) + m[shift_row:shift_row + 1, :]


def _norm_kernel(x_ref, g_ref, mod_ref, h_ref, hp_ref, *, shift_row, scale_row):
    h = _modulated_norm(x_ref[...], g_ref[...], mod_ref[...], shift_row, scale_row)
    h_ref[...] = h.astype(BF16)
    _store_packed(hp_ref, h)


def _norm_mod(x3, g, mod3, mod_row0, shift_row, scale_row, tl):
    b, l, d = x3.shape
    n_l = l // tl
    return pl.pallas_call(
        functools.partial(_norm_kernel, shift_row=shift_row, scale_row=scale_row),
        grid=(b, n_l),
        in_specs=[
            pl.BlockSpec((None, tl, d), lambda bi, li: (bi, li, 0)),
            pl.BlockSpec((1, d), lambda bi, li: (0, 0)),
            pl.BlockSpec((None, N_MOD, d), lambda bi, li: (mod_row0 + bi, 0, 0)),
        ],
        out_specs=[
            pl.BlockSpec((None, tl, d), lambda bi, li: (bi, li, 0)),
            pl.BlockSpec((tl * PACK_CHUNKS, LANES), lambda bi, li: (bi * n_l + li, 0)),
        ],
        out_shape=[
            jax.ShapeDtypeStruct((b, l, d), BF16),
            jax.ShapeDtypeStruct((b * l * PACK_CHUNKS, LANES), U32),
        ],
        compiler_params=_params("parallel", "parallel"),
        name="norm_mod",
    )(x3, g.reshape(1, d), mod3)


def _in_proj_kernel(x_ref, g_ref, mod_ref, wfl_ref, bf_ref, w_ref,
                    u_ref, q_ref, k32_ref, kb_ref, v32_ref, vb_ref, gates_ref, lf_ref, h_ref, *, q_scale):
    j = pl.program_id(2)
    nb, tl, _ = x_ref.shape
    nh = wfl_ref.shape[0]

    @pl.when(j == 0)
    def _():
        for bi in range(nb):
            hb = _modulated_norm(x_ref[bi], g_ref[...], mod_ref[bi], 0, 1).astype(BF16)
            h_ref[bi * tl:(bi + 1) * tl, :] = hb
            fl = lax.dot_general(wfl_ref[...], hb, NT_DIMS, preferred_element_type=F32) + bf_ref[...]
            lf_ref[bi] = jnp.minimum(fl, 0.0) - jnp.log(1.0 + jnp.exp(-jnp.abs(fl)))

    acc = jnp.dot(h_ref[...], w_ref[...], preferred_element_type=F32)
    rows = lambda bi: acc[bi * tl:(bi + 1) * tl, :]

    def store_heads(f32_ref, bf_ref_, bi):
        r = rows(bi)
        bf_ref_[bi] = r.astype(BF16)
        for hd in range(nh):
            f32_ref[bi, pl.ds(hd, tl, stride=nh), :] = r[:, hd * HEAD_DIM:(hd + 1) * HEAD_DIM]

    @pl.when(j == 0)
    def _():
        for bi in range(nb):
            u_ref[bi] = rows(bi)

    @pl.when(j == 1)
    def _():
        for bi in range(nb):
            q_ref[bi] = (rows(bi) * q_scale).astype(BF16)

    @pl.when(j == 2)
    def _():
        for bi in range(nb):
            store_heads(k32_ref, kb_ref, bi)

    @pl.when(j == 3)
    def _():
        for bi in range(nb):
            store_heads(v32_ref, vb_ref, bi)

    @pl.when(j >= 4)
    def _():
        for bi in range(nb):
            gates_ref[bi] = jax.nn.sigmoid(rows(bi)).astype(BF16)


def _in_proj(x3, g, mod3, mod_row0, w_fl_t, b_f, w_cat, q_scale, nb, tl):
    b, l, d = x3.shape
    nh = w_fl_t.shape[0]
    tn = nh * HEAD_DIM
    n_tiles = w_cat.shape[1] // tn
    n_gate_tiles = n_tiles - 4
    assert mod_row0 % nb == 0 and b % nb == 0
    tile = lambda width: pl.BlockSpec((nb, tl, width), lambda bb, li, j: (bb, li, 0))
    heads = pl.BlockSpec((nb, tl * nh, HEAD_DIM), lambda bb, li, j: (bb, li, 0))
    return pl.pallas_call(
        functools.partial(_in_proj_kernel, q_scale=q_scale),
        grid=(b // nb, l // tl, n_tiles),
        in_specs=[
            tile(d),
            pl.BlockSpec((1, d), lambda bb, li, j: (0, 0)),
            pl.BlockSpec((nb, N_MOD, d), lambda bb, li, j: (mod_row0 // nb + bb, 0, 0)),
            pl.BlockSpec((nh, d), lambda bb, li, j: (0, 0)),
            pl.BlockSpec((nh, 1), lambda bb, li, j: (0, 0)),
            pl.BlockSpec((d, tn), lambda bb, li, j: (0, j)),
        ],
        out_specs=[
            tile(tn), tile(tn), heads, tile(tn), heads, tile(tn),
            pl.BlockSpec((nb, tl, tn), lambda bb, li, j: (bb, li, jnp.maximum(j - 4, 0))),
            pl.BlockSpec((nb, nh, tl), lambda bb, li, j: (bb, 0, li)),
        ],
        out_shape=[
            jax.ShapeDtypeStruct((b, l, tn), F32),
            jax.ShapeDtypeStruct((b, l, tn), BF16),
            jax.ShapeDtypeStruct((b, l * nh, HEAD_DIM), F32),
            jax.ShapeDtypeStruct((b, l, tn), BF16),
            jax.ShapeDtypeStruct((b, l * nh, HEAD_DIM), F32),
            jax.ShapeDtypeStruct((b, l, tn), BF16),
            jax.ShapeDtypeStruct((b, l, n_gate_tiles * tn), BF16),
            jax.ShapeDtypeStruct((b, nh, l), F32),
        ],
        scratch_shapes=[pltpu.VMEM((nb * tl, d), BF16)],
        compiler_params=_params("parallel", "parallel", "arbitrary"),
        name="in_proj",
    )(x3, g.reshape(1, d), mod3, w_fl_t, b_f.reshape(nh, 1), w_cat)


def _mm_kernel(*refs, n_extra, epilogue):
    x_ref, w_ref = refs[0], refs[1]
    extras = refs[2:2 + n_extra]
    outs = refs[2 + n_extra:]
    acc = jnp.dot(x_ref[...], w_ref[...], preferred_element_type=F32)
    res = epilogue(acc, *[e[...] for e in extras])
    for o_ref, r in zip(outs, res):
        o_ref[...] = r.astype(o_ref.dtype)


def _mm(x3, w, *, n, w_col0=0, extras=(), epilogue, out_dtypes, tl, tn, w_outer, name):
    b, l, k = x3.shape
    nl, nj = l // tl, n // tn
    j0 = w_col0 // tn
    if w_outer:
        grid = (nj, b, nl)
        wrap = lambda fn: (lambda j, bi, li: fn(bi, li, j))
    else:
        grid = (b, nl, nj)
        wrap = lambda fn: (lambda bi, li, j: fn(bi, li, j))
    in_specs = [
        pl.BlockSpec((None, tl, k), wrap(lambda bi, li, j: (bi, li, 0))),
        pl.BlockSpec((k, tn), wrap(lambda bi, li, j: (0, j0 + j))),
    ]
    args = [x3, w]
    for arr, shape, fn in extras:
        in_specs.append(pl.BlockSpec(shape, wrap(fn)))
        args.append(arr)
    out = pl.pallas_call(
        functools.partial(_mm_kernel, n_extra=len(extras), epilogue=epilogue),
        grid=grid,
        in_specs=in_specs,
        out_specs=[pl.BlockSpec((None, tl, tn), wrap(lambda bi, li, j: (bi, li, j))) for _ in out_dtypes],
        out_shape=[jax.ShapeDtypeStruct((b, l, n), dt) for dt in out_dtypes],
        compiler_params=_params("parallel", "parallel", "parallel"),
        name=name,
    )(*args)
    return out


def _pool_kernel(u_ref, halo_ref, hist_ref, wp_ref, sp_ref, o_ref, full_ref, *, tl, start, gw):
    i = pl.program_id(1)
    full_ref[0:HALO, :] = jnp.where(i == 0, hist_ref[...], halo_ref[...])
    full_ref[HALO:HALO + tl, :] = u_ref[...]
    pos = start + i * tl + lax.broadcasted_iota(I32, (tl, 1), 0)
    for g, w in enumerate(POOL_WINDOWS):
        c0, c1 = g * gw, (g + 1) * gw
        cur = full_ref[HALO:HALO + tl, c0:c1]
        win = cur
        for d in range(1, w):
            win = win + full_ref[HALO - d:HALO - d + tl, c0:c1]
        cnt = jnp.minimum(pos + 1, w).astype(F32)
        dmean = win / cnt - cur
        y = jnp.dot(dmean.astype(BF16), wp_ref[g], preferred_element_type=F32) * sp_ref[:, c0:c1]
        o_ref[:, c0:c1] = y.astype(o_ref.dtype)


def _pool_mixer(u3, hist, w_pool, s_pool, start, tl):
    b, l, dp = u3.shape
    gw = dp // len(POOL_WINDOWS)
    halo_blocks = tl // HALO
    return pl.pallas_call(
        functools.partial(_pool_kernel, tl=tl, start=start, gw=gw),
        grid=(b, l // tl),
        in_specs=[
            pl.BlockSpec((None, tl, dp), lambda bi, li: (bi, li, 0)),
            pl.BlockSpec((None, HALO, dp), lambda bi, li: (bi, jnp.maximum(li * halo_blocks - 1, 0), 0)),
            pl.BlockSpec((None, HALO, dp), lambda bi, li: (bi, 0, 0)),
            pl.BlockSpec((len(POOL_WINDOWS), gw, gw), lambda bi, li: (0, 0, 0)),
            pl.BlockSpec((1, dp), lambda bi, li: (0, 0)),
        ],
        out_specs=pl.BlockSpec((None, tl, dp), lambda bi, li: (bi, li, 0)),
        out_shape=jax.ShapeDtypeStruct((b, l, dp), BF16),
        scratch_shapes=[pltpu.VMEM((HALO + tl, dp), F32)],
        compiler_params=_params("parallel", "parallel"),
        name="pool_mixer",
    )(u3, u3, hist, w_pool, s_pool.reshape(1, dp))


def _cumsum_kernel(x_ref, o_ref):
    x = x_ref[...]
    n = x.shape[1]
    idx = lax.broadcasted_iota(I32, x.shape, 1)
    s = 1
    while s < n:
        x = x + jnp.where(idx >= s, pltpu.roll(x, s, 1), 0.0)
        s *= 2
    o_ref[...] = x


def _cumsum_last(x3):
    b, h, n = x3.shape
    return pl.pallas_call(
        _cumsum_kernel,
        grid=(b,),
        in_specs=[pl.BlockSpec((None, h, n), lambda bi: (bi, 0, 0))],
        out_specs=pl.BlockSpec((None, h, n), lambda bi: (bi, 0, 0)),
        out_shape=jax.ShapeDtypeStruct((b, h, n), F32),
        compiler_params=_params("parallel"),
        name="cumsum_logf",
    )(x3)


def _attn_prompt_kernel(q_ref, k_ref, v_ref, fk_ref, o_ref, s0_ref, s1_ref, m_ref, l_ref, acc_ref, *, tq):
    qi = pl.program_id(2)
    q = q_ref[...]
    n_chunks = tq // LANES

    def produce(j, s_ref):
        off = pl.multiple_of(j * tq, tq)
        s = lax.dot_general(q, k_ref[pl.ds(off, tq), :], NT_DIMS, preferred_element_type=F32)
        s_ref[...] = s - fk_ref[j] * LOG2_E

    def consume(j, s_ref, diagonal):
        s = s_ref[...]
        if diagonal:
            row = lax.broadcasted_iota(I32, (tq, tq), 0)
            col = lax.broadcasted_iota(I32, (tq, tq), 1)
            s = jnp.where(col <= row, s, NEG_INF)
        chunks = [s[:, c * LANES:(c + 1) * LANES] for c in range(n_chunks)]
        m_old = m_ref[...]
        m_new = jnp.maximum(m_old, jnp.max(functools.reduce(jnp.maximum, chunks), axis=1, keepdims=True))
        alpha = jnp.exp2(m_old - m_new)
        p_chunks = [jnp.exp2(ch - m_new) for ch in chunks]
        l_ref[...] = alpha * l_ref[...] + functools.reduce(lambda a, c: a + c, p_chunks)
        p = jnp.concatenate(p_chunks, axis=1).astype(BF16)
        v = v_ref[pl.ds(pl.multiple_of(j * tq, tq), tq), :]
        acc_ref[...] = alpha * acc_ref[...] + jnp.dot(p, v, preferred_element_type=F32)
        m_ref[...] = m_new

    m_ref[...] = jnp.full(m_ref.shape, NEG_INF, F32)
    l_ref[...] = jnp.zeros(l_ref.shape, F32)
    acc_ref[...] = jnp.zeros(acc_ref.shape, F32)
    produce(0, s0_ref)

    def pair(jj, carry):
        j0 = 2 * jj
        produce(j0 + 1, s1_ref)
        consume(j0, s0_ref, False)
        produce(j0 + 2, s0_ref)
        consume(j0 + 1, s1_ref, False)
        return carry

    lax.fori_loop(0, qi // 2, pair, 0)

    @pl.when(qi % 2 == 1)
    def _():
        produce(qi, s1_ref)
        consume(qi - 1, s0_ref, False)
        consume(qi, s1_ref, True)

    @pl.when(qi % 2 == 0)
    def _():
        consume(qi, s0_ref, True)

    l = jnp.sum(l_ref[...], axis=1, keepdims=True)
    o_ref[...] = (acc_ref[...] / l).astype(o_ref.dtype)


def _attn_prompt(q3, k3, v3, f3, tq):
    b, l, da = q3.shape
    nh = da // HEAD_DIM
    nk = l // tq
    fk = f3.reshape(b * nh, nk, 1, tq)
    return pl.pallas_call(
        functools.partial(_attn_prompt_kernel, tq=tq),
        grid=(b, nh, nk),
        in_specs=[
            pl.BlockSpec((None, tq, HEAD_DIM), lambda bi, hi, qi: (bi, qi, hi)),
            pl.BlockSpec((None, l, HEAD_DIM), lambda bi, hi, qi: (bi, 0, hi)),
            pl.BlockSpec((None, l, HEAD_DIM), lambda bi, hi, qi: (bi, 0, hi)),
            pl.BlockSpec((None, nk, 1, tq), lambda bi, hi, qi: (bi * nh + hi, 0, 0, 0)),
        ],
        out_specs=pl.BlockSpec((None, tq, HEAD_DIM), lambda bi, hi, qi: (bi, qi, hi)),
        out_shape=jax.ShapeDtypeStruct((b, l, da), BF16),
        scratch_shapes=[
            pltpu.VMEM((tq, tq), F32),
            pltpu.VMEM((tq, tq), F32),
            pltpu.VMEM((tq, LANES), F32),
            pltpu.VMEM((tq, LANES), F32),
            pltpu.VMEM((tq, HEAD_DIM), F32),
        ],
        compiler_params=_params("parallel", "parallel", "parallel"),
        name="fox_prompt",
    )(q3, k3, v3, fk)


def _attn_sample_kernel(q_ref, kp_ref, vp_ref, kn_ref, vn_ref, f_ref, o_ref, *, past, lq):
    q = q_ref[...]
    f = f_ref[...]
    s1 = lax.dot_general(q, kp_ref[...].astype(BF16), NT_DIMS, preferred_element_type=F32) - f[:, :past]
    s2 = lax.dot_general(q, kn_ref[...], NT_DIMS, preferred_element_type=F32) - f[:, past:past + lq]
    row = lax.broadcasted_iota(I32, (lq, lq), 0)
    col = lax.broadcasted_iota(I32, (lq, lq), 1)
    s2 = jnp.where(col <= row, s2, NEG_INF)
    m = jnp.maximum(jnp.max(s1, axis=1, keepdims=True), jnp.max(s2, axis=1, keepdims=True))
    p1 = jnp.exp(s1 - m)
    p2 = jnp.exp(s2 - m)
    l = jnp.sum(p1, axis=1, keepdims=True) + jnp.sum(p2, axis=1, keepdims=True)
    acc = jnp.dot(p1.astype(BF16), vp_ref[...].astype(BF16), preferred_element_type=F32)
    acc = acc + jnp.dot(p2.astype(BF16), vn_ref[...], preferred_element_type=F32)
    o_ref[...] = (acc / l).astype(o_ref.dtype)


def _attn_sample(q3, kn3, vn3, kp3, vp3, f3):
    b, lq, da = q3.shape
    past = kp3.shape[1]
    nh = da // HEAD_DIM
    nf = f3.shape[2]
    new_spec = pl.BlockSpec((None, lq, HEAD_DIM), lambda bi, hi: (bi, 0, hi))
    past_spec = pl.BlockSpec((None, past, HEAD_DIM), lambda bi, hi: (bi, 0, hi))
    return pl.pallas_call(
        functools.partial(_attn_sample_kernel, past=past, lq=lq),
        grid=(b, nh),
        in_specs=[new_spec, past_spec, past_spec, new_spec, new_spec,
                  pl.BlockSpec((None, 1, nf), lambda bi, hi: (bi * nh + hi, 0, 0))],
        out_specs=new_spec,
        out_shape=jax.ShapeDtypeStruct((b, lq, da), BF16),
        compiler_params=_params("parallel", "parallel"),
        name="fox_sample",
    )(q3, kp3, vp3, kn3, vn3, f3.reshape(b * nh, 1, nf))


def _router_kernel(h_ref, wr_ref, br_ref, tri_ref, idx_ref, gate_ref, rank_ref, cnt_ref, carry_ref):
    @pl.when(pl.program_id(0) == 0)
    def _():
        carry_ref[...] = jnp.zeros_like(carry_ref)

    n_exp, tm = wr_ref.shape[0], h_ref.shape[0]
    scores = jax.nn.sigmoid(lax.dot_general(wr_ref[...], h_ref[...], NT_DIMS, preferred_element_type=F32))
    work = scores + br_ref[...]
    eidx = lax.broadcasted_iota(I32, (n_exp, tm), 0)
    picked = jnp.zeros((n_exp, tm), F32)
    onehots, gates = [], []
    for k in range(TOP_K):
        best = jnp.max(work, axis=0, keepdims=True)
        first = jnp.min(jnp.where(work == best, eidx, n_exp), axis=0, keepdims=True)
        onehot = eidx == first
        idx_ref[k:k + 1, :] = first
        gates.append(jnp.sum(jnp.where(onehot, scores, 0.0), axis=0, keepdims=True))
        onehots.append(onehot)
        picked = picked + jnp.where(onehot, 1.0, 0.0)
        work = jnp.where(onehot, NEG_INF, work)
    norm = ROUTED_SCALE / functools.reduce(lambda a, c: a + c, gates)
    cum = jnp.dot(picked.astype(BF16), tri_ref[...], preferred_element_type=F32)
    before = cum - picked + carry_ref[:, 0:1]
    for k in range(TOP_K):
        gate_ref[k:k + 1, :] = gates[k] * norm
        rank_ref[k:k + 1, :] = jnp.sum(jnp.where(onehots[k], before, 0.0), axis=0, keepdims=True).astype(I32)
    carry_ref[...] = carry_ref[...] + cum[:, tm - 1:tm]
    cnt_ref[...] = carry_ref[...]


def _router(h2, w_router_t, b_router, tm):
    t, d = h2.shape
    n_exp = w_router_t.shape[0]
    tri = jnp.triu(jnp.ones((tm, tm), BF16))
    tok_spec = pl.BlockSpec((TOP_K, tm), lambda i: (0, i))
    idx_t, gate_t, rank_t, cnt = pl.pallas_call(
        _router_kernel,
        grid=(t // tm,),
        in_specs=[
            pl.BlockSpec((tm, d), lambda i: (i, 0)),
            pl.BlockSpec((n_exp, d), lambda i: (0, 0)),
            pl.BlockSpec((n_exp, 1), lambda i: (0, 0)),
            pl.BlockSpec((tm, tm), lambda i: (0, 0)),
        ],
        out_specs=[tok_spec, tok_spec, tok_spec, pl.BlockSpec((n_exp, 128), lambda i: (0, 0))],
        out_shape=[
            jax.ShapeDtypeStruct((TOP_K, t), I32),
            jax.ShapeDtypeStruct((TOP_K, t), F32),
            jax.ShapeDtypeStruct((TOP_K, t), I32),
            jax.ShapeDtypeStruct((n_exp, 128), F32),
        ],
        scratch_shapes=[pltpu.VMEM((n_exp, 128), F32)],
        compiler_params=_params("arbitrary"),
        name="moe_router",
    )(h2, w_router_t, b_router.reshape(n_exp, 1), tri)
    return idx_t, gate_t, rank_t, cnt[:, 0].astype(I32)


def _moe_kernel(be_ref, bv_ref, nu_ref, x_ref, wg_ref, wu_ref, wd_ref, y_ref, wg_bf, wu_bf, wd_bf):
    i = pl.program_id(0)
    active = i < nu_ref[0]

    @pl.when(active & ((i == 0) | (be_ref[i] != be_ref[jnp.maximum(i - 1, 0)])))
    def _():
        wg_bf[...] = wg_ref[...].astype(BF16)
        wu_bf[...] = wu_ref[...].astype(BF16)
        wd_bf[...] = wd_ref[...].astype(BF16)

    @pl.when(active)
    def _():
        x = _load_packed(x_ref, MOE_ROWS, valid=bv_ref[i]).astype(BF16)
        g = jnp.dot(x, wg_bf[...], preferred_element_type=F32)
        u = jnp.dot(x, wu_bf[...], preferred_element_type=F32)
        a = (_silu(g) * u).astype(BF16)
        _store_packed(y_ref, jnp.dot(a, wd_bf[...], preferred_element_type=F32))


def _moe_experts(x_sorted, blk_e, blk_valid, n_used, w_e_gate, w_e_up, w_e_down):
    cap = x_sorted.shape[0] // PACK_CHUNKS
    d, de = w_e_gate.shape[1:]
    n_blocks = cap // MOE_ROWS
    row_map = lambda i, be, bv, nu: (jnp.minimum(i, nu[0] - 1), 0)
    w_map = lambda i, be, bv, nu: (be[i], 0, 0)
    return pl.pallas_call(
        _moe_kernel,
        grid_spec=pltpu.PrefetchScalarGridSpec(
            num_scalar_prefetch=3,
            grid=(n_blocks,),
            in_specs=[
                pl.BlockSpec((MOE_ROWS * PACK_CHUNKS, LANES), row_map),
                pl.BlockSpec((None, d, de), w_map),
                pl.BlockSpec((None, d, de), w_map),
                pl.BlockSpec((None, de, d), w_map),
            ],
            out_specs=pl.BlockSpec((MOE_ROWS * PACK_CHUNKS, LANES), row_map),
            scratch_shapes=[pltpu.VMEM((d, de), BF16), pltpu.VMEM((d, de), BF16), pltpu.VMEM((de, d), BF16)],
        ),
        out_shape=jax.ShapeDtypeStruct(x_sorted.shape, U32),
        compiler_params=_params("arbitrary"),
        name="moe_experts",
    )(blk_e, blk_valid, n_used, x_sorted, w_e_gate, w_e_up, w_e_down)


def _sc_mesh():
    return plsc.VectorSubcoreMesh(core_axis_name="c", subcore_axis_name="s")


def _sc_worker_id():
    return lax.axis_index("s") * SC_CORES + lax.axis_index("c")


def _sc_scatter_rows(rows, dest_win, n_out):
    t = rows.shape[0]
    n_all_win, _, win = dest_win.shape
    row_shape = rows.shape[1:]
    n_win = n_all_win // SC_WORKERS

    @functools.partial(
        pl.kernel, mesh=_sc_mesh(),
        out_type=jax.ShapeDtypeStruct((n_out,) + row_shape, rows.dtype),
        scratch_types=[
            pltpu.VMEM((n_win, TOP_K, win), I32),
            pltpu.VMEM((win,) + row_shape, rows.dtype),
            pltpu.SemaphoreType.DMA,
        ],
    )
    def scatter_kernel(rows_hbm, idx_hbm, out_hbm, idx_v, rows_v, sem):
        first = _sc_worker_id() * n_win
        pltpu.sync_copy(idx_hbm.at[pl.ds(first, n_win)], idx_v)

        @pl.loop(0, n_win)
        def _(w):
            pltpu.sync_copy(rows_hbm.at[pl.ds((first + w) * win, win)], rows_v)
            copies = [pltpu.make_async_copy(rows_v, out_hbm.at[idx_v.at[w, k]], sem) for k in range(TOP_K)]
            for c in copies:
                c.start()
            for c in copies:
                c.wait()

    return scatter_kernel(rows, dest_win)


def _sc_gather_rows(table, idx_win):
    n_all_win, win = idx_win.shape
    row_shape = table.shape[1:]
    n_win = n_all_win // SC_WORKERS

    @functools.partial(
        pl.kernel, mesh=_sc_mesh(),
        out_type=jax.ShapeDtypeStruct((n_all_win * win,) + row_shape, table.dtype),
        scratch_types=[
            pltpu.VMEM((n_win, win), I32),
            pltpu.VMEM((win,) + row_shape, table.dtype),
            pltpu.SemaphoreType.DMA,
        ],
    )
    def gather_kernel(table_hbm, idx_hbm, out_hbm, idx_v, rows_v, sem):
        first = _sc_worker_id() * n_win
        pltpu.sync_copy(idx_hbm.at[pl.ds(first, n_win)], idx_v)

        @pl.loop(0, n_win)
        def _(w):
            pltpu.async_copy(table_hbm.at[idx_v.at[w]], rows_v, sem).wait()
            pltpu.sync_copy(rows_v, out_hbm.at[pl.ds((first + w) * win, win)])

    return gather_kernel(table, idx_win)


def _final_kernel(x1_ref, h2_ref, yt_ref, gate_ref, mod_ref, wsg_ref, wsu_ref, wsd_ref, gf_ref, o_ref):
    h2 = h2_ref[...]
    g = jnp.dot(h2, wsg_ref[...], preferred_element_type=F32)
    u = jnp.dot(h2, wsu_ref[...], preferred_element_type=F32)
    f = jnp.dot((_silu(g) * u).astype(BF16), wsd_ref[...], preferred_element_type=F32)
    gate = gate_ref[...]
    tl = h2.shape[0]
    for k in range(TOP_K):
        f = f + gate[:, k:k + 1] * _load_packed(yt_ref.at[k], tl)
    x2 = x1_ref[...] + mod_ref[N_MOD - 1:N_MOD, :] * f
    ms = jnp.mean(x2 * x2, axis=-1, keepdims=True)
    o_ref[...] = x2 * lax.rsqrt(ms + EPS) * gf_ref[...]


def _final(x1, h2, y_tok, gate, mod3, mod_row0, w_s_gate, w_s_up, w_s_down, g_final, tl):
    b, l, d = x1.shape
    ds = w_s_gate.shape[1]
    tile = pl.BlockSpec((None, tl, d), lambda bi, li: (bi, li, 0))
    const = lambda shape: pl.BlockSpec(shape, lambda bi, li: (0,) * len(shape))
    return pl.pallas_call(
        _final_kernel,
        grid=(b, l // tl),
        in_specs=[
            tile,
            tile,
            pl.BlockSpec((TOP_K, None, tl * PACK_CHUNKS, LANES), lambda bi, li: (0, bi, li, 0)),
            pl.BlockSpec((None, tl, TOP_K), lambda bi, li: (bi, li, 0)),
            pl.BlockSpec((None, N_MOD, d), lambda bi, li: (mod_row0 + bi, 0, 0)),
            const((d, ds)),
            const((d, ds)),
            const((ds, d)),
            const((1, d)),
        ],
        out_specs=tile,
        out_shape=jax.ShapeDtypeStruct((b, l, d), F32),
        compiler_params=_params("parallel", "parallel"),
        name="combine_final",
    )(x1, h2, y_tok, gate, mod3, w_s_gate, w_s_up, w_s_down, g_final.reshape(1, d))


def _group_forward(x3, mod3, mod_row0, hist, start, past, wts, g_final, tiles):
    b, l, d = x3.shape
    tl, t_route, w_outer = tiles["tl"], tiles["t_route"], tiles["w_outer"]
    dp = wts["w_pool"].shape[0] * wts["w_pool"].shape[1]
    da = wts["w_up_b"].shape[0]
    nh = da // HEAD_DIM
    t = b * l

    assert dp == da, "u, q, k, v must each be one column tile of the input projection"
    scale = HEAD_DIM ** -0.5 * (LOG2_E if past is None else 1.0)
    u, qb, k, kb, v, vb, gates, logf_t = _in_proj(x3, wts["g_mix"], mod3, mod_row0, wts["w_fl_t"], wts["b_f"],
                                                  wts["w_cat"], scale, tiles["nb_in"], tl)
    mm = functools.partial(_mm, tl=tl, tn=1024, w_outer=w_outer)

    y_a = _pool_mixer(u, hist, wts["w_pool"], wts["s_pool"], start, tl)

    if past is None:
        y_b = _attn_prompt(qb, kb, vb, _cumsum_last(logf_t), tiles["tq"])
    else:
        cache_k, cache_v, cache_logf = past
        p = cache_k.shape[1]
        nf = -(-(p + l) // 128) * 128
        lf_all = jnp.concatenate([jnp.swapaxes(cache_logf.astype(F32), 1, 2), logf_t,
                                  jnp.zeros((b, nh, nf - p - l), F32)], axis=2)
        y_b = _attn_sample(qb, kb, vb, cache_k.reshape(b, p, da), cache_v.reshape(b, p, da), _cumsum_last(lf_all))

    tile_n = lambda off: (lambda bi, li, j: (bi, li, off + j))
    (m_a,) = mm(y_a, wts["w_up_a"], n=d, extras=[(gates, (None, tl, 1024), tile_n(0))],
                epilogue=lambda acc, ga: (ga.astype(F32) * acc,), out_dtypes=(BF16,), name="up_a")
    (merged,) = mm(y_b, wts["w_up_b"], n=d,
                   extras=[(gates, (None, tl, 1024), tile_n(d // 1024)), (m_a, (None, tl, 1024), tile_n(0))],
                   epilogue=lambda acc, gb, ma: (gb.astype(F32) * acc + ma.astype(F32),), out_dtypes=(BF16,),
                   name="up_b")
    mod_spec = (mod3, (None, N_MOD, 1024), lambda bi, li, j: (mod_row0 + bi, 0, j))
    (x1,) = mm(merged, wts["w_out"], n=d, extras=[(x3, (None, tl, 1024), tile_n(0)), mod_spec],
               epilogue=lambda acc, xr, m: (xr + m[2:3, :] * acc,), out_dtypes=(F32,), name="out_proj")

    h2, h2_packed = _norm_mod(x1, wts["g_ffn"], mod3, mod_row0, 3, 4, tl)
    idx_t, gate_t, rank_t, counts = _router(h2.reshape(t, d), wts["w_router_t"], wts["b_router"], t_route)

    n_exp = counts.shape[0]
    n_slots = t * TOP_K
    n_blocks = n_slots // MOE_ROWS + n_exp
    cap = n_blocks * MOE_ROWS
    padded = (counts + MOE_ROWS - 1) // MOE_ROWS * MOE_ROWS
    pad_end = jnp.cumsum(padded)
    pad_start = pad_end - padded
    blk_row0 = jnp.arange(n_blocks, dtype=I32) * MOE_ROWS
    blk_e = jnp.minimum(jnp.sum(pad_end[None, :] <= blk_row0[:, None], axis=1), n_exp - 1).astype(I32)
    blk_valid = jnp.clip(pad_start[blk_e] + counts[blk_e] - blk_row0, 0, MOE_ROWS).astype(I32)
    n_used = (pad_end[-1:] // MOE_ROWS).astype(I32)
    dest = rank_t + jnp.sum(jnp.where(idx_t[:, :, None] == jnp.arange(n_exp, dtype=I32), pad_start, 0), axis=2)

    win_d = min(SC_WINDOW, t // SC_WORKERS)
    dest_win = dest.reshape(TOP_K, t // win_d, win_d).transpose(1, 0, 2)
    row_shape = (PACK_CHUNKS, LANES)
    x_sorted = _sc_scatter_rows(h2_packed.reshape((t,) + row_shape), dest_win, cap)
    y_sorted = _moe_experts(x_sorted.reshape(cap * PACK_CHUNKS, LANES), blk_e, blk_valid, n_used,
                            wts["w_e_gate"], wts["w_e_up"], wts["w_e_down"])
    win_c = min(SC_WINDOW, n_slots // SC_WORKERS)
    y_tok = _sc_gather_rows(y_sorted.reshape((cap,) + row_shape), dest.reshape(n_slots // win_c, win_c))
    y_tok = y_tok.reshape(TOP_K, b, l * PACK_CHUNKS, LANES)

    y = _final(x1, h2, y_tok, gate_t.T.reshape(b, l, TOP_K), mod3, mod_row0,
               wts["w_s_gate"], wts["w_s_up"], wts["w_s_down"], g_final, tiles["tl_final"])

    new_pool = u[:, l - POOL_HIST:, :]
    return y, new_pool, k.reshape(b, l, nh, HEAD_DIM), v.reshape(b, l, nh, HEAD_DIM), jnp.swapaxes(logf_t, 1, 2)


def kernel(x_prompt, x_sample, cache_pool, cache_k, cache_v, cache_logf, c_prompt, c_sample, w_ada, b_ada, g_mix, w_in, b_f, w_pool, s_pool, w_up_a, w_up_b, w_out, g_ffn, w_router, b_router, w_e_gate, w_e_up, w_e_down, w_s_gate, w_s_up, w_s_down, g_final):
    depth = w_ada.shape[0]
    assert depth == 1, "a single layer is supported"
    bp, lp, d = x_prompt.shape
    bs, ls, _ = x_sample.shape
    dp = w_pool.shape[1] * w_pool.shape[2]
    da = w_up_b.shape[1]
    nh = da // HEAD_DIM
    assert bp <= MOD_ROWS_SAMPLE

    lyr = 0
    one = lambda a: a.reshape(a.shape[1:])
    c_all = jnp.zeros((MOD_ROWS_SAMPLE + bs, d), F32).at[:bp].set(c_prompt).at[MOD_ROWS_SAMPLE:].set(c_sample)
    mod3 = _ada_mod(c_all, one(w_ada), one(b_ada)).reshape(c_all.shape[0], N_MOD, d)

    n_main = dp + 3 * da
    w_in1 = one(w_in)
    wts = dict(
        g_mix=one(g_mix), g_ffn=one(g_ffn), b_f=one(b_f), s_pool=one(s_pool), b_router=one(b_router),
        w_cat=jnp.concatenate([w_in1[:, :n_main], w_in1[:, n_main + nh:]], axis=1).astype(BF16),
        w_fl_t=w_in1[:, n_main:n_main + nh].T.astype(BF16),
        w_pool=one(w_pool).astype(BF16),
        w_up_a=one(w_up_a).astype(BF16), w_up_b=one(w_up_b).astype(BF16), w_out=one(w_out).astype(BF16),
        w_router_t=one(w_router).T.astype(BF16),
        w_e_gate=one(w_e_gate), w_e_up=one(w_e_up), w_e_down=one(w_e_down),
        w_s_gate=one(w_s_gate).astype(BF16), w_s_up=one(w_s_up).astype(BF16), w_s_down=one(w_s_down).astype(BF16),
    )

    tiles_p = dict(tl=min(512, lp), tq=min(512, lp), t_route=min(512, bp * lp), w_outer=False,
                   tl_final=min(256, lp), nb_in=1)
    tiles_s = dict(tl=ls, t_route=bs * ls, w_outer=True, tl_final=ls, nb_in=bs)

    hist_p = jnp.zeros((bp, HALO, dp), F32)
    y_p, pool_p, k_p, v_p, f_p = _group_forward(x_prompt, mod3, MOD_ROWS_PROMPT, hist_p, 0, None, wts, g_final,
                                                tiles_p)
    hist_s = jnp.pad(one(cache_pool), ((0, 0), (HALO - POOL_HIST, 0), (0, 0)))
    past = (one(cache_k), one(cache_v), one(cache_logf))
    y_s, pool_s, k_s, v_s, f_s = _group_forward(x_sample, mod3, MOD_ROWS_SAMPLE, hist_s, cache_k.shape[2], past,
                                                wts, g_final, tiles_s)
    stack = lambda a: a[None]
    return (y_p, y_s, stack(pool_p), stack(k_p), stack(v_p), stack(f_p),
            stack(pool_s), stack(k_s), stack(v_s), stack(f_s))
```

```python
import functools

import jax
import jax.numpy as jnp
from jax import lax
from jax.experimental import pallas as pl
from jax.experimental.pallas import tpu as pltpu
from jax.experimental.pallas import tpu_sc as plsc

F32 = jnp.float32
BF16 = jnp.bfloat16
I32 = jnp.int32
U32 = jnp.uint32

EPS = 1e-6
N_MOD = 6
POOL_WINDOWS = (2, 4, 8, 16)
POOL_HIST = max(POOL_WINDOWS) - 1
HALO = 16
HEAD_DIM = 128
TOP_K = 8
ROUTED_SCALE = 2.5
MOE_ROWS = 512
MOE_CHAIN_ROWS = 256
MOE_CHUNKS = 2
MOD_ROWS_PROMPT = 0
MOD_ROWS_SAMPLE = 8
VMEM_LIMIT_BYTES = 52 * 1024 * 1024
NEG_INF = float("-inf")
LOG2_E = 1.4426950408889634
LANES = 128
PACK_CHUNKS = 8
SC_CORES = 2
SC_SUBCORES = 16
SC_WORKERS = SC_CORES * SC_SUBCORES
SC_WINDOW = 32
NT_DIMS = (((1,), (1,)), ((), ()))


def _params(*semantics):
    return pltpu.CompilerParams(dimension_semantics=semantics, vmem_limit_bytes=VMEM_LIMIT_BYTES)


def _silu(x):
    return x * jax.nn.sigmoid(x)


def _store_packed(ref, x, row0=0):
    rows, width = x.shape
    half = width // 2
    bits = pltpu.bitcast(x.astype(BF16).astype(F32), U32)
    packed = (bits[:, :half] >> 16) | (bits[:, half:] & jnp.uint32(0xFFFF0000))
    for j in range(PACK_CHUNKS):
        ref[pl.ds(row0 * PACK_CHUNKS + j, rows, stride=PACK_CHUNKS), :] = packed[:, j * LANES:(j + 1) * LANES]


def _load_packed(ref, rows, valid=None, row0=0):
    lo, hi = [], []
    for j in range(PACK_CHUNKS):
        p = ref[pl.ds(row0 * PACK_CHUNKS + j, rows, stride=PACK_CHUNKS), :]
        if valid is not None:
            p = jnp.where(row0 + lax.broadcasted_iota(I32, p.shape, 0) < valid, p, jnp.uint32(0))
        lo.append(pltpu.bitcast(p << 16, F32))
        hi.append(pltpu.bitcast(p & jnp.uint32(0xFFFF0000), F32))
    return jnp.concatenate(lo + hi, axis=1)


def _ada_kernel(c_ref, w_ref, b_ref, o_ref):
    a = _silu(c_ref[...]).astype(BF16)
    o_ref[...] = jnp.dot(a, w_ref[...].astype(BF16), preferred_element_type=F32) + b_ref[...]


def _ada_mod(c_all, w_ada, b_ada):
    rows, d = c_all.shape
    n = w_ada.shape[1]
    tn = 1024
    return pl.pallas_call(
        _ada_kernel,
        grid=(n // tn,),
        in_specs=[
            pl.BlockSpec((rows, d), lambda j: (0, 0)),
            pl.BlockSpec((d, tn), lambda j: (0, j)),
            pl.BlockSpec((1, tn), lambda j: (0, j)),
        ],
        out_specs=pl.BlockSpec((rows, tn), lambda j: (0, j)),
        out_shape=jax.ShapeDtypeStruct((rows, n), F32),
        compiler_params=_params("parallel"),
        name="ada_mod",
    )(c_all, w_ada, b_ada.reshape(1, n))


def _modulated_norm(x, g, m, shift_row, scale_row):
    ms = jnp.mean(x * x, axis=-1, keepdims=True)
    return x * lax.rsqrt(ms + EPS) * g * (1.0 + m[scale_row:scale_row + 1, :]) + m[shift_row:shift_row + 1, :]


def _norm_kernel(x_ref, g_ref, mod_ref, h_ref, hp_ref, *, shift_row, scale_row):
    h = _modulated_norm(x_ref[...], g_ref[...], mod_ref[...], shift_row, scale_row)
    h_ref[...] = h.astype(BF16)
    _store_packed(hp_ref, h)


def _norm_mod(x3, g, mod3, mod_row0, shift_row, scale_row, tl):
    b, l, d = x3.shape
    n_l = l // tl
    return pl.pallas_call(
        functools.partial(_norm_kernel, shift_row=shift_row, scale_row=scale_row),
        grid=(b, n_l),
        in_specs=[
            pl.BlockSpec((None, tl, d), lambda bi, li: (bi, li, 0)),
            pl.BlockSpec((1, d), lambda bi, li: (0, 0)),
            pl.BlockSpec((None, N_MOD, d), lambda bi, li: (mod_row0 + bi, 0, 0)),
        ],
        out_specs=[
            pl.BlockSpec((None, tl, d), lambda bi, li: (bi, li, 0)),
            pl.BlockSpec((tl * PACK_CHUNKS, LANES), lambda bi, li: (bi * n_l + li, 0)),
        ],
        out_shape=[
            jax.ShapeDtypeStruct((b, l, d), BF16),
            jax.ShapeDtypeStruct((b * l * PACK_CHUNKS, LANES), U32),
        ],
        compiler_params=_params("parallel", "parallel"),
        name="norm_mod",
    )(x3, g.reshape(1, d), mod3)


def _in_proj_kernel(x_ref, g_ref, mod_ref, wfl_ref, bf_ref, w_ref,
                    u_ref, q_ref, k32_ref, kb_ref, v32_ref, vb_ref, gates_ref, lf_ref, h_ref, *, q_scale):
    j = pl.program_id(2)
    nb, tl, _ = x_ref.shape
    nh = wfl_ref.shape[0]

    @pl.when(j == 0)
    def _():
        for bi in range(nb):
            hb = _modulated_norm(x_ref[bi], g_ref[...], mod_ref[bi], 0, 1).astype(BF16)
            h_ref[bi * tl:(bi + 1) * tl, :] = hb
            fl = lax.dot_general(wfl_ref[...], hb, NT_DIMS, preferred_element_type=F32) + bf_ref[...]
            lf_ref[bi] = jnp.minimum(fl, 0.0) - jnp.log(1.0 + jnp.exp(-jnp.abs(fl)))

    def tile_rows():
        acc = jnp.dot(h_ref[...], w_ref[...], preferred_element_type=F32)
        return [acc[bi * tl:(bi + 1) * tl, :] for bi in range(nb)]

    def store_heads(f32_ref, bf_ref_):
        for bi, r in enumerate(tile_rows()):
            bf_ref_[bi] = r.astype(BF16)
            for hd in range(nh):
                f32_ref[bi, pl.ds(hd, tl, stride=nh), :] = r[:, hd * HEAD_DIM:(hd + 1) * HEAD_DIM]

    @pl.when(j == 0)
    def _():
        for bi, r in enumerate(tile_rows()):
            u_ref[bi] = r

    @pl.when(j == 1)
    def _():
        for bi, r in enumerate(tile_rows()):
            q_ref[bi] = (r * q_scale).astype(BF16)

    @pl.when(j == 2)
    def _():
        store_heads(k32_ref, kb_ref)

    @pl.when(j == 3)
    def _():
        store_heads(v32_ref, vb_ref)

    @pl.when(j >= 4)
    def _():
        for bi, r in enumerate(tile_rows()):
            gates_ref[bi] = jax.nn.sigmoid(r).astype(BF16)


def _in_proj(x3, g, mod3, mod_row0, w_fl_t, b_f, w_cat, q_scale, nb, tl):
    b, l, d = x3.shape
    nh = w_fl_t.shape[0]
    tn = nh * HEAD_DIM
    n_tiles = w_cat.shape[1] // tn
    n_gate_tiles = n_tiles - 4
    assert mod_row0 % nb == 0 and b % nb == 0
    tile = lambda width: pl.BlockSpec((nb, tl, width), lambda bb, li, j: (bb, li, 0))
    heads = pl.BlockSpec((nb, tl * nh, HEAD_DIM), lambda bb, li, j: (bb, li, 0))
    return pl.pallas_call(
        functools.partial(_in_proj_kernel, q_scale=q_scale),
        grid=(b // nb, l // tl, n_tiles),
        in_specs=[
            tile(d),
            pl.BlockSpec((1, d), lambda bb, li, j: (0, 0)),
            pl.BlockSpec((nb, N_MOD, d), lambda bb, li, j: (mod_row0 // nb + bb, 0, 0)),
            pl.BlockSpec((nh, d), lambda bb, li, j: (0, 0)),
            pl.BlockSpec((nh, 1), lambda bb, li, j: (0, 0)),
            pl.BlockSpec((d, tn), lambda bb, li, j: (0, j)),
        ],
        out_specs=[
            tile(tn), tile(tn), heads, tile(tn), heads, tile(tn),
            pl.BlockSpec((nb, tl, tn), lambda bb, li, j: (bb, li, jnp.maximum(j - 4, 0))),
            pl.BlockSpec((nb, nh, tl), lambda bb, li, j: (bb, 0, li)),
        ],
        out_shape=[
            jax.ShapeDtypeStruct((b, l, tn), F32),
            jax.ShapeDtypeStruct((b, l, tn), BF16),
            jax.ShapeDtypeStruct((b, l * nh, HEAD_DIM), F32),
            jax.ShapeDtypeStruct((b, l, tn), BF16),
            jax.ShapeDtypeStruct((b, l * nh, HEAD_DIM), F32),
            jax.ShapeDtypeStruct((b, l, tn), BF16),
            jax.ShapeDtypeStruct((b, l, n_gate_tiles * tn), BF16),
            jax.ShapeDtypeStruct((b, nh, l), F32),
        ],
        scratch_shapes=[pltpu.VMEM((nb * tl, d), BF16)],
        compiler_params=_params("parallel", "parallel", "arbitrary"),
        name="in_proj",
    )(x3, g.reshape(1, d), mod3, w_fl_t, b_f.reshape(nh, 1), w_cat)


def _mm_kernel(*refs, n_extra, epilogue):
    x_ref, w_ref = refs[0], refs[1]
    extras = refs[2:2 + n_extra]
    outs = refs[2 + n_extra:]
    acc = jnp.dot(x_ref[...], w_ref[...], preferred_element_type=F32)
    res = epilogue(acc, *[e[...] for e in extras])
    for o_ref, r in zip(outs, res):
        o_ref[...] = r.astype(o_ref.dtype)


def _mm(x3, w, *, n, w_col0=0, extras=(), epilogue, out_dtypes, tl, tn, w_outer, name):
    b, l, k = x3.shape
    nl, nj = l // tl, n // tn
    j0 = w_col0 // tn
    if w_outer:
        grid = (nj, b, nl)
        wrap = lambda fn: (lambda j, bi, li: fn(bi, li, j))
    else:
        grid = (b, nl, nj)
        wrap = lambda fn: (lambda bi, li, j: fn(bi, li, j))
    in_specs = [
        pl.BlockSpec((None, tl, k), wrap(lambda bi, li, j: (bi, li, 0))),
        pl.BlockSpec((k, tn), wrap(lambda bi, li, j: (0, j0 + j))),
    ]
    args = [x3, w]
    for arr, shape, fn in extras:
        in_specs.append(pl.BlockSpec(shape, wrap(fn)))
        args.append(arr)
    out = pl.pallas_call(
        functools.partial(_mm_kernel, n_extra=len(extras), epilogue=epilogue),
        grid=grid,
        in_specs=in_specs,
        out_specs=[pl.BlockSpec((None, tl, tn), wrap(lambda bi, li, j: (bi, li, j))) for _ in out_dtypes],
        out_shape=[jax.ShapeDtypeStruct((b, l, n), dt) for dt in out_dtypes],
        compiler_params=_params("parallel", "parallel", "parallel"),
        name=name,
    )(*args)
    return out


def _pool_kernel(u_ref, halo_ref, hist_ref, wp_ref, sp_ref, o_ref, full_ref, *, tl, start, gw):
    i = pl.program_id(1)
    full_ref[0:HALO, :] = jnp.where(i == 0, hist_ref[...], halo_ref[...])
    full_ref[HALO:HALO + tl, :] = u_ref[...]
    pos = start + i * tl + lax.broadcasted_iota(I32, (tl, 1), 0)
    for g, w in enumerate(POOL_WINDOWS):
        c0, c1 = g * gw, (g + 1) * gw
        cur = full_ref[HALO:HALO + tl, c0:c1]
        win = cur
        for d in range(1, w):
            win = win + full_ref[HALO - d:HALO - d + tl, c0:c1]
        cnt = jnp.minimum(pos + 1, w).astype(F32)
        dmean = win / cnt - cur
        y = jnp.dot(dmean.astype(BF16), wp_ref[g], preferred_element_type=F32) * sp_ref[:, c0:c1]
        o_ref[:, c0:c1] = y.astype(o_ref.dtype)


def _pool_mixer(u3, hist, w_pool, s_pool, start, tl):
    b, l, dp = u3.shape
    gw = dp // len(POOL_WINDOWS)
    halo_blocks = tl // HALO
    return pl.pallas_call(
        functools.partial(_pool_kernel, tl=tl, start=start, gw=gw),
        grid=(b, l // tl),
        in_specs=[
            pl.BlockSpec((None, tl, dp), lambda bi, li: (bi, li, 0)),
            pl.BlockSpec((None, HALO, dp), lambda bi, li: (bi, jnp.maximum(li * halo_blocks - 1, 0), 0)),
            pl.BlockSpec((None, HALO, dp), lambda bi, li: (bi, 0, 0)),
            pl.BlockSpec((len(POOL_WINDOWS), gw, gw), lambda bi, li: (0, 0, 0)),
            pl.BlockSpec((1, dp), lambda bi, li: (0, 0)),
        ],
        out_specs=pl.BlockSpec((None, tl, dp), lambda bi, li: (bi, li, 0)),
        out_shape=jax.ShapeDtypeStruct((b, l, dp), BF16),
        scratch_shapes=[pltpu.VMEM((HALO + tl, dp), F32)],
        compiler_params=_params("parallel", "parallel"),
        name="pool_mixer",
    )(u3, u3, hist, w_pool, s_pool.reshape(1, dp))


def _cumsum_kernel(x_ref, o_ref):
    x = x_ref[...]
    n = x.shape[1]
    idx = lax.broadcasted_iota(I32, x.shape, 1)
    s = 1
    while s < n:
        x = x + jnp.where(idx >= s, pltpu.roll(x, s, 1), 0.0)
        s *= 2
    o_ref[...] = x


def _cumsum_last(x3):
    b, h, n = x3.shape
    return pl.pallas_call(
        _cumsum_kernel,
        grid=(b,),
        in_specs=[pl.BlockSpec((None, h, n), lambda bi: (bi, 0, 0))],
        out_specs=pl.BlockSpec((None, h, n), lambda bi: (bi, 0, 0)),
        out_shape=jax.ShapeDtypeStruct((b, h, n), F32),
        compiler_params=_params("parallel"),
        name="cumsum_logf",
    )(x3)


def _attn_prompt_kernel(q_ref, k_ref, v_ref, fk_ref, o_ref, s0_ref, s1_ref, m_ref, l_ref, acc_ref, *, tq):
    qi = pl.program_id(2)
    q = q_ref[...]
    n_chunks = tq // LANES

    def produce(j, s_ref):
        off = pl.multiple_of(j * tq, tq)
        s = lax.dot_general(q, k_ref[pl.ds(off, tq), :], NT_DIMS, preferred_element_type=F32)
        s_ref[...] = s - fk_ref[j] * LOG2_E

    def consume(j, s_ref, diagonal):
        s = s_ref[...]
        if diagonal:
            row = lax.broadcasted_iota(I32, (tq, tq), 0)
            col = lax.broadcasted_iota(I32, (tq, tq), 1)
            s = jnp.where(col <= row, s, NEG_INF)
        chunks = [s[:, c * LANES:(c + 1) * LANES] for c in range(n_chunks)]
        m_old = m_ref[...]
        m_new = jnp.maximum(m_old, jnp.max(functools.reduce(jnp.maximum, chunks), axis=1, keepdims=True))
        alpha = jnp.exp2(m_old - m_new)
        p_chunks = [jnp.exp2(ch - m_new) for ch in chunks]
        l_ref[...] = alpha * l_ref[...] + functools.reduce(lambda a, c: a + c, p_chunks)
        p = jnp.concatenate(p_chunks, axis=1).astype(BF16)
        v = v_ref[pl.ds(pl.multiple_of(j * tq, tq), tq), :]
        acc_ref[...] = alpha * acc_ref[...] + jnp.dot(p, v, preferred_element_type=F32)
        m_ref[...] = m_new

    m_ref[...] = jnp.full(m_ref.shape, NEG_INF, F32)
    l_ref[...] = jnp.zeros(l_ref.shape, F32)
    acc_ref[...] = jnp.zeros(acc_ref.shape, F32)
    produce(0, s0_ref)

    def pair(jj, carry):
        j0 = 2 * jj
        produce(j0 + 1, s1_ref)
        consume(j0, s0_ref, False)
        produce(j0 + 2, s0_ref)
        consume(j0 + 1, s1_ref, False)
        return carry

    lax.fori_loop(0, qi // 2, pair, 0)

    @pl.when(qi % 2 == 1)
    def _():
        produce(qi, s1_ref)
        consume(qi - 1, s0_ref, False)
        consume(qi, s1_ref, True)

    @pl.when(qi % 2 == 0)
    def _():
        consume(qi, s0_ref, True)

    l = jnp.sum(l_ref[...], axis=1, keepdims=True)
    o_ref[...] = (acc_ref[...] / l).astype(o_ref.dtype)


def _attn_prompt(q3, k3, v3, f3, tq):
    b, l, da = q3.shape
    nh = da // HEAD_DIM
    nk = l // tq
    fk = f3.reshape(b * nh, nk, 1, tq)
    return pl.pallas_call(
        functools.partial(_attn_prompt_kernel, tq=tq),
        grid=(b, nh, nk),
        in_specs=[
            pl.BlockSpec((None, tq, HEAD_DIM), lambda bi, hi, qi: (bi, qi, hi)),
            pl.BlockSpec((None, l, HEAD_DIM), lambda bi, hi, qi: (bi, 0, hi)),
            pl.BlockSpec((None, l, HEAD_DIM), lambda bi, hi, qi: (bi, 0, hi)),
            pl.BlockSpec((None, nk, 1, tq), lambda bi, hi, qi: (bi * nh + hi, 0, 0, 0)),
        ],
        out_specs=pl.BlockSpec((None, tq, HEAD_DIM), lambda bi, hi, qi: (bi, qi, hi)),
        out_shape=jax.ShapeDtypeStruct((b, l, da), BF16),
        scratch_shapes=[
            pltpu.VMEM((tq, tq), F32),
            pltpu.VMEM((tq, tq), F32),
            pltpu.VMEM((tq, LANES), F32),
            pltpu.VMEM((tq, LANES), F32),
            pltpu.VMEM((tq, HEAD_DIM), F32),
        ],
        compiler_params=_params("parallel", "parallel", "parallel"),
        name="fox_prompt",
    )(q3, k3, v3, fk)


def _attn_sample_kernel(q_ref, kp_ref, vp_ref, kn_ref, vn_ref, f_ref, o_ref, *, past, lq):
    q = q_ref[...]
    f = f_ref[...]
    s1 = lax.dot_general(q, kp_ref[...].astype(BF16), NT_DIMS, preferred_element_type=F32) - f[:, :past]
    s2 = lax.dot_general(q, kn_ref[...], NT_DIMS, preferred_element_type=F32) - f[:, past:past + lq]
    row = lax.broadcasted_iota(I32, (lq, lq), 0)
    col = lax.broadcasted_iota(I32, (lq, lq), 1)
    s2 = jnp.where(col <= row, s2, NEG_INF)
    m = jnp.maximum(jnp.max(s1, axis=1, keepdims=True), jnp.max(s2, axis=1, keepdims=True))
    p1 = jnp.exp(s1 - m)
    p2 = jnp.exp(s2 - m)
    l = jnp.sum(p1, axis=1, keepdims=True) + jnp.sum(p2, axis=1, keepdims=True)
    acc = jnp.dot(p1.astype(BF16), vp_ref[...].astype(BF16), preferred_element_type=F32)
    acc = acc + jnp.dot(p2.astype(BF16), vn_ref[...], preferred_element_type=F32)
    o_ref[...] = (acc / l).astype(o_ref.dtype)


def _attn_sample(q3, kn3, vn3, kp3, vp3, f3):
    b, lq, da = q3.shape
    past = kp3.shape[1]
    nh = da // HEAD_DIM
    nf = f3.shape[2]
    new_spec = pl.BlockSpec((None, lq, HEAD_DIM), lambda bi, hi: (bi, 0, hi))
    past_spec = pl.BlockSpec((None, past, HEAD_DIM), lambda bi, hi: (bi, 0, hi))
    return pl.pallas_call(
        functools.partial(_attn_sample_kernel, past=past, lq=lq),
        grid=(b, nh),
        in_specs=[new_spec, past_spec, past_spec, new_spec, new_spec,
                  pl.BlockSpec((None, 1, nf), lambda bi, hi: (bi * nh + hi, 0, 0))],
        out_specs=new_spec,
        out_shape=jax.ShapeDtypeStruct((b, lq, da), BF16),
        compiler_params=_params("parallel", "parallel"),
        name="fox_sample",
    )(q3, kp3, vp3, kn3, vn3, f3.reshape(b * nh, 1, nf))


def _router_kernel(h_ref, wr_ref, br_ref, tri_ref, cnt0_ref, idx_ref, gate_ref, rank_ref, cnt_ref, carry_ref):
    @pl.when(pl.program_id(0) == 0)
    def _():
        carry_ref[...] = cnt0_ref[...]

    n_exp, tm = wr_ref.shape[0], h_ref.shape[0]
    scores = jax.nn.sigmoid(lax.dot_general(wr_ref[...], h_ref[...], NT_DIMS, preferred_element_type=F32))
    work = scores + br_ref[...]
    eidx = lax.broadcasted_iota(I32, (n_exp, tm), 0)
    picked = jnp.zeros((n_exp, tm), F32)
    onehots, gates = [], []
    for k in range(TOP_K):
        best = jnp.max(work, axis=0, keepdims=True)
        first = jnp.min(jnp.where(work == best, eidx, n_exp), axis=0, keepdims=True)
        onehot = eidx == first
        idx_ref[k:k + 1, :] = first
        gates.append(jnp.sum(jnp.where(onehot, scores, 0.0), axis=0, keepdims=True))
        onehots.append(onehot)
        picked = picked + jnp.where(onehot, 1.0, 0.0)
        work = jnp.where(onehot, NEG_INF, work)
    norm = ROUTED_SCALE / functools.reduce(lambda a, c: a + c, gates)
    cum = jnp.dot(picked.astype(BF16), tri_ref[...], preferred_element_type=F32)
    before = cum - picked + carry_ref[:, 0:1]
    for k in range(TOP_K):
        gate_ref[k:k + 1, :] = gates[k] * norm
        rank_ref[k:k + 1, :] = jnp.sum(jnp.where(onehots[k], before, 0.0), axis=0, keepdims=True).astype(I32)
    carry_ref[...] = carry_ref[...] + cum[:, tm - 1:tm]
    cnt_ref[...] = carry_ref[...]


def _router(h2, tok0, ntok, w_router_t, b_router, cnt0, tm):
    d = h2.shape[1]
    n_exp = w_router_t.shape[0]
    tri = jnp.triu(jnp.ones((tm, tm), BF16))
    tok_spec = pl.BlockSpec((TOP_K, tm), lambda i: (0, i))
    cnt_spec = pl.BlockSpec((n_exp, LANES), lambda i: (0, 0))
    return pl.pallas_call(
        _router_kernel,
        grid=(ntok // tm,),
        in_specs=[
            pl.BlockSpec((tm, d), lambda i: (tok0 // tm + i, 0)),
            pl.BlockSpec((n_exp, d), lambda i: (0, 0)),
            pl.BlockSpec((n_exp, 1), lambda i: (0, 0)),
            pl.BlockSpec((tm, tm), lambda i: (0, 0)),
            cnt_spec,
        ],
        out_specs=[tok_spec, tok_spec, tok_spec, cnt_spec],
        out_shape=[
            jax.ShapeDtypeStruct((TOP_K, ntok), I32),
            jax.ShapeDtypeStruct((TOP_K, ntok), F32),
            jax.ShapeDtypeStruct((TOP_K, ntok), I32),
            jax.ShapeDtypeStruct((n_exp, LANES), F32),
        ],
        scratch_shapes=[pltpu.VMEM((n_exp, LANES), F32)],
        compiler_params=_params("arbitrary"),
        name="moe_router",
    )(h2, w_router_t, b_router.reshape(n_exp, 1), tri, cnt0)


def _moe_kernel(be_ref, bv_ref, nu_ref, x_ref, wg_ref, wu_ref, wd_ref, y_ref, wg_bf, wu_bf, wd_bf):
    i = pl.program_id(0)
    active = i < nu_ref[0]

    @pl.when(active & ((i == 0) | (be_ref[i] != be_ref[jnp.maximum(i - 1, 0)])))
    def _():
        wg_bf[...] = wg_ref[...].astype(BF16)
        wu_bf[...] = wu_ref[...].astype(BF16)
        wd_bf[...] = wd_ref[...].astype(BF16)

    @pl.when(active)
    def _():
        for row0 in range(0, MOE_ROWS, MOE_CHAIN_ROWS):
            x = _load_packed(x_ref, MOE_CHAIN_ROWS, valid=bv_ref[i], row0=row0).astype(BF16)
            g = jnp.dot(x, wg_bf[...], preferred_element_type=F32)
            u = jnp.dot(x, wu_bf[...], preferred_element_type=F32)
            a = (_silu(g) * u).astype(BF16)
            _store_packed(y_ref, jnp.dot(a, wd_bf[...], preferred_element_type=F32), row0=row0)


def _moe_experts(x_sorted, blk_e, blk_valid, n_used, w_e_gate, w_e_up, w_e_down):
    cap = x_sorted.shape[0] // PACK_CHUNKS
    d, de = w_e_gate.shape[1:]
    n_blocks = cap // MOE_ROWS
    row_map = lambda i, be, bv, nu: (jnp.minimum(i, nu[0] - 1), 0)
    w_map = lambda i, be, bv, nu: (be[i], 0, 0)
    return pl.pallas_call(
        _moe_kernel,
        grid_spec=pltpu.PrefetchScalarGridSpec(
            num_scalar_prefetch=3,
            grid=(n_blocks,),
            in_specs=[
                pl.BlockSpec((MOE_ROWS * PACK_CHUNKS, LANES), row_map),
                pl.BlockSpec((None, d, de), w_map),
                pl.BlockSpec((None, d, de), w_map),
                pl.BlockSpec((None, de, d), w_map),
            ],
            out_specs=pl.BlockSpec((MOE_ROWS * PACK_CHUNKS, LANES), row_map),
            scratch_shapes=[pltpu.VMEM((d, de), BF16), pltpu.VMEM((d, de), BF16), pltpu.VMEM((de, d), BF16)],
        ),
        out_shape=jax.ShapeDtypeStruct(x_sorted.shape, U32),
        compiler_params=_params("arbitrary"),
        name="moe_experts",
    )(blk_e, blk_valid, n_used, x_sorted, w_e_gate, w_e_up, w_e_down)


def _sc_mesh():
    return plsc.VectorSubcoreMesh(core_axis_name="c", subcore_axis_name="s")


def _sc_worker_id():
    return lax.axis_index("s") * SC_CORES + lax.axis_index("c")


def _sc_scatter_rows(segments, n_out):
    row_shape = segments[0][0].shape[1:]
    dtype = segments[0][0].dtype
    n_seg = len(segments)
    geom = [(tok0, dw.shape[0] // SC_WORKERS, dw.shape[2]) for _, tok0, dw in segments]
    scratch = []
    for _, n_win, win in geom:
        scratch += [pltpu.VMEM((n_win, TOP_K, win), I32), pltpu.VMEM((win,) + row_shape, dtype)]

    @functools.partial(
        pl.kernel, mesh=_sc_mesh(),
        out_type=jax.ShapeDtypeStruct((n_out,) + row_shape, dtype),
        scratch_types=scratch + [pltpu.SemaphoreType.DMA],
    )
    def scatter_kernel(*refs):
        out_hbm = refs[2 * n_seg]
        sem = refs[-1]
        for s, (tok0, n_win, win) in enumerate(geom):
            rows_hbm, idx_hbm = refs[2 * s], refs[2 * s + 1]
            idx_v, rows_v = refs[2 * n_seg + 1 + 2 * s], refs[2 * n_seg + 2 + 2 * s]
            first = _sc_worker_id() * n_win
            pltpu.sync_copy(idx_hbm.at[pl.ds(first, n_win)], idx_v)

            @pl.loop(0, n_win)
            def _(w):
                pltpu.sync_copy(rows_hbm.at[pl.ds(tok0 + (first + w) * win, win)], rows_v)
                copies = [pltpu.make_async_copy(rows_v, out_hbm.at[idx_v.at[w, k]], sem) for k in range(TOP_K)]
                for c in copies:
                    c.start()
                for c in copies:
                    c.wait()

    args = []
    for rows, _, dest_win in segments:
        args += [rows, dest_win]
    return scatter_kernel(*args)


def _sc_gather_rows(table, idx_wins):
    row_shape = table.shape[1:]
    n_seg = len(idx_wins)
    geom = [(iw.shape[0] // SC_WORKERS, iw.shape[1]) for iw in idx_wins]
    scratch = []
    for n_win, win in geom:
        scratch += [pltpu.VMEM((n_win, win), I32), pltpu.VMEM((win,) + row_shape, table.dtype)]

    @functools.partial(
        pl.kernel, mesh=_sc_mesh(),
        out_type=[jax.ShapeDtypeStruct((iw.shape[0] * iw.shape[1],) + row_shape, table.dtype) for iw in idx_wins],
        scratch_types=scratch + [pltpu.SemaphoreType.DMA],
    )
    def gather_kernel(*refs):
        table_hbm = refs[0]
        sem = refs[-1]
        for s, (n_win, win) in enumerate(geom):
            idx_hbm, out_hbm = refs[1 + s], refs[1 + n_seg + s]
            idx_v, rows_v = refs[1 + 2 * n_seg + 2 * s], refs[2 + 2 * n_seg + 2 * s]
            first = _sc_worker_id() * n_win
            pltpu.sync_copy(idx_hbm.at[pl.ds(first, n_win)], idx_v)

            @pl.loop(0, n_win)
            def _(w):
                pltpu.async_copy(table_hbm.at[idx_v.at[w]], rows_v, sem).wait()
                pltpu.sync_copy(rows_v, out_hbm.at[pl.ds((first + w) * win, win)])

    return gather_kernel(table, *idx_wins)


def _final_kernel(x1_ref, h2_ref, yt_ref, gate_ref, mod_ref, wsg_ref, wsu_ref, wsd_ref, gf_ref, *rest):
    o_ref = rest[-1]
    h2 = h2_ref[...]
    g = jnp.dot(h2, wsg_ref[...], preferred_element_type=F32)
    u = jnp.dot(h2, wsu_ref[...], preferred_element_type=F32)
    f = jnp.dot((_silu(g) * u).astype(BF16), wsd_ref[...], preferred_element_type=F32)
    gate = gate_ref[...]
    tl = h2.shape[0]
    for k in range(TOP_K):
        f = f + gate[:, k:k + 1] * _load_packed(yt_ref.at[k], tl)
    x2 = x1_ref[...] + mod_ref[N_MOD - 1:N_MOD, :] * f
    ms = jnp.mean(x2 * x2, axis=-1, keepdims=True)
    o_ref[...] = x2 * lax.rsqrt(ms + EPS) * gf_ref[...]


def _final(x1, h2, b0, y_tok, gate, mod3, mod_row0, w_s_gate, w_s_up, w_s_down, g_final, tl, y_prev=None):
    b, l, d = x1.shape
    nb = gate.shape[0]
    ds = w_s_gate.shape[1]
    tile = pl.BlockSpec((None, tl, d), lambda bi, li: (b0 + bi, li, 0))
    const = lambda shape: pl.BlockSpec(shape, lambda bi, li: (0,) * len(shape))
    in_specs = [
        tile,
        tile,
        pl.BlockSpec((TOP_K, None, tl * PACK_CHUNKS, LANES), lambda bi, li: (0, bi, li, 0)),
        pl.BlockSpec((None, tl, TOP_K), lambda bi, li: (bi, li, 0)),
        pl.BlockSpec((None, N_MOD, d), lambda bi, li: (mod_row0 + b0 + bi, 0, 0)),
        const((d, ds)),
        const((d, ds)),
        const((ds, d)),
        const((1, d)),
    ]
    args = [x1, h2, y_tok, gate, mod3, w_s_gate, w_s_up, w_s_down, g_final.reshape(1, d)]
    aliases = {}
    if y_prev is not None:
        in_specs.append(pl.BlockSpec(memory_space=pl.ANY))
        aliases = {len(args): 0}
        args.append(y_prev)
    return pl.pallas_call(
        _final_kernel,
        grid=(nb, l // tl),
        in_specs=in_specs,
        out_specs=tile,
        out_shape=jax.ShapeDtypeStruct((b, l, d), F32),
        input_output_aliases=aliases,
        compiler_params=_params("parallel", "parallel"),
        name="combine_final",
    )(*args)


def _group_mixer(x3, mod3, mod_row0, hist, start, past, wts, tiles):
    b, l, d = x3.shape
    tl, w_outer = tiles["tl"], tiles["w_outer"]
    dp = wts["w_pool"].shape[0] * wts["w_pool"].shape[1]
    da = wts["w_up_b"].shape[0]
    nh = da // HEAD_DIM
    t = b * l

    assert dp == da, "u, q, k, v must each be one column tile of the input projection"
    scale = HEAD_DIM ** -0.5 * (LOG2_E if past is None else 1.0)
    u, qb, k, kb, v, vb, gates, logf_t = _in_proj(x3, wts["g_mix"], mod3, mod_row0, wts["w_fl_t"], wts["b_f"],
                                                  wts["w_cat"], scale, tiles["nb_in"], tl)
    mm = functools.partial(_mm, tl=tl, tn=1024, w_outer=w_outer)

    y_a = _pool_mixer(u, hist, wts["w_pool"], wts["s_pool"], start, tl)

    if past is None:
        y_b = _attn_prompt(qb, kb, vb, _cumsum_last(logf_t), tiles["tq"])
    else:
        cache_k, cache_v, cache_logf = past
        p = cache_k.shape[1]
        nf = -(-(p + l) // 128) * 128
        lf_all = jnp.concatenate([jnp.swapaxes(cache_logf.astype(F32), 1, 2), logf_t,
                                  jnp.zeros((b, nh, nf - p - l), F32)], axis=2)
        y_b = _attn_sample(qb, kb, vb, cache_k.reshape(b, p, da), cache_v.reshape(b, p, da), _cumsum_last(lf_all))

    tile_n = lambda off: (lambda bi, li, j: (bi, li, off + j))
    (m_a,) = mm(y_a, wts["w_up_a"], n=d, extras=[(gates, (None, tl, 1024), tile_n(0))],
                epilogue=lambda acc, ga: (ga.astype(F32) * acc,), out_dtypes=(BF16,), name="up_a")
    (merged,) = mm(y_b, wts["w_up_b"], n=d,
                   extras=[(gates, (None, tl, 1024), tile_n(d // 1024)), (m_a, (None, tl, 1024), tile_n(0))],
                   epilogue=lambda acc, gb, ma: (gb.astype(F32) * acc + ma.astype(F32),), out_dtypes=(BF16,),
                   name="up_b")
    mod_spec = (mod3, (None, N_MOD, 1024), lambda bi, li, j: (mod_row0 + bi, 0, j))
    (x1,) = mm(merged, wts["w_out"], n=d, extras=[(x3, (None, tl, 1024), tile_n(0)), mod_spec],
               epilogue=lambda acc, xr, m: (xr + m[2:3, :] * acc,), out_dtypes=(F32,), name="out_proj")

    h2, h2_packed = _norm_mod(x1, wts["g_ffn"], mod3, mod_row0, 3, 4, tl)
    caches = (u[:, l - POOL_HIST:, :], k.reshape(b, l, nh, HEAD_DIM), v.reshape(b, l, nh, HEAD_DIM),
              jnp.swapaxes(logf_t, 1, 2))
    return dict(x1=x1, h2=h2, h2_packed=h2_packed, mod_row0=mod_row0, tiles=tiles), caches


def _moe_chunk(segments, wts):
    n_exp = wts["w_router_t"].shape[0]
    row_shape = (PACK_CHUNKS, LANES)
    counts_f = jnp.zeros((n_exp, LANES), F32)
    routed = []
    for grp, b0, nb in segments:
        b, l, d = grp["h2"].shape
        ntok = nb * l
        idx_t, gate_t, rank_t, counts_f = _router(grp["h2"].reshape(b * l, d), b0 * l, ntok, wts["w_router_t"],
                                                  wts["b_router"], counts_f, min(grp["tiles"]["t_route"], ntok))
        routed.append((idx_t, gate_t, rank_t, ntok))
    counts = counts_f[:, 0].astype(I32)

    n_slots = sum(r[3] for r in routed) * TOP_K
    n_blocks = -(-n_slots // MOE_ROWS) + n_exp
    cap = n_blocks * MOE_ROWS
    padded = (counts + MOE_ROWS - 1) // MOE_ROWS * MOE_ROWS
    pad_end = jnp.cumsum(padded)
    pad_start = pad_end - padded
    blk_row0 = jnp.arange(n_blocks, dtype=I32) * MOE_ROWS
    blk_e = jnp.minimum(jnp.sum(pad_end[None, :] <= blk_row0[:, None], axis=1), n_exp - 1).astype(I32)
    blk_valid = jnp.clip(pad_start[blk_e] + counts[blk_e] - blk_row0, 0, MOE_ROWS).astype(I32)
    n_used = (pad_end[-1:] // MOE_ROWS).astype(I32)

    scatter_segs, gather_idx, dests = [], [], []
    for (grp, b0, nb), (idx_t, _, rank_t, ntok) in zip(segments, routed):
        b, l, _ = grp["h2"].shape
        dest = rank_t + jnp.sum(jnp.where(idx_t[:, :, None] == jnp.arange(n_exp, dtype=I32), pad_start, 0), axis=2)
        win_d = min(SC_WINDOW, ntok // SC_WORKERS)
        dest_win = dest.reshape(TOP_K, ntok // win_d, win_d).transpose(1, 0, 2)
        scatter_segs.append((grp["h2_packed"].reshape((b * l,) + row_shape), b0 * l, dest_win))
        win_c = min(SC_WINDOW, ntok * TOP_K // SC_WORKERS)
        gather_idx.append(dest.reshape(ntok * TOP_K // win_c, win_c))

    x_sorted = _sc_scatter_rows(scatter_segs, cap)
    y_sorted = _moe_experts(x_sorted.reshape(cap * PACK_CHUNKS, LANES), blk_e, blk_valid, n_used,
                            wts["w_e_gate"], wts["w_e_up"], wts["w_e_down"])
    y_toks = _sc_gather_rows(y_sorted.reshape((cap,) + row_shape), gather_idx)
    out = []
    for (grp, b0, nb), (_, gate_t, _, _), y_tok in zip(segments, routed, y_toks):
        l = grp["h2"].shape[1]
        out.append((y_tok.reshape(TOP_K, nb, l * PACK_CHUNKS, LANES), gate_t.T.reshape(nb, l, TOP_K)))
    return out


def kernel(x_prompt, x_sample, cache_pool, cache_k, cache_v, cache_logf, c_prompt, c_sample, w_ada, b_ada, g_mix, w_in, b_f, w_pool, s_pool, w_up_a, w_up_b, w_out, g_ffn, w_router, b_router, w_e_gate, w_e_up, w_e_down, w_s_gate, w_s_up, w_s_down, g_final):
    depth = w_ada.shape[0]
    assert depth == 1, "a single layer is supported"
    bp, lp, d = x_prompt.shape
    bs, ls, _ = x_sample.shape
    dp = w_pool.shape[1] * w_pool.shape[2]
    da = w_up_b.shape[1]
    nh = da // HEAD_DIM
    assert bp <= MOD_ROWS_SAMPLE

    lyr = 0
    one = lambda a: a.reshape(a.shape[1:])
    c_all = jnp.zeros((MOD_ROWS_SAMPLE + bs, d), F32).at[:bp].set(c_prompt).at[MOD_ROWS_SAMPLE:].set(c_sample)
    mod3 = _ada_mod(c_all, one(w_ada), one(b_ada)).reshape(c_all.shape[0], N_MOD, d)

    n_main = dp + 3 * da
    w_in1 = one(w_in)
    wts = dict(
        g_mix=one(g_mix), g_ffn=one(g_ffn), b_f=one(b_f), s_pool=one(s_pool), b_router=one(b_router),
        w_cat=jnp.concatenate([w_in1[:, :n_main], w_in1[:, n_main + nh:]], axis=1).astype(BF16),
        w_fl_t=w_in1[:, n_main:n_main + nh].T.astype(BF16),
        w_pool=one(w_pool).astype(BF16),
        w_up_a=one(w_up_a).astype(BF16), w_up_b=one(w_up_b).astype(BF16), w_out=one(w_out).astype(BF16),
        w_router_t=one(w_router).T.astype(BF16),
        w_e_gate=one(w_e_gate), w_e_up=one(w_e_up), w_e_down=one(w_e_down),
        w_s_gate=one(w_s_gate).astype(BF16), w_s_up=one(w_s_up).astype(BF16), w_s_down=one(w_s_down).astype(BF16),
    )

    tiles_p = dict(tl=min(512, lp), tq=min(512, lp), t_route=min(512, bp * lp), w_outer=False,
                   tl_final=min(256, lp), nb_in=1)
    tiles_s = dict(tl=ls, t_route=bs * ls, w_outer=True, tl_final=ls, nb_in=bs)

    hist_p = jnp.zeros((bp, HALO, dp), F32)
    grp_p, caches_p = _group_mixer(x_prompt, mod3, MOD_ROWS_PROMPT, hist_p, 0, None, wts, tiles_p)
    hist_s = jnp.pad(one(cache_pool), ((0, 0), (HALO - POOL_HIST, 0), (0, 0)))
    past = (one(cache_k), one(cache_v), one(cache_logf))
    grp_s, caches_s = _group_mixer(x_sample, mod3, MOD_ROWS_SAMPLE, hist_s, cache_k.shape[2], past, wts, tiles_s)

    n_chunks = MOE_CHUNKS if bp % MOE_CHUNKS == 0 else 1
    nb_c = bp // n_chunks
    chunks = [[(grp_p, c * nb_c, nb_c)] for c in range(n_chunks)]
    chunks[-1].append((grp_s, 0, bs))
    outs = {id(grp_p): None, id(grp_s): None}
    for segments in chunks:
        for (grp, b0, nb), (y_tok, gate) in zip(segments, _moe_chunk(segments, wts)):
            outs[id(grp)] = _final(grp["x1"], grp["h2"], b0, y_tok, gate, mod3, grp["mod_row0"], wts["w_s_gate"],
                                   wts["w_s_up"], wts["w_s_down"], g_final, grp["tiles"]["tl_final"],
                                   y_prev=outs[id(grp)])
    stack = lambda a: a[None]
    return (outs[id(grp_p)], outs[id(grp_s)], *map(stack, caches_p), *map(stack, caches_s))
```

```python
import functools

import jax
import jax.numpy as jnp
from jax import lax
from jax.experimental import pallas as pl
from jax.experimental.pallas import tpu as pltpu
from jax.experimental.pallas import tpu_sc as plsc

F32 = jnp.float32
BF16 = jnp.bfloat16
I32 = jnp.int32
U32 = jnp.uint32

EPS = 1e-6
N_MOD = 6
POOL_WINDOWS = (2, 4, 8, 16)
POOL_HIST = max(POOL_WINDOWS) - 1
HALO = 16
HEAD_DIM = 128
TOP_K = 8
ROUTED_SCALE = 2.5
MOE_ROWS = 512
MOE_CHAIN_ROWS = 256
MOE_CHUNKS = 2
MOD_ROWS_PROMPT = 0
MOD_ROWS_SAMPLE = 8
VMEM_LIMIT_BYTES = 52 * 1024 * 1024
NEG_INF = float("-inf")
LOG2_E = 1.4426950408889634
LANES = 128
PACK_CHUNKS = 8
SC_CORES = 2
SC_SUBCORES = 16
SC_WORKERS = SC_CORES * SC_SUBCORES
SC_WINDOW = 32
NT_DIMS = (((1,), (1,)), ((), ()))


def _params(*semantics):
    return pltpu.CompilerParams(dimension_semantics=semantics, vmem_limit_bytes=VMEM_LIMIT_BYTES)


def _silu(x):
    return x * jax.nn.sigmoid(x)


def _store_packed(ref, x, row0=0):
    rows, width = x.shape
    half = width // 2
    bits = pltpu.bitcast(x.astype(BF16).astype(F32), U32)
    packed = (bits[:, :half] >> 16) | (bits[:, half:] & jnp.uint32(0xFFFF0000))
    for j in range(PACK_CHUNKS):
        ref[pl.ds(row0 * PACK_CHUNKS + j, rows, stride=PACK_CHUNKS), :] = packed[:, j * LANES:(j + 1) * LANES]


def _load_packed(ref, rows, valid=None, row0=0):
    lo, hi = [], []
    for j in range(PACK_CHUNKS):
        p = ref[pl.ds(row0 * PACK_CHUNKS + j, rows, stride=PACK_CHUNKS), :]
        if valid is not None:
            p = jnp.where(row0 + lax.broadcasted_iota(I32, p.shape, 0) < valid, p, jnp.uint32(0))
        lo.append(pltpu.bitcast(p << 16, F32))
        hi.append(pltpu.bitcast(p & jnp.uint32(0xFFFF0000), F32))
    return jnp.concatenate(lo + hi, axis=1)


def _ada_kernel(c_ref, w_ref, b_ref, o_ref):
    a = _silu(c_ref[...]).astype(BF16)
    o_ref[...] = jnp.dot(a, w_ref[...].astype(BF16), preferred_element_type=F32) + b_ref[...]


def _ada_mod(c_all, w_ada, b_ada):
    rows, d = c_all.shape
    n = w_ada.shape[1]
    tn = 1024
    return pl.pallas_call(
        _ada_kernel,
        grid=(n // tn,),
        in_specs=[
            pl.BlockSpec((rows, d), lambda j: (0, 0)),
            pl.BlockSpec((d, tn), lambda j: (0, j)),
            pl.BlockSpec((1, tn), lambda j: (0, j)),
        ],
        out_specs=pl.BlockSpec((rows, tn), lambda j: (0, j)),
        out_shape=jax.ShapeDtypeStruct((rows, n), F32),
        compiler_params=_params("parallel"),
        name="ada_mod",
    )(c_all, w_ada, b_ada.reshape(1, n))


def _modulated_norm(x, g, m, shift_row, scale_row):
    ms = jnp.mean(x * x, axis=-1, keepdims=True)
    return x * lax.rsqrt(ms + EPS) * g * (1.0 + m[scale_row:scale_row + 1, :]) + m[shift_row:shift_row + 1, :]


def _in_proj_kernel(x_ref, g_ref, mod_ref, wfl_ref, bf_ref, w_ref,
                    u_ref, q_ref, k32_ref, kb_ref, v32_ref, vb_ref, gates_ref, lf_ref, h_ref, *, q_scale):
    j = pl.program_id(2)
    nb, tl, _ = x_ref.shape
    nh = wfl_ref.shape[0]

    @pl.when(j == 0)
    def _():
        for bi in range(nb):
            hb = _modulated_norm(x_ref[bi], g_ref[...], mod_ref[bi], 0, 1).astype(BF16)
            h_ref[bi * tl:(bi + 1) * tl, :] = hb
            fl = lax.dot_general(wfl_ref[...], hb, NT_DIMS, preferred_element_type=F32) + bf_ref[...]
            lf_ref[bi] = jnp.minimum(fl, 0.0) - jnp.log(1.0 + jnp.exp(-jnp.abs(fl)))

    def tile_rows():
        acc = jnp.dot(h_ref[...], w_ref[...], preferred_element_type=F32)
        return [acc[bi * tl:(bi + 1) * tl, :] for bi in range(nb)]

    def store_heads(f32_ref, bf_ref_):
        for bi, r in enumerate(tile_rows()):
            bf_ref_[bi] = r.astype(BF16)
            for hd in range(nh):
                f32_ref[bi, pl.ds(hd, tl, stride=nh), :] = r[:, hd * HEAD_DIM:(hd + 1) * HEAD_DIM]

    @pl.when(j == 0)
    def _():
        for bi, r in enumerate(tile_rows()):
            u_ref[bi] = r

    @pl.when(j == 1)
    def _():
        for bi, r in enumerate(tile_rows()):
            q_ref[bi] = (r * q_scale).astype(BF16)

    @pl.when(j == 2)
    def _():
        store_heads(k32_ref, kb_ref)

    @pl.when(j == 3)
    def _():
        store_heads(v32_ref, vb_ref)

    @pl.when(j >= 4)
    def _():
        for bi, r in enumerate(tile_rows()):
            gates_ref[bi] = jax.nn.sigmoid(r).astype(BF16)


def _in_proj(x3, g, mod3, mod_row0, w_fl_t, b_f, w_cat, q_scale, nb, tl):
    b, l, d = x3.shape
    nh = w_fl_t.shape[0]
    tn = nh * HEAD_DIM
    n_tiles = w_cat.shape[1] // tn
    n_gate_tiles = n_tiles - 4
    assert mod_row0 % nb == 0 and b % nb == 0
    tile = lambda width: pl.BlockSpec((nb, tl, width), lambda bb, li, j: (bb, li, 0))
    heads = pl.BlockSpec((nb, tl * nh, HEAD_DIM), lambda bb, li, j: (bb, li, 0))
    return pl.pallas_call(
        functools.partial(_in_proj_kernel, q_scale=q_scale),
        grid=(b // nb, l // tl, n_tiles),
        in_specs=[
            tile(d),
            pl.BlockSpec((1, d), lambda bb, li, j: (0, 0)),
            pl.BlockSpec((nb, N_MOD, d), lambda bb, li, j: (mod_row0 // nb + bb, 0, 0)),
            pl.BlockSpec((nh, d), lambda bb, li, j: (0, 0)),
            pl.BlockSpec((nh, 1), lambda bb, li, j: (0, 0)),
            pl.BlockSpec((d, tn), lambda bb, li, j: (0, j)),
        ],
        out_specs=[
            tile(tn), tile(tn), heads, tile(tn), heads, tile(tn),
            pl.BlockSpec((nb, tl, tn), lambda bb, li, j: (bb, li, jnp.maximum(j - 4, 0))),
            pl.BlockSpec((nb, nh, tl), lambda bb, li, j: (bb, 0, li)),
        ],
        out_shape=[
            jax.ShapeDtypeStruct((b, l, tn), F32),
            jax.ShapeDtypeStruct((b, l, tn), BF16),
            jax.ShapeDtypeStruct((b, l * nh, HEAD_DIM), F32),
            jax.ShapeDtypeStruct((b, l, tn), BF16),
            jax.ShapeDtypeStruct((b, l * nh, HEAD_DIM), F32),
            jax.ShapeDtypeStruct((b, l, tn), BF16),
            jax.ShapeDtypeStruct((b, l, n_gate_tiles * tn), BF16),
            jax.ShapeDtypeStruct((b, nh, l), F32),
        ],
        scratch_shapes=[pltpu.VMEM((nb * tl, d), BF16)],
        compiler_params=_params("parallel", "parallel", "arbitrary"),
        name="in_proj",
    )(x3, g.reshape(1, d), mod3, w_fl_t, b_f.reshape(nh, 1), w_cat)


def _mixer_out_kernel(ya_ref, yb_ref, ga_ref, gb_ref, x_ref, mod_ref, g_ref, wa_ref, wb_ref, wo_ref,
                      x1_ref, h2_ref, h2p_ref):
    merged = ga_ref[...].astype(F32) * jnp.dot(ya_ref[...], wa_ref[...], preferred_element_type=F32)
    merged = merged + gb_ref[...].astype(F32) * jnp.dot(yb_ref[...], wb_ref[...], preferred_element_type=F32)
    m = mod_ref[...]
    x1 = x_ref[...] + m[2:3, :] * jnp.dot(merged.astype(BF16), wo_ref[...], preferred_element_type=F32)
    x1_ref[...] = x1
    h2 = _modulated_norm(x1, g_ref[...], m, 3, 4)
    h2_ref[...] = h2.astype(BF16)
    _store_packed(h2p_ref, h2)


def _mixer_out(y_a, y_b, gates, x3, mod3, mod_row0, g_ffn, w_up_a, w_up_b, w_out, tl):
    b, l, d = x3.shape
    n_l = l // tl
    dp, da = y_a.shape[2], y_b.shape[2]
    tile = lambda width, col=0: pl.BlockSpec((None, tl, width), lambda bi, li: (bi, li, col))
    resident = lambda shape: pl.BlockSpec(shape, lambda bi, li: (0,) * len(shape), pipeline_mode=pl.Buffered(1))
    return pl.pallas_call(
        _mixer_out_kernel,
        grid=(b, n_l),
        in_specs=[
            tile(dp), tile(da), tile(d, 0), tile(d, 1), tile(d),
            pl.BlockSpec((None, N_MOD, d), lambda bi, li: (mod_row0 + bi, 0, 0)),
            pl.BlockSpec((1, d), lambda bi, li: (0, 0)),
            resident((dp, d)), resident((da, d)), resident((d, d)),
        ],
        out_specs=[
            tile(d), tile(d),
            pl.BlockSpec((tl * PACK_CHUNKS, LANES), lambda bi, li: (bi * n_l + li, 0)),
        ],
        out_shape=[
            jax.ShapeDtypeStruct((b, l, d), F32),
            jax.ShapeDtypeStruct((b, l, d), BF16),
            jax.ShapeDtypeStruct((b * l * PACK_CHUNKS, LANES), U32),
        ],
        compiler_params=_params("parallel", "parallel"),
        name="mixer_out",
    )(y_a, y_b, gates, gates, x3, mod3, g_ffn.reshape(1, d), w_up_a, w_up_b, w_out)


def _pool_kernel(u_ref, halo_ref, hist_ref, wp_ref, sp_ref, o_ref, full_ref, *, tl, start, gw):
    i = pl.program_id(1)
    full_ref[0:HALO, :] = jnp.where(i == 0, hist_ref[...], halo_ref[...])
    full_ref[HALO:HALO + tl, :] = u_ref[...]
    pos = start + i * tl + lax.broadcasted_iota(I32, (tl, 1), 0)
    for g, w in enumerate(POOL_WINDOWS):
        c0, c1 = g * gw, (g + 1) * gw
        cur = full_ref[HALO:HALO + tl, c0:c1]
        win = cur
        for d in range(1, w):
            win = win + full_ref[HALO - d:HALO - d + tl, c0:c1]
        cnt = jnp.minimum(pos + 1, w).astype(F32)
        dmean = win / cnt - cur
        y = jnp.dot(dmean.astype(BF16), wp_ref[g], preferred_element_type=F32) * sp_ref[:, c0:c1]
        o_ref[:, c0:c1] = y.astype(o_ref.dtype)


def _pool_mixer(u3, hist, w_pool, s_pool, start, tl):
    b, l, dp = u3.shape
    gw = dp // len(POOL_WINDOWS)
    halo_blocks = tl // HALO
    return pl.pallas_call(
        functools.partial(_pool_kernel, tl=tl, start=start, gw=gw),
        grid=(b, l // tl),
        in_specs=[
            pl.BlockSpec((None, tl, dp), lambda bi, li: (bi, li, 0)),
            pl.BlockSpec((None, HALO, dp), lambda bi, li: (bi, jnp.maximum(li * halo_blocks - 1, 0), 0)),
            pl.BlockSpec((None, HALO, dp), lambda bi, li: (bi, 0, 0)),
            pl.BlockSpec((len(POOL_WINDOWS), gw, gw), lambda bi, li: (0, 0, 0)),
            pl.BlockSpec((1, dp), lambda bi, li: (0, 0)),
        ],
        out_specs=pl.BlockSpec((None, tl, dp), lambda bi, li: (bi, li, 0)),
        out_shape=jax.ShapeDtypeStruct((b, l, dp), BF16),
        scratch_shapes=[pltpu.VMEM((HALO + tl, dp), F32)],
        compiler_params=_params("parallel", "parallel"),
        name="pool_mixer",
    )(u3, u3, hist, w_pool, s_pool.reshape(1, dp))


def _cumsum_kernel(x_ref, o_ref):
    x = x_ref[...]
    n = x.shape[1]
    idx = lax.broadcasted_iota(I32, x.shape, 1)
    s = 1
    while s < n:
        x = x + jnp.where(idx >= s, pltpu.roll(x, s, 1), 0.0)
        s *= 2
    o_ref[...] = x


def _cumsum_last(x3):
    b, h, n = x3.shape
    return pl.pallas_call(
        _cumsum_kernel,
        grid=(b,),
        in_specs=[pl.BlockSpec((None, h, n), lambda bi: (bi, 0, 0))],
        out_specs=pl.BlockSpec((None, h, n), lambda bi: (bi, 0, 0)),
        out_shape=jax.ShapeDtypeStruct((b, h, n), F32),
        compiler_params=_params("parallel"),
        name="cumsum_logf",
    )(x3)


def _attn_prompt_kernel(q_ref, k_ref, v_ref, fk_ref, o_ref, s0_ref, s1_ref, m_ref, l_ref, acc_ref, *, tq):
    qi = pl.program_id(2)
    q = q_ref[...]
    n_chunks = tq // LANES

    def produce(j, s_ref):
        off = pl.multiple_of(j * tq, tq)
        s = lax.dot_general(q, k_ref[pl.ds(off, tq), :], NT_DIMS, preferred_element_type=F32)
        s_ref[...] = s - fk_ref[j] * LOG2_E

    def consume(j, s_ref, diagonal):
        s = s_ref[...]
        if diagonal:
            row = lax.broadcasted_iota(I32, (tq, tq), 0)
            col = lax.broadcasted_iota(I32, (tq, tq), 1)
            s = jnp.where(col <= row, s, NEG_INF)
        chunks = [s[:, c * LANES:(c + 1) * LANES] for c in range(n_chunks)]
        m_old = m_ref[...]
        m_new = jnp.maximum(m_old, jnp.max(functools.reduce(jnp.maximum, chunks), axis=1, keepdims=True))
        alpha = jnp.exp2(m_old - m_new)
        p_chunks = [jnp.exp2(ch - m_new) for ch in chunks]
        l_ref[...] = alpha * l_ref[...] + functools.reduce(lambda a, c: a + c, p_chunks)
        p = jnp.concatenate(p_chunks, axis=1).astype(BF16)
        v = v_ref[pl.ds(pl.multiple_of(j * tq, tq), tq), :]
        acc_ref[...] = alpha * acc_ref[...] + jnp.dot(p, v, preferred_element_type=F32)
        m_ref[...] = m_new

    m_ref[...] = jnp.full(m_ref.shape, NEG_INF, F32)
    l_ref[...] = jnp.zeros(l_ref.shape, F32)
    acc_ref[...] = jnp.zeros(acc_ref.shape, F32)
    produce(0, s0_ref)

    def pair(jj, carry):
        j0 = 2 * jj
        produce(j0 + 1, s1_ref)
        consume(j0, s0_ref, False)
        produce(j0 + 2, s0_ref)
        consume(j0 + 1, s1_ref, False)
        return carry

    lax.fori_loop(0, qi // 2, pair, 0)

    @pl.when(qi % 2 == 1)
    def _():
        produce(qi, s1_ref)
        consume(qi - 1, s0_ref, False)
        consume(qi, s1_ref, True)

    @pl.when(qi % 2 == 0)
    def _():
        consume(qi, s0_ref, True)

    l = jnp.sum(l_ref[...], axis=1, keepdims=True)
    o_ref[...] = (acc_ref[...] / l).astype(o_ref.dtype)


def _attn_prompt(q3, k3, v3, f3, tq):
    b, l, da = q3.shape
    nh = da // HEAD_DIM
    nk = l // tq
    fk = f3.reshape(b * nh, nk, 1, tq)
    return pl.pallas_call(
        functools.partial(_attn_prompt_kernel, tq=tq),
        grid=(b, nh, nk),
        in_specs=[
            pl.BlockSpec((None, tq, HEAD_DIM), lambda bi, hi, qi: (bi, qi, hi)),
            pl.BlockSpec((None, l, HEAD_DIM), lambda bi, hi, qi: (bi, 0, hi)),
            pl.BlockSpec((None, l, HEAD_DIM), lambda bi, hi, qi: (bi, 0, hi)),
            pl.BlockSpec((None, nk, 1, tq), lambda bi, hi, qi: (bi * nh + hi, 0, 0, 0)),
        ],
        out_specs=pl.BlockSpec((None, tq, HEAD_DIM), lambda bi, hi, qi: (bi, qi, hi)),
        out_shape=jax.ShapeDtypeStruct((b, l, da), BF16),
        scratch_shapes=[
            pltpu.VMEM((tq, tq), F32),
            pltpu.VMEM((tq, tq), F32),
            pltpu.VMEM((tq, LANES), F32),
            pltpu.VMEM((tq, LANES), F32),
            pltpu.VMEM((tq, HEAD_DIM), F32),
        ],
        compiler_params=_params("parallel", "parallel", "parallel"),
        name="fox_prompt",
    )(q3, k3, v3, fk)


def _attn_sample_kernel(q_ref, kp_ref, vp_ref, kn_ref, vn_ref, fp_ref, fn_ref, o_ref, m_ref, l_ref, acc_ref, *,
                        nh, pc, lq):
    c = pl.program_id(1)

    @pl.when(c == 0)
    def _():
        m_ref[...] = jnp.full(m_ref.shape, NEG_INF, F32)
        l_ref[...] = jnp.zeros(l_ref.shape, F32)
        acc_ref[...] = jnp.zeros(acc_ref.shape, F32)

    for h in range(nh):
        cols = slice(h * HEAD_DIM, (h + 1) * HEAD_DIM)
        q = q_ref[:, cols]
        k = kp_ref[pl.ds(h, pc, stride=nh), :].astype(BF16)
        v = vp_ref[pl.ds(h, pc, stride=nh), :].astype(BF16)
        s = lax.dot_general(q, k, NT_DIMS, preferred_element_type=F32) - fp_ref[h:h + 1, :]
        chunks = [s[:, j * LANES:(j + 1) * LANES] for j in range(pc // LANES)]
        m_old = m_ref[h]
        m_new = jnp.maximum(m_old, jnp.max(functools.reduce(jnp.maximum, chunks), axis=1, keepdims=True))
        alpha = jnp.exp(m_old - m_new)
        p_chunks = [jnp.exp(ch - m_new) for ch in chunks]
        l_ref[h] = alpha * l_ref[h] + functools.reduce(lambda a, b: a + b, p_chunks)
        p = jnp.concatenate(p_chunks, axis=1).astype(BF16)
        acc_ref[h] = alpha * acc_ref[h] + jnp.dot(p, v, preferred_element_type=F32)
        m_ref[h] = m_new

    @pl.when(c == pl.num_programs(1) - 1)
    def _():
        row = lax.broadcasted_iota(I32, (lq, lq), 0)
        col = lax.broadcasted_iota(I32, (lq, lq), 1)
        for h in range(nh):
            cols = slice(h * HEAD_DIM, (h + 1) * HEAD_DIM)
            s = lax.dot_general(q_ref[:, cols], kn_ref[:, cols], NT_DIMS, preferred_element_type=F32)
            s = jnp.where(col <= row, s - fn_ref[h:h + 1, :lq], NEG_INF)
            m_old = m_ref[h]
            m_new = jnp.maximum(m_old, jnp.max(s, axis=1, keepdims=True))
            alpha = jnp.exp(m_old - m_new)
            p = jnp.exp(s - m_new[:, :lq])
            l = jnp.sum(alpha * l_ref[h], axis=1, keepdims=True) + jnp.sum(p, axis=1, keepdims=True)
            acc = alpha * acc_ref[h] + jnp.dot(p.astype(BF16), vn_ref[:, cols], preferred_element_type=F32)
            o_ref[:, cols] = (acc / l).astype(o_ref.dtype)


def _attn_sample(q3, kn3, vn3, cache_k, cache_v, f_past, f_new, pc):
    b, lq, da = q3.shape
    _, past, nh, dh = cache_k.shape
    nc = past // pc
    new_spec = pl.BlockSpec((None, lq, da), lambda bi, ci: (bi, 0, 0))
    past_spec = pl.BlockSpec((None, pc * nh, dh), lambda bi, ci: (bi, ci, 0))
    return pl.pallas_call(
        functools.partial(_attn_sample_kernel, nh=nh, pc=pc, lq=lq),
        grid=(b, nc),
        in_specs=[new_spec, past_spec, past_spec, new_spec, new_spec,
                  pl.BlockSpec((None, None, nh, pc), lambda bi, ci: (bi, ci, 0, 0)),
                  pl.BlockSpec((None, nh, LANES), lambda bi, ci: (bi, 0, 0))],
        out_specs=new_spec,
        out_shape=jax.ShapeDtypeStruct((b, lq, da), BF16),
        scratch_shapes=[pltpu.VMEM((nh, lq, LANES), F32), pltpu.VMEM((nh, lq, LANES), F32),
                        pltpu.VMEM((nh, lq, HEAD_DIM), F32)],
        compiler_params=_params("parallel", "arbitrary"),
        name="fox_sample",
    )(q3, cache_k.reshape(b, past * nh, dh), cache_v.reshape(b, past * nh, dh), kn3, vn3, f_past, f_new)


def _router_kernel(h_ref, wr_ref, br_ref, tri_ref, cnt0_ref, idx_ref, gate_ref, rank_ref, cnt_ref, carry_ref):
    @pl.when(pl.program_id(0) == 0)
    def _():
        carry_ref[...] = cnt0_ref[...]

    n_exp, tm = wr_ref.shape[0], h_ref.shape[0]
    scores = jax.nn.sigmoid(lax.dot_general(wr_ref[...], h_ref[...], NT_DIMS, preferred_element_type=F32))
    work = scores + br_ref[...]
    eidx = lax.broadcasted_iota(I32, (n_exp, tm), 0)
    picked = jnp.zeros((n_exp, tm), F32)
    onehots, gates = [], []
    for k in range(TOP_K):
        best = jnp.max(work, axis=0, keepdims=True)
        first = jnp.min(jnp.where(work == best, eidx, n_exp), axis=0, keepdims=True)
        onehot = eidx == first
        idx_ref[k:k + 1, :] = first
        gates.append(jnp.sum(jnp.where(onehot, scores, 0.0), axis=0, keepdims=True))
        onehots.append(onehot)
        picked = picked + jnp.where(onehot, 1.0, 0.0)
        work = jnp.where(onehot, NEG_INF, work)
    norm = ROUTED_SCALE / functools.reduce(lambda a, c: a + c, gates)
    cum = jnp.dot(picked.astype(BF16), tri_ref[...], preferred_element_type=F32)
    before = cum - picked + carry_ref[:, 0:1]
    for k in range(TOP_K):
        gate_ref[k:k + 1, :] = gates[k] * norm
        rank_ref[k:k + 1, :] = jnp.sum(jnp.where(onehots[k], before, 0.0), axis=0, keepdims=True).astype(I32)
    carry_ref[...] = carry_ref[...] + cum[:, tm - 1:tm]
    cnt_ref[...] = carry_ref[...]


def _router(h2, tok0, ntok, w_router_t, b_router, cnt0, tm):
    d = h2.shape[1]
    n_exp = w_router_t.shape[0]
    tri = jnp.triu(jnp.ones((tm, tm), BF16))
    tok_spec = pl.BlockSpec((TOP_K, tm), lambda i: (0, i))
    cnt_spec = pl.BlockSpec((n_exp, LANES), lambda i: (0, 0))
    return pl.pallas_call(
        _router_kernel,
        grid=(ntok // tm,),
        in_specs=[
            pl.BlockSpec((tm, d), lambda i: (tok0 // tm + i, 0)),
            pl.BlockSpec((n_exp, d), lambda i: (0, 0)),
            pl.BlockSpec((n_exp, 1), lambda i: (0, 0)),
            pl.BlockSpec((tm, tm), lambda i: (0, 0)),
            cnt_spec,
        ],
        out_specs=[tok_spec, tok_spec, tok_spec, cnt_spec],
        out_shape=[
            jax.ShapeDtypeStruct((TOP_K, ntok), I32),
            jax.ShapeDtypeStruct((TOP_K, ntok), F32),
            jax.ShapeDtypeStruct((TOP_K, ntok), I32),
            jax.ShapeDtypeStruct((n_exp, LANES), F32),
        ],
        scratch_shapes=[pltpu.VMEM((n_exp, LANES), F32)],
        compiler_params=_params("arbitrary"),
        name="moe_router",
    )(h2, w_router_t, b_router.reshape(n_exp, 1), tri, cnt0)


def _moe_kernel(be_ref, bv_ref, nu_ref, x_ref, wg_ref, wu_ref, wd_ref, y_ref, wg_bf, wu_bf, wd_bf):
    i = pl.program_id(0)
    active = i < nu_ref[0]

    @pl.when(active & ((i == 0) | (be_ref[i] != be_ref[jnp.maximum(i - 1, 0)])))
    def _():
        wg_bf[...] = wg_ref[...].astype(BF16)
        wu_bf[...] = wu_ref[...].astype(BF16)
        wd_bf[...] = wd_ref[...].astype(BF16)

    @pl.when(active)
    def _():
        for row0 in range(0, MOE_ROWS, MOE_CHAIN_ROWS):
            x = _load_packed(x_ref, MOE_CHAIN_ROWS, valid=bv_ref[i], row0=row0).astype(BF16)
            g = jnp.dot(x, wg_bf[...], preferred_element_type=F32)
            u = jnp.dot(x, wu_bf[...], preferred_element_type=F32)
            a = (_silu(g) * u).astype(BF16)
            _store_packed(y_ref, jnp.dot(a, wd_bf[...], preferred_element_type=F32), row0=row0)


def _moe_experts(x_sorted, blk_e, blk_valid, n_used, w_e_gate, w_e_up, w_e_down):
    cap = x_sorted.shape[0] // PACK_CHUNKS
    d, de = w_e_gate.shape[1:]
    n_blocks = cap // MOE_ROWS
    row_map = lambda i, be, bv, nu: (jnp.minimum(i, nu[0] - 1), 0)
    w_map = lambda i, be, bv, nu: (be[i], 0, 0)
    return pl.pallas_call(
        _moe_kernel,
        grid_spec=pltpu.PrefetchScalarGridSpec(
            num_scalar_prefetch=3,
            grid=(n_blocks,),
            in_specs=[
                pl.BlockSpec((MOE_ROWS * PACK_CHUNKS, LANES), row_map),
                pl.BlockSpec((None, d, de), w_map),
                pl.BlockSpec((None, d, de), w_map),
                pl.BlockSpec((None, de, d), w_map),
            ],
            out_specs=pl.BlockSpec((MOE_ROWS * PACK_CHUNKS, LANES), row_map),
            scratch_shapes=[pltpu.VMEM((d, de), BF16), pltpu.VMEM((d, de), BF16), pltpu.VMEM((de, d), BF16)],
        ),
        out_shape=jax.ShapeDtypeStruct(x_sorted.shape, U32),
        compiler_params=_params("arbitrary"),
        name="moe_experts",
    )(blk_e, blk_valid, n_used, x_sorted, w_e_gate, w_e_up, w_e_down)


def _sc_mesh():
    return plsc.VectorSubcoreMesh(core_axis_name="c", subcore_axis_name="s")


def _sc_worker_id():
    return lax.axis_index("s") * SC_CORES + lax.axis_index("c")


def _sc_scatter_rows(segments, n_out):
    row_shape = segments[0][0].shape[1:]
    dtype = segments[0][0].dtype
    n_seg = len(segments)
    geom = [(tok0, dw.shape[0] // SC_WORKERS, dw.shape[2]) for _, tok0, dw in segments]
    scratch = []
    for _, n_win, win in geom:
        scratch += [pltpu.VMEM((n_win, TOP_K, win), I32), pltpu.VMEM((win,) + row_shape, dtype)]

    @functools.partial(
        pl.kernel, mesh=_sc_mesh(),
        out_type=jax.ShapeDtypeStruct((n_out,) + row_shape, dtype),
        scratch_types=scratch + [pltpu.SemaphoreType.DMA],
    )
    def scatter_kernel(*refs):
        out_hbm = refs[2 * n_seg]
        sem = refs[-1]
        for s, (tok0, n_win, win) in enumerate(geom):
            rows_hbm, idx_hbm = refs[2 * s], refs[2 * s + 1]
            idx_v, rows_v = refs[2 * n_seg + 1 + 2 * s], refs[2 * n_seg + 2 + 2 * s]
            first = _sc_worker_id() * n_win
            pltpu.sync_copy(idx_hbm.at[pl.ds(first, n_win)], idx_v)

            @pl.loop(0, n_win)
            def _(w):
                pltpu.sync_copy(rows_hbm.at[pl.ds(tok0 + (first + w) * win, win)], rows_v)
                copies = [pltpu.make_async_copy(rows_v, out_hbm.at[idx_v.at[w, k]], sem) for k in range(TOP_K)]
                for c in copies:
                    c.start()
                for c in copies:
                    c.wait()

    args = []
    for rows, _, dest_win in segments:
        args += [rows, dest_win]
    return scatter_kernel(*args)


def _sc_gather_rows(table, idx_wins):
    row_shape = table.shape[1:]
    n_seg = len(idx_wins)
    geom = [(iw.shape[0] // SC_WORKERS, iw.shape[1]) for iw in idx_wins]
    scratch = []
    for n_win, win in geom:
        scratch += [pltpu.VMEM((n_win, win), I32), pltpu.VMEM((win,) + row_shape, table.dtype)]

    @functools.partial(
        pl.kernel, mesh=_sc_mesh(),
        out_type=[jax.ShapeDtypeStruct((iw.shape[0] * iw.shape[1],) + row_shape, table.dtype) for iw in idx_wins],
        scratch_types=scratch + [pltpu.SemaphoreType.DMA],
    )
    def gather_kernel(*refs):
        table_hbm = refs[0]
        sem = refs[-1]
        for s, (n_win, win) in enumerate(geom):
            idx_hbm, out_hbm = refs[1 + s], refs[1 + n_seg + s]
            idx_v, rows_v = refs[1 + 2 * n_seg + 2 * s], refs[2 + 2 * n_seg + 2 * s]
            first = _sc_worker_id() * n_win
            pltpu.sync_copy(idx_hbm.at[pl.ds(first, n_win)], idx_v)

            @pl.loop(0, n_win)
            def _(w):
                pltpu.async_copy(table_hbm.at[idx_v.at[w]], rows_v, sem).wait()
                pltpu.sync_copy(rows_v, out_hbm.at[pl.ds((first + w) * win, win)])

    return gather_kernel(table, *idx_wins)


def _final_kernel(x1_ref, h2_ref, yt_ref, gate_ref, mod_ref, wsg_ref, wsu_ref, wsd_ref, gf_ref, *rest):
    o_ref = rest[-1]
    h2 = h2_ref[...]
    g = jnp.dot(h2, wsg_ref[...], preferred_element_type=F32)
    u = jnp.dot(h2, wsu_ref[...], preferred_element_type=F32)
    f = jnp.dot((_silu(g) * u).astype(BF16), wsd_ref[...], preferred_element_type=F32)
    gate = gate_ref[...]
    tl = h2.shape[0]
    for k in range(TOP_K):
        f = f + gate[:, k:k + 1] * _load_packed(yt_ref.at[k], tl)
    x2 = x1_ref[...] + mod_ref[N_MOD - 1:N_MOD, :] * f
    ms = jnp.mean(x2 * x2, axis=-1, keepdims=True)
    o_ref[...] = x2 * lax.rsqrt(ms + EPS) * gf_ref[...]


def _final(x1, h2, b0, y_tok, gate, mod3, mod_row0, w_s_gate, w_s_up, w_s_down, g_final, tl, y_prev=None):
    b, l, d = x1.shape
    nb = gate.shape[0]
    ds = w_s_gate.shape[1]
    tile = pl.BlockSpec((None, tl, d), lambda bi, li: (b0 + bi, li, 0))
    const = lambda shape: pl.BlockSpec(shape, lambda bi, li: (0,) * len(shape), pipeline_mode=pl.Buffered(1))
    in_specs = [
        tile,
        tile,
        pl.BlockSpec((TOP_K, None, tl * PACK_CHUNKS, LANES), lambda bi, li: (0, bi, li, 0)),
        pl.BlockSpec((None, tl, TOP_K), lambda bi, li: (bi, li, 0)),
        pl.BlockSpec((None, N_MOD, d), lambda bi, li: (mod_row0 + b0 + bi, 0, 0)),
        const((d, ds)),
        const((d, ds)),
        const((ds, d)),
        const((1, d)),
    ]
    args = [x1, h2, y_tok, gate, mod3, w_s_gate, w_s_up, w_s_down, g_final.reshape(1, d)]
    aliases = {}
    if y_prev is not None:
        in_specs.append(pl.BlockSpec(memory_space=pl.ANY))
        aliases = {len(args): 0}
        args.append(y_prev)
    return pl.pallas_call(
        _final_kernel,
        grid=(nb, l // tl),
        in_specs=in_specs,
        out_specs=tile,
        out_shape=jax.ShapeDtypeStruct((b, l, d), F32),
        input_output_aliases=aliases,
        compiler_params=_params("parallel", "parallel"),
        name="combine_final",
    )(*args)


def _group_mixer(x3, mod3, mod_row0, hist, start, past, wts, tiles):
    b, l, d = x3.shape
    tl = tiles["tl"]
    dp = wts["w_pool"].shape[0] * wts["w_pool"].shape[1]
    da = wts["w_up_b"].shape[0]
    nh = da // HEAD_DIM
    t = b * l

    assert dp == da, "u, q, k, v must each be one column tile of the input projection"
    scale = HEAD_DIM ** -0.5 * (LOG2_E if past is None else 1.0)
    u, qb, k, kb, v, vb, gates, logf_t = _in_proj(x3, wts["g_mix"], mod3, mod_row0, wts["w_fl_t"], wts["b_f"],
                                                  wts["w_cat"], scale, tiles["nb_in"], tl)

    y_a = _pool_mixer(u, hist, wts["w_pool"], wts["s_pool"], start, tl)

    if past is None:
        y_b = _attn_prompt(qb, kb, vb, _cumsum_last(logf_t), tiles["tq"])
    else:
        cache_k, cache_v, cache_logf = past
        p = cache_k.shape[1]
        pc = tiles["past_chunk"]
        assert l <= LANES and p % pc == 0 and p % LANES == 0
        lf_all = jnp.concatenate([jnp.swapaxes(cache_logf.astype(F32), 1, 2), logf_t,
                                  jnp.zeros((b, nh, LANES - l), F32)], axis=2)
        f_all = _cumsum_last(lf_all)
        f_past = f_all[:, :, :p].reshape(b, nh, p // pc, pc).transpose(0, 2, 1, 3)
        y_b = _attn_sample(qb, kb, vb, cache_k, cache_v, f_past, f_all[:, :, p:], pc)

    x1, h2, h2_packed = _mixer_out(y_a, y_b, gates, x3, mod3, mod_row0, wts["g_ffn"], wts["w_up_a"], wts["w_up_b"],
                                   wts["w_out"], tiles["tl_out"])
    caches = (u[:, l - POOL_HIST:, :], k.reshape(b, l, nh, HEAD_DIM), v.reshape(b, l, nh, HEAD_DIM),
              jnp.swapaxes(logf_t, 1, 2))
    return dict(x1=x1, h2=h2, h2_packed=h2_packed, mod_row0=mod_row0, tiles=tiles), caches


def _moe_chunk(segments, wts):
    n_exp = wts["w_router_t"].shape[0]
    row_shape = (PACK_CHUNKS, LANES)
    counts_f = jnp.zeros((n_exp, LANES), F32)
    routed = []
    for grp, b0, nb in segments:
        b, l, d = grp["h2"].shape
        ntok = nb * l
        idx_t, gate_t, rank_t, counts_f = _router(grp["h2"].reshape(b * l, d), b0 * l, ntok, wts["w_router_t"],
                                                  wts["b_router"], counts_f, min(grp["tiles"]["t_route"], ntok))
        routed.append((idx_t, gate_t, rank_t, ntok))
    counts = counts_f[:, 0].astype(I32)

    n_slots = sum(r[3] for r in routed) * TOP_K
    n_blocks = -(-n_slots // MOE_ROWS) + n_exp
    cap = n_blocks * MOE_ROWS
    padded = (counts + MOE_ROWS - 1) // MOE_ROWS * MOE_ROWS
    pad_end = jnp.cumsum(padded)
    pad_start = pad_end - padded
    blk_row0 = jnp.arange(n_blocks, dtype=I32) * MOE_ROWS
    blk_e = jnp.minimum(jnp.sum(pad_end[None, :] <= blk_row0[:, None], axis=1), n_exp - 1).astype(I32)
    blk_valid = jnp.clip(pad_start[blk_e] + counts[blk_e] - blk_row0, 0, MOE_ROWS).astype(I32)
    n_used = (pad_end[-1:] // MOE_ROWS).astype(I32)

    scatter_segs, gather_idx = [], []
    for (grp, b0, nb), (idx_t, _, rank_t, ntok) in zip(segments, routed):
        b, l, _ = grp["h2"].shape
        dest = rank_t + jnp.sum(jnp.where(idx_t[:, :, None] == jnp.arange(n_exp, dtype=I32), pad_start, 0), axis=2)
        win_d = min(SC_WINDOW, ntok // SC_WORKERS)
        dest_win = dest.reshape(TOP_K, ntok // win_d, win_d).transpose(1, 0, 2)
        scatter_segs.append((grp["h2_packed"].reshape((b * l,) + row_shape), b0 * l, dest_win))
        win_c = min(SC_WINDOW, ntok * TOP_K // SC_WORKERS)
        gather_idx.append(dest.reshape(ntok * TOP_K // win_c, win_c))

    x_sorted = _sc_scatter_rows(scatter_segs, cap)
    y_sorted = _moe_experts(x_sorted.reshape(cap * PACK_CHUNKS, LANES), blk_e, blk_valid, n_used,
                            wts["w_e_gate"], wts["w_e_up"], wts["w_e_down"])
    y_toks = _sc_gather_rows(y_sorted.reshape((cap,) + row_shape), gather_idx)
    out = []
    for (grp, b0, nb), (_, gate_t, _, _), y_tok in zip(segments, routed, y_toks):
        l = grp["h2"].shape[1]
        out.append((y_tok.reshape(TOP_K, nb, l * PACK_CHUNKS, LANES), gate_t.T.reshape(nb, l, TOP_K)))
    return out


def kernel(x_prompt, x_sample, cache_pool, cache_k, cache_v, cache_logf, c_prompt, c_sample, w_ada, b_ada, g_mix, w_in, b_f, w_pool, s_pool, w_up_a, w_up_b, w_out, g_ffn, w_router, b_router, w_e_gate, w_e_up, w_e_down, w_s_gate, w_s_up, w_s_down, g_final):
    depth = w_ada.shape[0]
    assert depth == 1, "a single layer is supported"
    bp, lp, d = x_prompt.shape
    bs, ls, _ = x_sample.shape
    dp = w_pool.shape[1] * w_pool.shape[2]
    da = w_up_b.shape[1]
    nh = da // HEAD_DIM
    assert bp <= MOD_ROWS_SAMPLE

    one = lambda a: a.reshape(a.shape[1:])
    c_all = jnp.zeros((MOD_ROWS_SAMPLE + bs, d), F32).at[:bp].set(c_prompt).at[MOD_ROWS_SAMPLE:].set(c_sample)
    mod3 = _ada_mod(c_all, one(w_ada), one(b_ada)).reshape(c_all.shape[0], N_MOD, d)

    n_main = dp + 3 * da
    w_in1 = one(w_in)
    wts = dict(
        g_mix=one(g_mix), g_ffn=one(g_ffn), b_f=one(b_f), s_pool=one(s_pool), b_router=one(b_router),
        w_cat=jnp.concatenate([w_in1[:, :n_main], w_in1[:, n_main + nh:]], axis=1).astype(BF16),
        w_fl_t=w_in1[:, n_main:n_main + nh].T.astype(BF16),
        w_pool=one(w_pool).astype(BF16),
        w_up_a=one(w_up_a).astype(BF16), w_up_b=one(w_up_b).astype(BF16), w_out=one(w_out).astype(BF16),
        w_router_t=one(w_router).T.astype(BF16),
        w_e_gate=one(w_e_gate), w_e_up=one(w_e_up), w_e_down=one(w_e_down),
        w_s_gate=one(w_s_gate).astype(BF16), w_s_up=one(w_s_up).astype(BF16), w_s_down=one(w_s_down).astype(BF16),
    )

    tiles_p = dict(tl=min(512, lp), tq=min(512, lp), t_route=min(512, bp * lp), tl_out=min(256, lp),
                   tl_final=min(256, lp), nb_in=1)
    tiles_s = dict(tl=ls, t_route=bs * ls, tl_out=ls, tl_final=ls, nb_in=bs, past_chunk=min(512, cache_k.shape[2]))

    hist_p = jnp.zeros((bp, HALO, dp), F32)
    grp_p, caches_p = _group_mixer(x_prompt, mod3, MOD_ROWS_PROMPT, hist_p, 0, None, wts, tiles_p)
    hist_s = jnp.pad(one(cache_pool), ((0, 0), (HALO - POOL_HIST, 0), (0, 0)))
    past = (one(cache_k), one(cache_v), one(cache_logf))
    grp_s, caches_s = _group_mixer(x_sample, mod3, MOD_ROWS_SAMPLE, hist_s, cache_k.shape[2], past, wts, tiles_s)

    n_chunks = MOE_CHUNKS if bp % MOE_CHUNKS == 0 else 1
    nb_c = bp // n_chunks
    chunks = [[(grp_p, c * nb_c, nb_c)] for c in range(n_chunks)]
    chunks[-1].append((grp_s, 0, bs))
    outs = {id(grp_p): None, id(grp_s): None}
    for segments in chunks:
        for (grp, b0, nb), (y_tok, gate) in zip(segments, _moe_chunk(segments, wts)):
            outs[id(grp)] = _final(grp["x1"], grp["h2"], b0, y_tok, gate, mod3, grp["mod_row0"], wts["w_s_gate"],
                                   wts["w_s_up"], wts["w_s_down"], g_final, grp["tiles"]["tl_final"],
                                   y_prev=outs[id(grp)])
    stack = lambda a: a[None]
    return (outs[id(grp_p)], outs[id(grp_s)], *map(stack, caches_p), *map(stack, caches_s))
```

```python
import functools

import jax
import jax.numpy as jnp
from jax import lax
from jax.experimental import pallas as pl
from jax.experimental.pallas import tpu as pltpu
from jax.experimental.pallas import tpu_sc as plsc

F32 = jnp.float32
BF16 = jnp.bfloat16
I32 = jnp.int32
U32 = jnp.uint32

EPS = 1e-6
N_MOD = 6
POOL_WINDOWS = (2, 4, 8, 16)
POOL_HIST = max(POOL_WINDOWS) - 1
HALO = 16
HEAD_DIM = 128
TOP_K = 8
ROUTED_SCALE = 2.5
MOE_ROWS = 512
MOE_CHAIN_ROWS = 256
MOE_CHUNKS = 2
MOD_ROWS_PROMPT = 0
MOD_ROWS_SAMPLE = 8
VMEM_LIMIT_BYTES = 52 * 1024 * 1024
NEG_INF = float("-inf")
LOG2_E = 1.4426950408889634
LANES = 128
PACK_CHUNKS = 8
SC_CORES = 2
SC_SUBCORES = 16
SC_WORKERS = SC_CORES * SC_SUBCORES
SC_WINDOW = 32
NT_DIMS = (((1,), (1,)), ((), ()))


def _params(*semantics):
    return pltpu.CompilerParams(dimension_semantics=semantics, vmem_limit_bytes=VMEM_LIMIT_BYTES)


def _silu(x):
    return x * jax.nn.sigmoid(x)


def _store_packed(ref, x, row0=0):
    rows, width = x.shape
    half = width // 2
    bits = pltpu.bitcast(x.astype(BF16).astype(F32), U32)
    packed = (bits[:, :half] >> 16) | (bits[:, half:] & jnp.uint32(0xFFFF0000))
    for j in range(PACK_CHUNKS):
        ref[pl.ds(row0 * PACK_CHUNKS + j, rows, stride=PACK_CHUNKS), :] = packed[:, j * LANES:(j + 1) * LANES]


def _load_packed(ref, rows, valid=None, row0=0):
    lo, hi = [], []
    for j in range(PACK_CHUNKS):
        p = ref[pl.ds(row0 * PACK_CHUNKS + j, rows, stride=PACK_CHUNKS), :]
        if valid is not None:
            p = jnp.where(row0 + lax.broadcasted_iota(I32, p.shape, 0) < valid, p, jnp.uint32(0))
        lo.append(pltpu.bitcast(p << 16, F32))
        hi.append(pltpu.bitcast(p & jnp.uint32(0xFFFF0000), F32))
    return jnp.concatenate(lo + hi, axis=1)


def _ada_kernel(c_ref, w_ref, b_ref, o_ref):
    a = _silu(c_ref[...]).astype(BF16)
    o_ref[...] = jnp.dot(a, w_ref[...].astype(BF16), preferred_element_type=F32) + b_ref[...]


def _ada_mod(c_all, w_ada, b_ada):
    rows, d = c_all.shape
    n = w_ada.shape[1]
    tn = 1024
    return pl.pallas_call(
        _ada_kernel,
        grid=(n // tn,),
        in_specs=[
            pl.BlockSpec((rows, d), lambda j: (0, 0)),
            pl.BlockSpec((d, tn), lambda j: (0, j)),
            pl.BlockSpec((1, tn), lambda j: (0, j)),
        ],
        out_specs=pl.BlockSpec((rows, tn), lambda j: (0, j)),
        out_shape=jax.ShapeDtypeStruct((rows, n), F32),
        compiler_params=_params("parallel"),
        name="ada_mod",
    )(c_all, w_ada, b_ada.reshape(1, n))


def _modulated_norm(x, g, m, shift_row, scale_row):
    ms = jnp.mean(x * x, axis=-1, keepdims=True)
    return x * lax.rsqrt(ms + EPS) * g * (1.0 + m[scale_row:scale_row + 1, :]) + m[shift_row:shift_row + 1, :]


def _in_proj_kernel(x_ref, g_ref, mod_ref, wfl_ref, bf_ref, w_ref,
                    u_ref, q_ref, k32_ref, kb_ref, v32_ref, vb_ref, gates_ref, lf_ref, h_ref, *, q_scale):
    j = pl.program_id(2)
    nb, tl, _ = x_ref.shape
    nh = wfl_ref.shape[0]

    @pl.when(j == 0)
    def _():
        for bi in range(nb):
            hb = _modulated_norm(x_ref[bi], g_ref[...], mod_ref[bi], 0, 1).astype(BF16)
            h_ref[bi * tl:(bi + 1) * tl, :] = hb
            fl = lax.dot_general(wfl_ref[...], hb, NT_DIMS, preferred_element_type=F32) + bf_ref[...]
            lf_ref[bi] = jnp.minimum(fl, 0.0) - jnp.log(1.0 + jnp.exp(-jnp.abs(fl)))

    def tile_rows():
        acc = jnp.dot(h_ref[...], w_ref[...], preferred_element_type=F32)
        return [acc[bi * tl:(bi + 1) * tl, :] for bi in range(nb)]

    def store_heads(f32_ref, bf_ref_):
        for bi, r in enumerate(tile_rows()):
            bf_ref_[bi] = r.astype(BF16)
            for hd in range(nh):
                f32_ref[bi, pl.ds(hd, tl, stride=nh), :] = r[:, hd * HEAD_DIM:(hd + 1) * HEAD_DIM]

    @pl.when(j == 0)
    def _():
        for bi, r in enumerate(tile_rows()):
            u_ref[bi] = r

    @pl.when(j == 1)
    def _():
        for bi, r in enumerate(tile_rows()):
            q_ref[bi] = (r * q_scale).astype(BF16)

    @pl.when(j == 2)
    def _():
        store_heads(k32_ref, kb_ref)

    @pl.when(j == 3)
    def _():
        store_heads(v32_ref, vb_ref)

    @pl.when(j >= 4)
    def _():
        for bi, r in enumerate(tile_rows()):
            gates_ref[bi] = jax.nn.sigmoid(r).astype(BF16)


def _in_proj(x3, g, mod3, mod_row0, w_fl_t, b_f, w_cat, q_scale, nb, tl):
    b, l, d = x3.shape
    nh = w_fl_t.shape[0]
    tn = nh * HEAD_DIM
    n_tiles = w_cat.shape[1] // tn
    n_gate_tiles = n_tiles - 4
    assert mod_row0 % nb == 0 and b % nb == 0
    tile = lambda width: pl.BlockSpec((nb, tl, width), lambda bb, li, j: (bb, li, 0))
    heads = pl.BlockSpec((nb, tl * nh, HEAD_DIM), lambda bb, li, j: (bb, li, 0))
    return pl.pallas_call(
        functools.partial(_in_proj_kernel, q_scale=q_scale),
        grid=(b // nb, l // tl, n_tiles),
        in_specs=[
            tile(d),
            pl.BlockSpec((1, d), lambda bb, li, j: (0, 0)),
            pl.BlockSpec((nb, N_MOD, d), lambda bb, li, j: (mod_row0 // nb + bb, 0, 0)),
            pl.BlockSpec((nh, d), lambda bb, li, j: (0, 0)),
            pl.BlockSpec((nh, 1), lambda bb, li, j: (0, 0)),
            pl.BlockSpec((d, tn), lambda bb, li, j: (0, j)),
        ],
        out_specs=[
            tile(tn), tile(tn), heads, tile(tn), heads, tile(tn),
            pl.BlockSpec((nb, tl, tn), lambda bb, li, j: (bb, li, jnp.maximum(j - 4, 0))),
            pl.BlockSpec((nb, nh, tl), lambda bb, li, j: (bb, 0, li)),
        ],
        out_shape=[
            jax.ShapeDtypeStruct((b, l, tn), F32),
            jax.ShapeDtypeStruct((b, l, tn), BF16),
            jax.ShapeDtypeStruct((b, l * nh, HEAD_DIM), F32),
            jax.ShapeDtypeStruct((b, l, tn), BF16),
            jax.ShapeDtypeStruct((b, l * nh, HEAD_DIM), F32),
            jax.ShapeDtypeStruct((b, l, tn), BF16),
            jax.ShapeDtypeStruct((b, l, n_gate_tiles * tn), BF16),
            jax.ShapeDtypeStruct((b, nh, l), F32),
        ],
        scratch_shapes=[pltpu.VMEM((nb * tl, d), BF16)],
        compiler_params=_params("parallel", "parallel", "arbitrary"),
        name="in_proj",
    )(x3, g.reshape(1, d), mod3, w_fl_t, b_f.reshape(nh, 1), w_cat)


def _mixer_out_kernel(ya_ref, yb_ref, ga_ref, gb_ref, x_ref, mod_ref, g_ref, wa_ref, wb_ref, wo_ref,
                      x1_ref, h2_ref, h2p_ref):
    merged = ga_ref[...].astype(F32) * jnp.dot(ya_ref[...], wa_ref[...], preferred_element_type=F32)
    merged = merged + gb_ref[...].astype(F32) * jnp.dot(yb_ref[...], wb_ref[...], preferred_element_type=F32)
    m = mod_ref[...]
    x1 = x_ref[...] + m[2:3, :] * jnp.dot(merged.astype(BF16), wo_ref[...], preferred_element_type=F32)
    x1_ref[...] = x1
    h2 = _modulated_norm(x1, g_ref[...], m, 3, 4)
    h2_ref[...] = h2.astype(BF16)
    _store_packed(h2p_ref, h2)


def _mixer_out(y_a, y_b, gates, x3, mod3, mod_row0, g_ffn, w_up_a, w_up_b, w_out, tl):
    b, l, d = x3.shape
    n_l = l // tl
    dp, da = y_a.shape[2], y_b.shape[2]
    tile = lambda width, col=0: pl.BlockSpec((None, tl, width), lambda bi, li: (bi, li, col))
    resident = lambda shape: pl.BlockSpec(shape, lambda bi, li: (0,) * len(shape), pipeline_mode=pl.Buffered(1))
    return pl.pallas_call(
        _mixer_out_kernel,
        grid=(b, n_l),
        in_specs=[
            tile(dp), tile(da), tile(d, 0), tile(d, 1), tile(d),
            pl.BlockSpec((None, N_MOD, d), lambda bi, li: (mod_row0 + bi, 0, 0)),
            pl.BlockSpec((1, d), lambda bi, li: (0, 0)),
            resident((dp, d)), resident((da, d)), resident((d, d)),
        ],
        out_specs=[
            tile(d), tile(d),
            pl.BlockSpec((tl * PACK_CHUNKS, LANES), lambda bi, li: (bi * n_l + li, 0)),
        ],
        out_shape=[
            jax.ShapeDtypeStruct((b, l, d), F32),
            jax.ShapeDtypeStruct((b, l, d), BF16),
            jax.ShapeDtypeStruct((b * l * PACK_CHUNKS, LANES), U32),
        ],
        compiler_params=_params("parallel", "parallel"),
        name="mixer_out",
    )(y_a, y_b, gates, gates, x3, mod3, g_ffn.reshape(1, d), w_up_a, w_up_b, w_out)


def _pool_kernel(u_ref, halo_ref, hist_ref, wp_ref, sp_ref, o_ref, full_ref, *, tl, start, gw):
    i = pl.program_id(1)
    full_ref[0:HALO, :] = jnp.where(i == 0, hist_ref[...], halo_ref[...])
    full_ref[HALO:HALO + tl, :] = u_ref[...]
    pos = start + i * tl + lax.broadcasted_iota(I32, (tl, 1), 0)
    for g, w in enumerate(POOL_WINDOWS):
        c0, c1 = g * gw, (g + 1) * gw
        cur = full_ref[HALO:HALO + tl, c0:c1]
        win = cur
        for d in range(1, w):
            win = win + full_ref[HALO - d:HALO - d + tl, c0:c1]
        cnt = jnp.minimum(pos + 1, w).astype(F32)
        dmean = win / cnt - cur
        y = jnp.dot(dmean.astype(BF16), wp_ref[g], preferred_element_type=F32) * sp_ref[:, c0:c1]
        o_ref[:, c0:c1] = y.astype(o_ref.dtype)


def _pool_mixer(u3, hist, w_pool, s_pool, start, tl):
    b, l, dp = u3.shape
    gw = dp // len(POOL_WINDOWS)
    halo_blocks = tl // HALO
    return pl.pallas_call(
        functools.partial(_pool_kernel, tl=tl, start=start, gw=gw),
        grid=(b, l // tl),
        in_specs=[
            pl.BlockSpec((None, tl, dp), lambda bi, li: (bi, li, 0)),
            pl.BlockSpec((None, HALO, dp), lambda bi, li: (bi, jnp.maximum(li * halo_blocks - 1, 0), 0)),
            pl.BlockSpec((None, HALO, dp), lambda bi, li: (bi, 0, 0)),
            pl.BlockSpec((len(POOL_WINDOWS), gw, gw), lambda bi, li: (0, 0, 0)),
            pl.BlockSpec((1, dp), lambda bi, li: (0, 0)),
        ],
        out_specs=pl.BlockSpec((None, tl, dp), lambda bi, li: (bi, li, 0)),
        out_shape=jax.ShapeDtypeStruct((b, l, dp), BF16),
        scratch_shapes=[pltpu.VMEM((HALO + tl, dp), F32)],
        compiler_params=_params("parallel", "parallel"),
        name="pool_mixer",
    )(u3, u3, hist, w_pool, s_pool.reshape(1, dp))


def _cumsum_kernel(x_ref, o_ref):
    x = x_ref[...]
    n = x.shape[1]
    idx = lax.broadcasted_iota(I32, x.shape, 1)
    s = 1
    while s < n:
        x = x + jnp.where(idx >= s, pltpu.roll(x, s, 1), 0.0)
        s *= 2
    o_ref[...] = x


def _cumsum_last(x3):
    b, h, n = x3.shape
    return pl.pallas_call(
        _cumsum_kernel,
        grid=(b,),
        in_specs=[pl.BlockSpec((None, h, n), lambda bi: (bi, 0, 0))],
        out_specs=pl.BlockSpec((None, h, n), lambda bi: (bi, 0, 0)),
        out_shape=jax.ShapeDtypeStruct((b, h, n), F32),
        compiler_params=_params("parallel"),
        name="cumsum_logf",
    )(x3)


def _attn_prompt_kernel(q_ref, k_ref, v_ref, fk_ref, o_ref, s0_ref, s1_ref, m_ref, l_ref, acc_ref, *, tq):
    qi = pl.program_id(2)
    q = q_ref[...]
    n_chunks = tq // LANES

    def produce(j, s_ref):
        off = pl.multiple_of(j * tq, tq)
        s = lax.dot_general(q, k_ref[pl.ds(off, tq), :], NT_DIMS, preferred_element_type=F32)
        s_ref[...] = s - fk_ref[j] * LOG2_E

    def consume(j, s_ref, diagonal):
        s = s_ref[...]
        if diagonal:
            row = lax.broadcasted_iota(I32, (tq, tq), 0)
            col = lax.broadcasted_iota(I32, (tq, tq), 1)
            s = jnp.where(col <= row, s, NEG_INF)
        chunks = [s[:, c * LANES:(c + 1) * LANES] for c in range(n_chunks)]
        m_old = m_ref[...]
        m_new = jnp.maximum(m_old, jnp.max(functools.reduce(jnp.maximum, chunks), axis=1, keepdims=True))
        alpha = jnp.exp2(m_old - m_new)
        p_chunks = [jnp.exp2(ch - m_new) for ch in chunks]
        l_ref[...] = alpha * l_ref[...] + functools.reduce(lambda a, c: a + c, p_chunks)
        p = jnp.concatenate(p_chunks, axis=1).astype(BF16)
        v = v_ref[pl.ds(pl.multiple_of(j * tq, tq), tq), :]
        acc_ref[...] = alpha * acc_ref[...] + jnp.dot(p, v, preferred_element_type=F32)
        m_ref[...] = m_new

    m_ref[...] = jnp.full(m_ref.shape, NEG_INF, F32)
    l_ref[...] = jnp.zeros(l_ref.shape, F32)
    acc_ref[...] = jnp.zeros(acc_ref.shape, F32)
    produce(0, s0_ref)

    def pair(jj, carry):
        j0 = 2 * jj
        produce(j0 + 1, s1_ref)
        consume(j0, s0_ref, False)
        produce(j0 + 2, s0_ref)
        consume(j0 + 1, s1_ref, False)
        return carry

    lax.fori_loop(0, qi // 2, pair, 0)

    @pl.when(qi % 2 == 1)
    def _():
        produce(qi, s1_ref)
        consume(qi - 1, s0_ref, False)
        consume(qi, s1_ref, True)

    @pl.when(qi % 2 == 0)
    def _():
        consume(qi, s0_ref, True)

    l = jnp.sum(l_ref[...], axis=1, keepdims=True)
    o_ref[...] = (acc_ref[...] / l).astype(o_ref.dtype)


def _attn_prompt(q3, k3, v3, f3, tq):
    b, l, da = q3.shape
    nh = da // HEAD_DIM
    nk = l // tq
    fk = f3.reshape(b * nh, nk, 1, tq)
    return pl.pallas_call(
        functools.partial(_attn_prompt_kernel, tq=tq),
        grid=(b, nh, nk),
        in_specs=[
            pl.BlockSpec((None, tq, HEAD_DIM), lambda bi, hi, qi: (bi, qi, hi)),
            pl.BlockSpec((None, l, HEAD_DIM), lambda bi, hi, qi: (bi, 0, hi)),
            pl.BlockSpec((None, l, HEAD_DIM), lambda bi, hi, qi: (bi, 0, hi)),
            pl.BlockSpec((None, nk, 1, tq), lambda bi, hi, qi: (bi * nh + hi, 0, 0, 0)),
        ],
        out_specs=pl.BlockSpec((None, tq, HEAD_DIM), lambda bi, hi, qi: (bi, qi, hi)),
        out_shape=jax.ShapeDtypeStruct((b, l, da), BF16),
        scratch_shapes=[
            pltpu.VMEM((tq, tq), F32),
            pltpu.VMEM((tq, tq), F32),
            pltpu.VMEM((tq, LANES), F32),
            pltpu.VMEM((tq, LANES), F32),
            pltpu.VMEM((tq, HEAD_DIM), F32),
        ],
        compiler_params=_params("parallel", "parallel", "parallel"),
        name="fox_prompt",
    )(q3, k3, v3, fk)


def _attn_sample_kernel(q_ref, kp_ref, vp_ref, kn_ref, vn_ref, fp_ref, fn_ref, o_ref, m_ref, l_ref, acc_ref, *,
                        nh, pc, lq):
    c = pl.program_id(1)

    @pl.when(c == 0)
    def _():
        m_ref[...] = jnp.full(m_ref.shape, NEG_INF, F32)
        l_ref[...] = jnp.zeros(l_ref.shape, F32)
        acc_ref[...] = jnp.zeros(acc_ref.shape, F32)

    for h in range(nh):
        cols = slice(h * HEAD_DIM, (h + 1) * HEAD_DIM)
        q = q_ref[:, cols]
        k = kp_ref[pl.ds(h, pc, stride=nh), :].astype(BF16)
        v = vp_ref[pl.ds(h, pc, stride=nh), :].astype(BF16)
        s = lax.dot_general(q, k, NT_DIMS, preferred_element_type=F32) - fp_ref[h:h + 1, :]
        chunks = [s[:, j * LANES:(j + 1) * LANES] for j in range(pc // LANES)]
        m_old = m_ref[h]
        m_new = jnp.maximum(m_old, jnp.max(functools.reduce(jnp.maximum, chunks), axis=1, keepdims=True))
        alpha = jnp.exp(m_old - m_new)
        p_chunks = [jnp.exp(ch - m_new) for ch in chunks]
        l_ref[h] = alpha * l_ref[h] + functools.reduce(lambda a, b: a + b, p_chunks)
        p = jnp.concatenate(p_chunks, axis=1).astype(BF16)
        acc_ref[h] = alpha * acc_ref[h] + jnp.dot(p, v, preferred_element_type=F32)
        m_ref[h] = m_new

    @pl.when(c == pl.num_programs(1) - 1)
    def _():
        row = lax.broadcasted_iota(I32, (lq, lq), 0)
        col = lax.broadcasted_iota(I32, (lq, lq), 1)
        for h in range(nh):
            cols = slice(h * HEAD_DIM, (h + 1) * HEAD_DIM)
            s = lax.dot_general(q_ref[:, cols], kn_ref[:, cols], NT_DIMS, preferred_element_type=F32)
            s = jnp.where(col <= row, s - fn_ref[h:h + 1, :lq], NEG_INF)
            m_old = m_ref[h]
            m_new = jnp.maximum(m_old, jnp.max(s, axis=1, keepdims=True))
            alpha = jnp.exp(m_old - m_new)
            p = jnp.exp(s - m_new[:, :lq])
            l = jnp.sum(alpha * l_ref[h], axis=1, keepdims=True) + jnp.sum(p, axis=1, keepdims=True)
            acc = alpha * acc_ref[h] + jnp.dot(p.astype(BF16), vn_ref[:, cols], preferred_element_type=F32)
            o_ref[:, cols] = (acc / l).astype(o_ref.dtype)


def _attn_sample(q3, kn3, vn3, cache_k, cache_v, f_past, f_new, pc):
    b, lq, da = q3.shape
    _, past, nh, dh = cache_k.shape
    nc = past // pc
    new_spec = pl.BlockSpec((None, lq, da), lambda bi, ci: (bi, 0, 0))
    past_spec = pl.BlockSpec((None, pc * nh, dh), lambda bi, ci: (bi, ci, 0))
    return pl.pallas_call(
        functools.partial(_attn_sample_kernel, nh=nh, pc=pc, lq=lq),
        grid=(b, nc),
        in_specs=[new_spec, past_spec, past_spec, new_spec, new_spec,
                  pl.BlockSpec((None, None, nh, pc), lambda bi, ci: (bi, ci, 0, 0)),
                  pl.BlockSpec((None, nh, LANES), lambda bi, ci: (bi, 0, 0))],
        out_specs=new_spec,
        out_shape=jax.ShapeDtypeStruct((b, lq, da), BF16),
        scratch_shapes=[pltpu.VMEM((nh, lq, LANES), F32), pltpu.VMEM((nh, lq, LANES), F32),
                        pltpu.VMEM((nh, lq, HEAD_DIM), F32)],
        compiler_params=_params("parallel", "arbitrary"),
        name="fox_sample",
    )(q3, cache_k.reshape(b, past * nh, dh), cache_v.reshape(b, past * nh, dh), kn3, vn3, f_past, f_new)


def _router_kernel(h_ref, wr_ref, br_ref, tri_ref, cnt0_ref, idx_ref, gate_ref, rank_ref, cnt_ref, carry_ref):
    @pl.when(pl.program_id(0) == 0)
    def _():
        carry_ref[...] = cnt0_ref[...]

    n_exp, tm = wr_ref.shape[0], h_ref.shape[0]
    scores = jax.nn.sigmoid(lax.dot_general(wr_ref[...], h_ref[...], NT_DIMS, preferred_element_type=F32))
    work = scores + br_ref[...]
    eidx = lax.broadcasted_iota(I32, (n_exp, tm), 0)
    picked = jnp.zeros((n_exp, tm), F32)
    onehots, gates = [], []
    for k in range(TOP_K):
        best = jnp.max(work, axis=0, keepdims=True)
        first = jnp.min(jnp.where(work == best, eidx, n_exp), axis=0, keepdims=True)
        onehot = eidx == first
        idx_ref[k:k + 1, :] = first
        gates.append(jnp.sum(jnp.where(onehot, scores, 0.0), axis=0, keepdims=True))
        onehots.append(onehot)
        picked = picked + jnp.where(onehot, 1.0, 0.0)
        work = jnp.where(onehot, NEG_INF, work)
    norm = ROUTED_SCALE / functools.reduce(lambda a, c: a + c, gates)
    cum = jnp.dot(picked.astype(BF16), tri_ref[...], preferred_element_type=F32)
    before = cum - picked + carry_ref[:, 0:1]
    for k in range(TOP_K):
        gate_ref[k:k + 1, :] = gates[k] * norm
        rank_ref[k:k + 1, :] = jnp.sum(jnp.where(onehots[k], before, 0.0), axis=0, keepdims=True).astype(I32)
    carry_ref[...] = carry_ref[...] + cum[:, tm - 1:tm]
    cnt_ref[...] = carry_ref[...]


def _router(h2, tok0, ntok, w_router_t, b_router, cnt0, tm):
    d = h2.shape[1]
    n_exp = w_router_t.shape[0]
    tri = jnp.triu(jnp.ones((tm, tm), BF16))
    tok_spec = pl.BlockSpec((TOP_K, tm), lambda i: (0, i))
    cnt_spec = pl.BlockSpec((n_exp, LANES), lambda i: (0, 0))
    return pl.pallas_call(
        _router_kernel,
        grid=(ntok // tm,),
        in_specs=[
            pl.BlockSpec((tm, d), lambda i: (tok0 // tm + i, 0)),
            pl.BlockSpec((n_exp, d), lambda i: (0, 0)),
            pl.BlockSpec((n_exp, 1), lambda i: (0, 0)),
            pl.BlockSpec((tm, tm), lambda i: (0, 0)),
            cnt_spec,
        ],
        out_specs=[tok_spec, tok_spec, tok_spec, cnt_spec],
        out_shape=[
            jax.ShapeDtypeStruct((TOP_K, ntok), I32),
            jax.ShapeDtypeStruct((TOP_K, ntok), F32),
            jax.ShapeDtypeStruct((TOP_K, ntok), I32),
            jax.ShapeDtypeStruct((n_exp, LANES), F32),
        ],
        scratch_shapes=[pltpu.VMEM((n_exp, LANES), F32)],
        compiler_params=_params("arbitrary"),
        name="moe_router",
    )(h2, w_router_t, b_router.reshape(n_exp, 1), tri, cnt0)


def _moe_kernel(be_ref, bv_ref, nx_ref, nu_ref, x_ref, wg_hbm, wu_hbm, wd_hbm, y_ref,
                wg_f32, wu_f32, wd_f32, wg_bf, wu_bf, wd_bf, sems):
    i = pl.program_id(0)
    active = i < nu_ref[0]
    expert = be_ref[i]

    def fetch(e):
        pairs = ((wg_hbm, wg_f32), (wu_hbm, wu_f32), (wd_hbm, wd_f32))
        return [pltpu.make_async_copy(w.at[e], buf, sems.at[n]) for n, (w, buf) in enumerate(pairs)]

    @pl.when(active & (i == 0))
    def _():
        for c in fetch(expert):
            c.start()

    @pl.when(active & ((i == 0) | (expert != be_ref[jnp.maximum(i - 1, 0)])))
    def _():
        for c in fetch(expert):
            c.wait()
        wg_bf[...] = wg_f32[...].astype(BF16)
        wu_bf[...] = wu_f32[...].astype(BF16)
        wd_bf[...] = wd_f32[...].astype(BF16)

        @pl.when(nx_ref[i] >= 0)
        def _():
            for c in fetch(nx_ref[i]):
                c.start()

    def chain(row0):
        x = _load_packed(x_ref, MOE_CHAIN_ROWS, valid=bv_ref[i], row0=row0).astype(BF16)
        g = jnp.dot(x, wg_bf[...], preferred_element_type=F32)
        u = jnp.dot(x, wu_bf[...], preferred_element_type=F32)
        a = (_silu(g) * u).astype(BF16)
        _store_packed(y_ref, jnp.dot(a, wd_bf[...], preferred_element_type=F32), row0=row0)

    for row0 in range(0, MOE_ROWS, MOE_CHAIN_ROWS):
        pl.when(active & (bv_ref[i] > row0))(functools.partial(chain, row0))


def _moe_experts(x_sorted, blk_e, blk_valid, blk_next, n_used, w_e_gate, w_e_up, w_e_down):
    cap = x_sorted.shape[0] // PACK_CHUNKS
    d, de = w_e_gate.shape[1:]
    n_blocks = cap // MOE_ROWS
    row_map = lambda i, be, bv, nx, nu: (jnp.minimum(i, nu[0] - 1), 0)
    hbm = pl.BlockSpec(memory_space=pl.ANY)
    return pl.pallas_call(
        _moe_kernel,
        grid_spec=pltpu.PrefetchScalarGridSpec(
            num_scalar_prefetch=4,
            grid=(n_blocks,),
            in_specs=[pl.BlockSpec((MOE_ROWS * PACK_CHUNKS, LANES), row_map), hbm, hbm, hbm],
            out_specs=pl.BlockSpec((MOE_ROWS * PACK_CHUNKS, LANES), row_map),
            scratch_shapes=[
                pltpu.VMEM((d, de), F32), pltpu.VMEM((d, de), F32), pltpu.VMEM((de, d), F32),
                pltpu.VMEM((d, de), BF16), pltpu.VMEM((d, de), BF16), pltpu.VMEM((de, d), BF16),
                pltpu.SemaphoreType.DMA((3,)),
            ],
        ),
        out_shape=jax.ShapeDtypeStruct(x_sorted.shape, U32),
        compiler_params=_params("arbitrary"),
        name="moe_experts",
    )(blk_e, blk_valid, blk_next, n_used, x_sorted, w_e_gate, w_e_up, w_e_down)


def _sc_mesh():
    return plsc.VectorSubcoreMesh(core_axis_name="c", subcore_axis_name="s")


def _sc_worker_id():
    return lax.axis_index("s") * SC_CORES + lax.axis_index("c")


def _sc_scatter_rows(segments, n_out):
    row_shape = segments[0][0].shape[1:]
    dtype = segments[0][0].dtype
    n_seg = len(segments)
    geom = [(tok0, dw.shape[0] // SC_WORKERS, dw.shape[2]) for _, tok0, dw in segments]
    scratch = []
    for _, n_win, win in geom:
        scratch += [pltpu.VMEM((n_win, TOP_K, win), I32), pltpu.VMEM((win,) + row_shape, dtype)]

    @functools.partial(
        pl.kernel, mesh=_sc_mesh(),
        out_type=jax.ShapeDtypeStruct((n_out,) + row_shape, dtype),
        scratch_types=scratch + [pltpu.SemaphoreType.DMA],
    )
    def scatter_kernel(*refs):
        out_hbm = refs[2 * n_seg]
        sem = refs[-1]
        for s, (tok0, n_win, win) in enumerate(geom):
            rows_hbm, idx_hbm = refs[2 * s], refs[2 * s + 1]
            idx_v, rows_v = refs[2 * n_seg + 1 + 2 * s], refs[2 * n_seg + 2 + 2 * s]
            first = _sc_worker_id() * n_win
            pltpu.sync_copy(idx_hbm.at[pl.ds(first, n_win)], idx_v)

            @pl.loop(0, n_win)
            def _(w):
                pltpu.sync_copy(rows_hbm.at[pl.ds(tok0 + (first + w) * win, win)], rows_v)
                copies = [pltpu.make_async_copy(rows_v, out_hbm.at[idx_v.at[w, k]], sem) for k in range(TOP_K)]
                for c in copies:
                    c.start()
                for c in copies:
                    c.wait()

    args = []
    for rows, _, dest_win in segments:
        args += [rows, dest_win]
    return scatter_kernel(*args)


def _sc_gather_rows(table, idx_wins):
    row_shape = table.shape[1:]
    n_seg = len(idx_wins)
    geom = [(iw.shape[0] // SC_WORKERS, iw.shape[1]) for iw in idx_wins]
    scratch = []
    for n_win, win in geom:
        scratch += [pltpu.VMEM((n_win, win), I32), pltpu.VMEM((win,) + row_shape, table.dtype)]

    @functools.partial(
        pl.kernel, mesh=_sc_mesh(),
        out_type=[jax.ShapeDtypeStruct((iw.shape[0] * iw.shape[1],) + row_shape, table.dtype) for iw in idx_wins],
        scratch_types=scratch + [pltpu.SemaphoreType.DMA],
    )
    def gather_kernel(*refs):
        table_hbm = refs[0]
        sem = refs[-1]
        for s, (n_win, win) in enumerate(geom):
            idx_hbm, out_hbm = refs[1 + s], refs[1 + n_seg + s]
            idx_v, rows_v = refs[1 + 2 * n_seg + 2 * s], refs[2 + 2 * n_seg + 2 * s]
            first = _sc_worker_id() * n_win
            pltpu.sync_copy(idx_hbm.at[pl.ds(first, n_win)], idx_v)

            @pl.loop(0, n_win)
            def _(w):
                pltpu.async_copy(table_hbm.at[idx_v.at[w]], rows_v, sem).wait()
                pltpu.sync_copy(rows_v, out_hbm.at[pl.ds((first + w) * win, win)])

    return gather_kernel(table, *idx_wins)


def _final_kernel(x1_ref, h2_ref, yt_ref, gate_ref, mod_ref, wsg_ref, wsu_ref, wsd_ref, gf_ref, *rest):
    o_ref = rest[-1]
    h2 = h2_ref[...]
    g = jnp.dot(h2, wsg_ref[...], preferred_element_type=F32)
    u = jnp.dot(h2, wsu_ref[...], preferred_element_type=F32)
    f = jnp.dot((_silu(g) * u).astype(BF16), wsd_ref[...], preferred_element_type=F32)
    gate = gate_ref[...]
    tl = h2.shape[0]
    for k in range(TOP_K):
        f = f + gate[:, k:k + 1] * _load_packed(yt_ref.at[k], tl)
    x2 = x1_ref[...] + mod_ref[N_MOD - 1:N_MOD, :] * f
    ms = jnp.mean(x2 * x2, axis=-1, keepdims=True)
    o_ref[...] = x2 * lax.rsqrt(ms + EPS) * gf_ref[...]


def _final(x1, h2, b0, y_tok, gate, mod3, mod_row0, w_s_gate, w_s_up, w_s_down, g_final, tl, y_prev=None):
    b, l, d = x1.shape
    nb = gate.shape[0]
    ds = w_s_gate.shape[1]
    tile = pl.BlockSpec((None, tl, d), lambda bi, li: (b0 + bi, li, 0))
    const = lambda shape: pl.BlockSpec(shape, lambda bi, li: (0,) * len(shape), pipeline_mode=pl.Buffered(1))
    in_specs = [
        tile,
        tile,
        pl.BlockSpec((TOP_K, None, tl * PACK_CHUNKS, LANES), lambda bi, li: (0, bi, li, 0)),
        pl.BlockSpec((None, tl, TOP_K), lambda bi, li: (bi, li, 0)),
        pl.BlockSpec((None, N_MOD, d), lambda bi, li: (mod_row0 + b0 + bi, 0, 0)),
        const((d, ds)),
        const((d, ds)),
        const((ds, d)),
        const((1, d)),
    ]
    args = [x1, h2, y_tok, gate, mod3, w_s_gate, w_s_up, w_s_down, g_final.reshape(1, d)]
    aliases = {}
    if y_prev is not None:
        in_specs.append(pl.BlockSpec(memory_space=pl.ANY))
        aliases = {len(args): 0}
        args.append(y_prev)
    return pl.pallas_call(
        _final_kernel,
        grid=(nb, l // tl),
        in_specs=in_specs,
        out_specs=tile,
        out_shape=jax.ShapeDtypeStruct((b, l, d), F32),
        input_output_aliases=aliases,
        compiler_params=_params("parallel", "parallel"),
        name="combine_final",
    )(*args)


def _group_mixer(x3, mod3, mod_row0, hist, start, past, wts, tiles):
    b, l, d = x3.shape
    tl = tiles["tl"]
    dp = wts["w_pool"].shape[0] * wts["w_pool"].shape[1]
    da = wts["w_up_b"].shape[0]
    nh = da // HEAD_DIM
    t = b * l

    assert dp == da, "u, q, k, v must each be one column tile of the input projection"
    scale = HEAD_DIM ** -0.5 * (LOG2_E if past is None else 1.0)
    u, qb, k, kb, v, vb, gates, logf_t = _in_proj(x3, wts["g_mix"], mod3, mod_row0, wts["w_fl_t"], wts["b_f"],
                                                  wts["w_cat"], scale, tiles["nb_in"], tl)

    y_a = _pool_mixer(u, hist, wts["w_pool"], wts["s_pool"], start, tl)

    if past is None:
        y_b = _attn_prompt(qb, kb, vb, _cumsum_last(logf_t), tiles["tq"])
    else:
        cache_k, cache_v, cache_logf = past
        p = cache_k.shape[1]
        pc = tiles["past_chunk"]
        assert l <= LANES and p % pc == 0 and p % LANES == 0
        lf_all = jnp.concatenate([jnp.swapaxes(cache_logf.astype(F32), 1, 2), logf_t,
                                  jnp.zeros((b, nh, LANES - l), F32)], axis=2)
        f_all = _cumsum_last(lf_all)
        f_past = f_all[:, :, :p].reshape(b, nh, p // pc, pc).transpose(0, 2, 1, 3)
        y_b = _attn_sample(qb, kb, vb, cache_k, cache_v, f_past, f_all[:, :, p:], pc)

    x1, h2, h2_packed = _mixer_out(y_a, y_b, gates, x3, mod3, mod_row0, wts["g_ffn"], wts["w_up_a"], wts["w_up_b"],
                                   wts["w_out"], tiles["tl_out"])
    caches = (u[:, l - POOL_HIST:, :], k.reshape(b, l, nh, HEAD_DIM), v.reshape(b, l, nh, HEAD_DIM),
              jnp.swapaxes(logf_t, 1, 2))
    return dict(x1=x1, h2=h2, h2_packed=h2_packed, mod_row0=mod_row0, tiles=tiles), caches


def _moe_chunk(segments, wts):
    n_exp = wts["w_router_t"].shape[0]
    row_shape = (PACK_CHUNKS, LANES)
    counts_f = jnp.zeros((n_exp, LANES), F32)
    routed = []
    for grp, b0, nb in segments:
        b, l, d = grp["h2"].shape
        ntok = nb * l
        idx_t, gate_t, rank_t, counts_f = _router(grp["h2"].reshape(b * l, d), b0 * l, ntok, wts["w_router_t"],
                                                  wts["b_router"], counts_f, min(grp["tiles"]["t_route"], ntok))
        routed.append((idx_t, gate_t, rank_t, ntok))
    counts = counts_f[:, 0].astype(I32)

    n_slots = sum(r[3] for r in routed) * TOP_K
    n_blocks = -(-n_slots // MOE_ROWS) + n_exp
    cap = n_blocks * MOE_ROWS
    padded = (counts + MOE_ROWS - 1) // MOE_ROWS * MOE_ROWS
    pad_end = jnp.cumsum(padded)
    pad_start = pad_end - padded
    blk_row0 = jnp.arange(n_blocks, dtype=I32) * MOE_ROWS
    blk_e = jnp.minimum(jnp.sum(pad_end[None, :] <= blk_row0[:, None], axis=1), n_exp - 1).astype(I32)
    blk_valid = jnp.clip(pad_start[blk_e] + counts[blk_e] - blk_row0, 0, MOE_ROWS).astype(I32)
    n_used = (pad_end[-1:] // MOE_ROWS).astype(I32)
    later = lax.cummin(jnp.where(counts > 0, jnp.arange(n_exp, dtype=I32), n_exp), axis=0, reverse=True)
    next_e = jnp.concatenate([later[1:], jnp.full((1,), n_exp, I32)])
    blk_next = jnp.where(next_e < n_exp, next_e, -1)[blk_e].astype(I32)

    scatter_segs, gather_idx = [], []
    for (grp, b0, nb), (idx_t, _, rank_t, ntok) in zip(segments, routed):
        b, l, _ = grp["h2"].shape
        dest = rank_t + pad_start[idx_t]
        win_d = min(SC_WINDOW, ntok // SC_WORKERS)
        dest_win = dest.reshape(TOP_K, ntok // win_d, win_d).transpose(1, 0, 2)
        scatter_segs.append((grp["h2_packed"].reshape((b * l,) + row_shape), b0 * l, dest_win))
        win_c = min(SC_WINDOW, ntok * TOP_K // SC_WORKERS)
        gather_idx.append(dest.reshape(ntok * TOP_K // win_c, win_c))

    x_sorted = _sc_scatter_rows(scatter_segs, cap)
    y_sorted = _moe_experts(x_sorted.reshape(cap * PACK_CHUNKS, LANES), blk_e, blk_valid, blk_next, n_used,
                            wts["w_e_gate"], wts["w_e_up"], wts["w_e_down"])
    y_toks = _sc_gather_rows(y_sorted.reshape((cap,) + row_shape), gather_idx)
    out = []
    for (grp, b0, nb), (_, gate_t, _, _), y_tok in zip(segments, routed, y_toks):
        l = grp["h2"].shape[1]
        out.append((y_tok.reshape(TOP_K, nb, l * PACK_CHUNKS, LANES), gate_t.T.reshape(nb, l, TOP_K)))
    return out


def kernel(x_prompt, x_sample, cache_pool, cache_k, cache_v, cache_logf, c_prompt, c_sample, w_ada, b_ada, g_mix, w_in, b_f, w_pool, s_pool, w_up_a, w_up_b, w_out, g_ffn, w_router, b_router, w_e_gate, w_e_up, w_e_down, w_s_gate, w_s_up, w_s_down, g_final):
    depth = w_ada.shape[0]
    assert depth == 1, "a single layer is supported"
    bp, lp, d = x_prompt.shape
    bs, ls, _ = x_sample.shape
    dp = w_pool.shape[1] * w_pool.shape[2]
    da = w_up_b.shape[1]
    nh = da // HEAD_DIM
    assert bp <= MOD_ROWS_SAMPLE

    one = lambda a: a.reshape(a.shape[1:])
    c_all = jnp.zeros((MOD_ROWS_SAMPLE + bs, d), F32).at[:bp].set(c_prompt).at[MOD_ROWS_SAMPLE:].set(c_sample)
    mod3 = _ada_mod(c_all, one(w_ada), one(b_ada)).reshape(c_all.shape[0], N_MOD, d)

    n_main = dp + 3 * da
    w_in1 = one(w_in)
    wts = dict(
        g_mix=one(g_mix), g_ffn=one(g_ffn), b_f=one(b_f), s_pool=one(s_pool), b_router=one(b_router),
        w_cat=jnp.concatenate([w_in1[:, :n_main], w_in1[:, n_main + nh:]], axis=1).astype(BF16),
        w_fl_t=w_in1[:, n_main:n_main + nh].T.astype(BF16),
        w_pool=one(w_pool).astype(BF16),
        w_up_a=one(w_up_a).astype(BF16), w_up_b=one(w_up_b).astype(BF16), w_out=one(w_out).astype(BF16),
        w_router_t=one(w_router).T.astype(BF16),
        w_e_gate=one(w_e_gate), w_e_up=one(w_e_up), w_e_down=one(w_e_down),
        w_s_gate=one(w_s_gate).astype(BF16), w_s_up=one(w_s_up).astype(BF16), w_s_down=one(w_s_down).astype(BF16),
    )

    tiles_p = dict(tl=min(512, lp), tq=min(512, lp), t_route=min(512, bp * lp), tl_out=min(256, lp),
                   tl_final=min(256, lp), nb_in=1)
    tiles_s = dict(tl=ls, t_route=bs * ls, tl_out=ls, tl_final=ls, nb_in=bs, past_chunk=min(512, cache_k.shape[2]))

    hist_p = jnp.zeros((bp, HALO, dp), F32)
    grp_p, caches_p = _group_mixer(x_prompt, mod3, MOD_ROWS_PROMPT, hist_p, 0, None, wts, tiles_p)
    hist_s = jnp.pad(one(cache_pool), ((0, 0), (HALO - POOL_HIST, 0), (0, 0)))
    past = (one(cache_k), one(cache_v), one(cache_logf))
    grp_s, caches_s = _group_mixer(x_sample, mod3, MOD_ROWS_SAMPLE, hist_s, cache_k.shape[2], past, wts, tiles_s)

    n_chunks = MOE_CHUNKS if bp % MOE_CHUNKS == 0 else 1
    nb_c = bp // n_chunks
    chunks = [[(grp_p, c * nb_c, nb_c)] for c in range(n_chunks)]
    chunks[-1].append((grp_s, 0, bs))
    outs = {id(grp_p): None, id(grp_s): None}
    for segments in chunks:
        for (grp, b0, nb), (y_tok, gate) in zip(segments, _moe_chunk(segments, wts)):
            outs[id(grp)] = _final(grp["x1"], grp["h2"], b0, y_tok, gate, mod3, grp["mod_row0"], wts["w_s_gate"],
                                   wts["w_s_up"], wts["w_s_down"], g_final, grp["tiles"]["tl_final"],
                                   y_prev=outs[id(grp)])
    stack = lambda a: a[None]
    return (outs[id(grp_p)], outs[id(grp_s)], *map(stack, caches_p), *map(stack, caches_s))
```

```python
import functools

import jax
import jax.numpy as jnp
from jax import lax
from jax.experimental import pallas as pl
from jax.experimental.pallas import tpu as pltpu
from jax.experimental.pallas import tpu_sc as plsc

F32 = jnp.float32
BF16 = jnp.bfloat16
I32 = jnp.int32
U32 = jnp.uint32

EPS = 1e-6
N_MOD = 6
POOL_WINDOWS = (2, 4, 8, 16)
POOL_HIST = max(POOL_WINDOWS) - 1
HALO = 16
HEAD_DIM = 128
TOP_K = 8
ROUTED_SCALE = 2.5
MOE_ROWS = 512
MOE_CHAIN_ROWS = MOE_ROWS // 2
MOE_CHUNKS = 2
MOD_ROWS_PROMPT = 0
MOD_ROWS_SAMPLE = 8
VMEM_LIMIT_BYTES = 52 * 1024 * 1024
NEG_INF = float("-inf")
LOG2_E = 1.4426950408889634
LANES = 128
PACK_CHUNKS = 8
SC_CORES = 2
SC_SUBCORES = 16
SC_WORKERS = SC_CORES * SC_SUBCORES
SC_WINDOW = 32
NT_DIMS = (((1,), (1,)), ((), ()))


def _params(*semantics):
    return pltpu.CompilerParams(dimension_semantics=semantics, vmem_limit_bytes=VMEM_LIMIT_BYTES)


def _silu(x):
    return x * jax.nn.sigmoid(x)


def _store_packed(ref, x, row0=0):
    rows, width = x.shape
    half = width // 2
    bits = pltpu.bitcast(x.astype(BF16).astype(F32), U32)
    packed = (bits[:, :half] >> 16) | (bits[:, half:] & jnp.uint32(0xFFFF0000))
    for j in range(PACK_CHUNKS):
        ref[pl.ds(row0 * PACK_CHUNKS + j, rows, stride=PACK_CHUNKS), :] = packed[:, j * LANES:(j + 1) * LANES]


def _load_packed(ref, rows, valid=None, row0=0):
    lo, hi = [], []
    for j in range(PACK_CHUNKS):
        p = ref[pl.ds(row0 * PACK_CHUNKS + j, rows, stride=PACK_CHUNKS), :]
        if valid is not None:
            p = jnp.where(row0 + lax.broadcasted_iota(I32, p.shape, 0) < valid, p, jnp.uint32(0))
        lo.append(pltpu.bitcast(p << 16, F32))
        hi.append(pltpu.bitcast(p & jnp.uint32(0xFFFF0000), F32))
    return jnp.concatenate(lo + hi, axis=1)


def _ada_kernel(c_ref, w_ref, b_ref, o_ref):
    a = _silu(c_ref[...]).astype(BF16)
    o_ref[...] = jnp.dot(a, w_ref[...].astype(BF16), preferred_element_type=F32) + b_ref[...]


def _ada_mod(c_all, w_ada, b_ada):
    rows, d = c_all.shape
    n = w_ada.shape[1]
    tn = 1024
    return pl.pallas_call(
        _ada_kernel,
        grid=(n // tn,),
        in_specs=[
            pl.BlockSpec((rows, d), lambda j: (0, 0)),
            pl.BlockSpec((d, tn), lambda j: (0, j)),
            pl.BlockSpec((1, tn), lambda j: (0, j)),
        ],
        out_specs=pl.BlockSpec((rows, tn), lambda j: (0, j)),
        out_shape=jax.ShapeDtypeStruct((rows, n), F32),
        compiler_params=_params("parallel"),
        name="ada_mod",
    )(c_all, w_ada, b_ada.reshape(1, n))


def _modulated_norm(x, g, m, shift_row, scale_row):
    ms = jnp.mean(x * x, axis=-1, keepdims=True)
    return x * lax.rsqrt(ms + EPS) * g * (1.0 + m[scale_row:scale_row + 1, :]) + m[shift_row:shift_row + 1, :]


def _in_proj_kernel(x_ref, g_ref, mod_ref, wfl_ref, bf_ref, w_ref,
                    u_ref, q_ref, k32_ref, kb_ref, v32_ref, vb_ref, gates_ref, lf_ref, h_ref, *, q_scale):
    j = pl.program_id(2)
    nb, tl, _ = x_ref.shape
    nh = wfl_ref.shape[0]

    @pl.when(j == 0)
    def _():
        for bi in range(nb):
            hb = _modulated_norm(x_ref[bi], g_ref[...], mod_ref[bi], 0, 1).astype(BF16)
            h_ref[bi * tl:(bi + 1) * tl, :] = hb
            fl = lax.dot_general(wfl_ref[...], hb, NT_DIMS, preferred_element_type=F32) + bf_ref[...]
            lf_ref[bi] = jnp.minimum(fl, 0.0) - jnp.log(1.0 + jnp.exp(-jnp.abs(fl)))

    def tile_rows():
        acc = jnp.dot(h_ref[...], w_ref[...], preferred_element_type=F32)
        return [acc[bi * tl:(bi + 1) * tl, :] for bi in range(nb)]

    def store_heads(f32_ref, bf_ref_):
        for bi, r in enumerate(tile_rows()):
            bf_ref_[bi] = r.astype(BF16)
            for hd in range(nh):
                f32_ref[bi, pl.ds(hd, tl, stride=nh), :] = r[:, hd * HEAD_DIM:(hd + 1) * HEAD_DIM]

    @pl.when(j == 0)
    def _():
        for bi, r in enumerate(tile_rows()):
            u_ref[bi] = r

    @pl.when(j == 1)
    def _():
        for bi, r in enumerate(tile_rows()):
            q_ref[bi] = (r * q_scale).astype(BF16)

    @pl.when(j == 2)
    def _():
        store_heads(k32_ref, kb_ref)

    @pl.when(j == 3)
    def _():
        store_heads(v32_ref, vb_ref)

    @pl.when(j >= 4)
    def _():
        for bi, r in enumerate(tile_rows()):
            gates_ref[bi] = jax.nn.sigmoid(r).astype(BF16)


def _in_proj(x3, g, mod3, mod_row0, w_fl_t, b_f, w_cat, q_scale, nb, tl):
    b, l, d = x3.shape
    nh = w_fl_t.shape[0]
    tn = nh * HEAD_DIM
    n_tiles = w_cat.shape[1] // tn
    n_gate_tiles = n_tiles - 4
    assert mod_row0 % nb == 0 and b % nb == 0
    tile = lambda width: pl.BlockSpec((nb, tl, width), lambda bb, li, j: (bb, li, 0))
    heads = pl.BlockSpec((nb, tl * nh, HEAD_DIM), lambda bb, li, j: (bb, li, 0))
    return pl.pallas_call(
        functools.partial(_in_proj_kernel, q_scale=q_scale),
        grid=(b // nb, l // tl, n_tiles),
        in_specs=[
            tile(d),
            pl.BlockSpec((1, d), lambda bb, li, j: (0, 0)),
            pl.BlockSpec((nb, N_MOD, d), lambda bb, li, j: (mod_row0 // nb + bb, 0, 0)),
            pl.BlockSpec((nh, d), lambda bb, li, j: (0, 0)),
            pl.BlockSpec((nh, 1), lambda bb, li, j: (0, 0)),
            pl.BlockSpec((d, tn), lambda bb, li, j: (0, j)),
        ],
        out_specs=[
            tile(tn), tile(tn), heads, tile(tn), heads, tile(tn),
            pl.BlockSpec((nb, tl, tn), lambda bb, li, j: (bb, li, jnp.maximum(j - 4, 0))),
            pl.BlockSpec((nb, nh, tl), lambda bb, li, j: (bb, 0, li)),
        ],
        out_shape=[
            jax.ShapeDtypeStruct((b, l, tn), F32),
            jax.ShapeDtypeStruct((b, l, tn), BF16),
            jax.ShapeDtypeStruct((b, l * nh, HEAD_DIM), F32),
            jax.ShapeDtypeStruct((b, l, tn), BF16),
            jax.ShapeDtypeStruct((b, l * nh, HEAD_DIM), F32),
            jax.ShapeDtypeStruct((b, l, tn), BF16),
            jax.ShapeDtypeStruct((b, l, n_gate_tiles * tn), BF16),
            jax.ShapeDtypeStruct((b, nh, l), F32),
        ],
        scratch_shapes=[pltpu.VMEM((nb * tl, d), BF16)],
        compiler_params=_params("parallel", "parallel", "arbitrary"),
        name="in_proj",
    )(x3, g.reshape(1, d), mod3, w_fl_t, b_f.reshape(nh, 1), w_cat)


def _mixer_out_kernel(ya_ref, yb_ref, ga_ref, gb_ref, x_ref, mod_ref, g_ref, wa_ref, wb_ref, wo_ref,
                      x1_ref, h2_ref, h2p_ref):
    merged = ga_ref[...].astype(F32) * jnp.dot(ya_ref[...], wa_ref[...], preferred_element_type=F32)
    merged = merged + gb_ref[...].astype(F32) * jnp.dot(yb_ref[...], wb_ref[...], preferred_element_type=F32)
    m = mod_ref[...]
    x1 = x_ref[...] + m[2:3, :] * jnp.dot(merged.astype(BF16), wo_ref[...], preferred_element_type=F32)
    x1_ref[...] = x1
    h2 = _modulated_norm(x1, g_ref[...], m, 3, 4)
    h2_ref[...] = h2.astype(BF16)
    _store_packed(h2p_ref, h2)


def _mixer_out(y_a, y_b, gates, x3, mod3, mod_row0, g_ffn, w_up_a, w_up_b, w_out, tl):
    b, l, d = x3.shape
    n_l = l // tl
    dp, da = y_a.shape[2], y_b.shape[2]
    tile = lambda width, col=0: pl.BlockSpec((None, tl, width), lambda bi, li: (bi, li, col))
    resident = lambda shape: pl.BlockSpec(shape, lambda bi, li: (0,) * len(shape), pipeline_mode=pl.Buffered(1))
    return pl.pallas_call(
        _mixer_out_kernel,
        grid=(b, n_l),
        in_specs=[
            tile(dp), tile(da), tile(d, 0), tile(d, 1), tile(d),
            pl.BlockSpec((None, N_MOD, d), lambda bi, li: (mod_row0 + bi, 0, 0)),
            pl.BlockSpec((1, d), lambda bi, li: (0, 0)),
            resident((dp, d)), resident((da, d)), resident((d, d)),
        ],
        out_specs=[
            tile(d), tile(d),
            pl.BlockSpec((tl * PACK_CHUNKS, LANES), lambda bi, li: (bi * n_l + li, 0)),
        ],
        out_shape=[
            jax.ShapeDtypeStruct((b, l, d), F32),
            jax.ShapeDtypeStruct((b, l, d), BF16),
            jax.ShapeDtypeStruct((b * l * PACK_CHUNKS, LANES), U32),
        ],
        compiler_params=_params("parallel", "parallel"),
        name="mixer_out",
    )(y_a, y_b, gates, gates, x3, mod3, g_ffn.reshape(1, d), w_up_a, w_up_b, w_out)


def _pool_kernel(u_ref, halo_ref, hist_ref, wp_ref, sp_ref, o_ref, full_ref, *, tl, start, gw):
    i = pl.program_id(1)
    full_ref[0:HALO, :] = jnp.where(i == 0, hist_ref[...], halo_ref[...])
    full_ref[HALO:HALO + tl, :] = u_ref[...]
    pos = start + i * tl + lax.broadcasted_iota(I32, (tl, 1), 0)
    for g, w in enumerate(POOL_WINDOWS):
        c0, c1 = g * gw, (g + 1) * gw
        cur = full_ref[HALO:HALO + tl, c0:c1]
        win = cur
        for d in range(1, w):
            win = win + full_ref[HALO - d:HALO - d + tl, c0:c1]
        cnt = jnp.minimum(pos + 1, w).astype(F32)
        dmean = win / cnt - cur
        y = jnp.dot(dmean.astype(BF16), wp_ref[g], preferred_element_type=F32) * sp_ref[:, c0:c1]
        o_ref[:, c0:c1] = y.astype(o_ref.dtype)


def _pool_mixer(u3, hist, w_pool, s_pool, start, tl):
    b, l, dp = u3.shape
    gw = dp // len(POOL_WINDOWS)
    halo_blocks = tl // HALO
    return pl.pallas_call(
        functools.partial(_pool_kernel, tl=tl, start=start, gw=gw),
        grid=(b, l // tl),
        in_specs=[
            pl.BlockSpec((None, tl, dp), lambda bi, li: (bi, li, 0)),
            pl.BlockSpec((None, HALO, dp), lambda bi, li: (bi, jnp.maximum(li * halo_blocks - 1, 0), 0)),
            pl.BlockSpec((None, HALO, dp), lambda bi, li: (bi, 0, 0)),
            pl.BlockSpec((len(POOL_WINDOWS), gw, gw), lambda bi, li: (0, 0, 0)),
            pl.BlockSpec((1, dp), lambda bi, li: (0, 0)),
        ],
        out_specs=pl.BlockSpec((None, tl, dp), lambda bi, li: (bi, li, 0)),
        out_shape=jax.ShapeDtypeStruct((b, l, dp), BF16),
        scratch_shapes=[pltpu.VMEM((HALO + tl, dp), F32)],
        compiler_params=_params("parallel", "parallel"),
        name="pool_mixer",
    )(u3, u3, hist, w_pool, s_pool.reshape(1, dp))


def _cumsum_kernel(x_ref, o_ref):
    x = x_ref[...]
    n = x.shape[1]
    idx = lax.broadcasted_iota(I32, x.shape, 1)
    s = 1
    while s < n:
        x = x + jnp.where(idx >= s, pltpu.roll(x, s, 1), 0.0)
        s *= 2
    o_ref[...] = x


def _cumsum_last(x3):
    b, h, n = x3.shape
    return pl.pallas_call(
        _cumsum_kernel,
        grid=(b,),
        in_specs=[pl.BlockSpec((None, h, n), lambda bi: (bi, 0, 0))],
        out_specs=pl.BlockSpec((None, h, n), lambda bi: (bi, 0, 0)),
        out_shape=jax.ShapeDtypeStruct((b, h, n), F32),
        compiler_params=_params("parallel"),
        name="cumsum_logf",
    )(x3)


def _attn_prompt_kernel(q_ref, k_ref, v_ref, fk_ref, o_ref, s0_ref, s1_ref, m_ref, l_ref, acc_ref, *, tq):
    qi = pl.program_id(2)
    q = q_ref[...]
    n_chunks = tq // LANES

    def produce(j, s_ref):
        off = pl.multiple_of(j * tq, tq)
        s = lax.dot_general(q, k_ref[pl.ds(off, tq), :], NT_DIMS, preferred_element_type=F32)
        s_ref[...] = s - fk_ref[j] * LOG2_E

    def consume(j, s_ref, diagonal):
        s = s_ref[...]
        if diagonal:
            row = lax.broadcasted_iota(I32, (tq, tq), 0)
            col = lax.broadcasted_iota(I32, (tq, tq), 1)
            s = jnp.where(col <= row, s, NEG_INF)
        chunks = [s[:, c * LANES:(c + 1) * LANES] for c in range(n_chunks)]
        m_old = m_ref[...]
        m_new = jnp.maximum(m_old, jnp.max(functools.reduce(jnp.maximum, chunks), axis=1, keepdims=True))
        alpha = jnp.exp2(m_old - m_new)
        p_chunks = [jnp.exp2(ch - m_new) for ch in chunks]
        l_ref[...] = alpha * l_ref[...] + functools.reduce(lambda a, c: a + c, p_chunks)
        p = jnp.concatenate(p_chunks, axis=1).astype(BF16)
        v = v_ref[pl.ds(pl.multiple_of(j * tq, tq), tq), :]
        acc_ref[...] = alpha * acc_ref[...] + jnp.dot(p, v, preferred_element_type=F32)
        m_ref[...] = m_new

    m_ref[...] = jnp.full(m_ref.shape, NEG_INF, F32)
    l_ref[...] = jnp.zeros(l_ref.shape, F32)
    acc_ref[...] = jnp.zeros(acc_ref.shape, F32)
    produce(0, s0_ref)

    def pair(jj, carry):
        j0 = 2 * jj
        produce(j0 + 1, s1_ref)
        consume(j0, s0_ref, False)
        produce(j0 + 2, s0_ref)
        consume(j0 + 1, s1_ref, False)
        return carry

    lax.fori_loop(0, qi // 2, pair, 0)

    @pl.when(qi % 2 == 1)
    def _():
        produce(qi, s1_ref)
        consume(qi - 1, s0_ref, False)
        consume(qi, s1_ref, True)

    @pl.when(qi % 2 == 0)
    def _():
        consume(qi, s0_ref, True)

    l = jnp.sum(l_ref[...], axis=1, keepdims=True)
    o_ref[...] = (acc_ref[...] / l).astype(o_ref.dtype)


def _attn_prompt(q3, k3, v3, f3, tq):
    b, l, da = q3.shape
    nh = da // HEAD_DIM
    nk = l // tq
    fk = f3.reshape(b * nh, nk, 1, tq)
    return pl.pallas_call(
        functools.partial(_attn_prompt_kernel, tq=tq),
        grid=(b, nh, nk),
        in_specs=[
            pl.BlockSpec((None, tq, HEAD_DIM), lambda bi, hi, qi: (bi, qi, hi)),
            pl.BlockSpec((None, l, HEAD_DIM), lambda bi, hi, qi: (bi, 0, hi)),
            pl.BlockSpec((None, l, HEAD_DIM), lambda bi, hi, qi: (bi, 0, hi)),
            pl.BlockSpec((None, nk, 1, tq), lambda bi, hi, qi: (bi * nh + hi, 0, 0, 0)),
        ],
        out_specs=pl.BlockSpec((None, tq, HEAD_DIM), lambda bi, hi, qi: (bi, qi, hi)),
        out_shape=jax.ShapeDtypeStruct((b, l, da), BF16),
        scratch_shapes=[
            pltpu.VMEM((tq, tq), F32),
            pltpu.VMEM((tq, tq), F32),
            pltpu.VMEM((tq, LANES), F32),
            pltpu.VMEM((tq, LANES), F32),
            pltpu.VMEM((tq, HEAD_DIM), F32),
        ],
        compiler_params=_params("parallel", "parallel", "parallel"),
        name="fox_prompt",
    )(q3, k3, v3, fk)


def _attn_sample_kernel(q_ref, kp_ref, vp_ref, kn_ref, vn_ref, fp_ref, fn_ref, o_ref, m_ref, l_ref, acc_ref, *,
                        nh, pc, lq):
    c = pl.program_id(1)

    @pl.when(c == 0)
    def _():
        m_ref[...] = jnp.full(m_ref.shape, NEG_INF, F32)
        l_ref[...] = jnp.zeros(l_ref.shape, F32)
        acc_ref[...] = jnp.zeros(acc_ref.shape, F32)

    for h in range(nh):
        cols = slice(h * HEAD_DIM, (h + 1) * HEAD_DIM)
        q = q_ref[:, cols]
        k = kp_ref[pl.ds(h, pc, stride=nh), :].astype(BF16)
        v = vp_ref[pl.ds(h, pc, stride=nh), :].astype(BF16)
        s = lax.dot_general(q, k, NT_DIMS, preferred_element_type=F32) - fp_ref[h:h + 1, :]
        chunks = [s[:, j * LANES:(j + 1) * LANES] for j in range(pc // LANES)]
        m_old = m_ref[h]
        m_new = jnp.maximum(m_old, jnp.max(functools.reduce(jnp.maximum, chunks), axis=1, keepdims=True))
        alpha = jnp.exp(m_old - m_new)
        p_chunks = [jnp.exp(ch - m_new) for ch in chunks]
        l_ref[h] = alpha * l_ref[h] + functools.reduce(lambda a, b: a + b, p_chunks)
        p = jnp.concatenate(p_chunks, axis=1).astype(BF16)
        acc_ref[h] = alpha * acc_ref[h] + jnp.dot(p, v, preferred_element_type=F32)
        m_ref[h] = m_new

    @pl.when(c == pl.num_programs(1) - 1)
    def _():
        row = lax.broadcasted_iota(I32, (lq, lq), 0)
        col = lax.broadcasted_iota(I32, (lq, lq), 1)
        for h in range(nh):
            cols = slice(h * HEAD_DIM, (h + 1) * HEAD_DIM)
            s = lax.dot_general(q_ref[:, cols], kn_ref[:, cols], NT_DIMS, preferred_element_type=F32)
            s = jnp.where(col <= row, s - fn_ref[h:h + 1, :lq], NEG_INF)
            m_old = m_ref[h]
            m_new = jnp.maximum(m_old, jnp.max(s, axis=1, keepdims=True))
            alpha = jnp.exp(m_old - m_new)
            p = jnp.exp(s - m_new[:, :lq])
            l = jnp.sum(alpha * l_ref[h], axis=1, keepdims=True) + jnp.sum(p, axis=1, keepdims=True)
            acc = alpha * acc_ref[h] + jnp.dot(p.astype(BF16), vn_ref[:, cols], preferred_element_type=F32)
            o_ref[:, cols] = (acc / l).astype(o_ref.dtype)


def _attn_sample(q3, kn3, vn3, cache_k, cache_v, f_past, f_new, pc):
    b, lq, da = q3.shape
    _, past, nh, dh = cache_k.shape
    nc = past // pc
    new_spec = pl.BlockSpec((None, lq, da), lambda bi, ci: (bi, 0, 0))
    past_spec = pl.BlockSpec((None, pc * nh, dh), lambda bi, ci: (bi, ci, 0))
    return pl.pallas_call(
        functools.partial(_attn_sample_kernel, nh=nh, pc=pc, lq=lq),
        grid=(b, nc),
        in_specs=[new_spec, past_spec, past_spec, new_spec, new_spec,
                  pl.BlockSpec((None, None, nh, pc), lambda bi, ci: (bi, ci, 0, 0)),
                  pl.BlockSpec((None, nh, LANES), lambda bi, ci: (bi, 0, 0))],
        out_specs=new_spec,
        out_shape=jax.ShapeDtypeStruct((b, lq, da), BF16),
        scratch_shapes=[pltpu.VMEM((nh, lq, LANES), F32), pltpu.VMEM((nh, lq, LANES), F32),
                        pltpu.VMEM((nh, lq, HEAD_DIM), F32)],
        compiler_params=_params("parallel", "arbitrary"),
        name="fox_sample",
    )(q3, cache_k.reshape(b, past * nh, dh), cache_v.reshape(b, past * nh, dh), kn3, vn3, f_past, f_new)


def _router_kernel(h_ref, wr_ref, br_ref, tri_ref, cnt0_ref, idx_ref, gate_ref, rank_ref, cnt_ref, carry_ref):
    @pl.when(pl.program_id(0) == 0)
    def _():
        carry_ref[...] = cnt0_ref[...]

    n_exp, tm = wr_ref.shape[0], h_ref.shape[0]
    scores = jax.nn.sigmoid(lax.dot_general(wr_ref[...], h_ref[...], NT_DIMS, preferred_element_type=F32))
    work = scores + br_ref[...]
    eidx = lax.broadcasted_iota(I32, (n_exp, tm), 0)
    picked = jnp.zeros((n_exp, tm), F32)
    onehots, gates = [], []
    for k in range(TOP_K):
        best = jnp.max(work, axis=0, keepdims=True)
        first = jnp.min(jnp.where(work == best, eidx, n_exp), axis=0, keepdims=True)
        onehot = eidx == first
        idx_ref[k:k + 1, :] = first
        gates.append(jnp.sum(jnp.where(onehot, scores, 0.0), axis=0, keepdims=True))
        onehots.append(onehot)
        picked = picked + jnp.where(onehot, 1.0, 0.0)
        work = jnp.where(onehot, NEG_INF, work)
    norm = ROUTED_SCALE / functools.reduce(lambda a, c: a + c, gates)
    cum = jnp.dot(picked.astype(BF16), tri_ref[...], preferred_element_type=F32)
    before = cum - picked + carry_ref[:, 0:1]
    for k in range(TOP_K):
        gate_ref[k:k + 1, :] = gates[k] * norm
        rank_ref[k:k + 1, :] = jnp.sum(jnp.where(onehots[k], before, 0.0), axis=0, keepdims=True).astype(I32)
    carry_ref[...] = carry_ref[...] + cum[:, tm - 1:tm]
    cnt_ref[...] = carry_ref[...]


def _router(h2, tok0, ntok, w_router_t, b_router, cnt0, tm):
    d = h2.shape[1]
    n_exp = w_router_t.shape[0]
    tri = jnp.triu(jnp.ones((tm, tm), BF16))
    tok_spec = pl.BlockSpec((TOP_K, tm), lambda i: (0, i))
    cnt_spec = pl.BlockSpec((n_exp, LANES), lambda i: (0, 0))
    return pl.pallas_call(
        _router_kernel,
        grid=(ntok // tm,),
        in_specs=[
            pl.BlockSpec((tm, d), lambda i: (tok0 // tm + i, 0)),
            pl.BlockSpec((n_exp, d), lambda i: (0, 0)),
            pl.BlockSpec((n_exp, 1), lambda i: (0, 0)),
            pl.BlockSpec((tm, tm), lambda i: (0, 0)),
            cnt_spec,
        ],
        out_specs=[tok_spec, tok_spec, tok_spec, cnt_spec],
        out_shape=[
            jax.ShapeDtypeStruct((TOP_K, ntok), I32),
            jax.ShapeDtypeStruct((TOP_K, ntok), F32),
            jax.ShapeDtypeStruct((TOP_K, ntok), I32),
            jax.ShapeDtypeStruct((n_exp, LANES), F32),
        ],
        scratch_shapes=[pltpu.VMEM((n_exp, LANES), F32)],
        compiler_params=_params("arbitrary"),
        name="moe_router",
    )(h2, w_router_t, b_router.reshape(n_exp, 1), tri, cnt0)


def _slot_rows_kernel(ps_ref, idx_ref, rank_ref, o_ref):
    idx = idx_ref[...]
    rows = rank_ref[...]
    for e in range(ps_ref.shape[0]):
        rows = rows + jnp.where(idx == e, ps_ref[e], 0)
    o_ref[...] = rows


def _slot_rows(pad_start, idx_t, rank_t):
    k, ntok = idx_t.shape
    tm = min(2048, ntok)
    spec = pl.BlockSpec((k, tm), lambda i, ps: (0, i))
    return pl.pallas_call(
        _slot_rows_kernel,
        grid_spec=pltpu.PrefetchScalarGridSpec(num_scalar_prefetch=1, grid=(ntok // tm,), in_specs=[spec, spec],
                                               out_specs=spec),
        out_shape=jax.ShapeDtypeStruct((k, ntok), I32),
        compiler_params=_params("parallel"),
        name="moe_slot_rows",
    )(pad_start, idx_t, rank_t)


def _moe_kernel(be_ref, bv_ref, nx_ref, nu_ref, x_ref, wg_hbm, wu_hbm, wd_hbm, y_ref,
                wg_f32, wu_f32, wd_f32, wg_bf, wu_bf, wd_bf, sems):
    i = pl.program_id(0)
    active = i < nu_ref[0]
    expert = be_ref[i]

    def fetch(e):
        pairs = ((wg_hbm, wg_f32), (wu_hbm, wu_f32), (wd_hbm, wd_f32))
        return [pltpu.make_async_copy(w.at[e], buf, sems.at[n]) for n, (w, buf) in enumerate(pairs)]

    @pl.when(active & (i == 0))
    def _():
        for c in fetch(expert):
            c.start()

    @pl.when(active & ((i == 0) | (expert != be_ref[jnp.maximum(i - 1, 0)])))
    def _():
        for c in fetch(expert):
            c.wait()
        wg_bf[...] = wg_f32[...].astype(BF16)
        wu_bf[...] = wu_f32[...].astype(BF16)
        wd_bf[...] = wd_f32[...].astype(BF16)

        @pl.when(nx_ref[i] >= 0)
        def _():
            for c in fetch(nx_ref[i]):
                c.start()

    def chain(row0):
        x = _load_packed(x_ref, MOE_CHAIN_ROWS, valid=bv_ref[i], row0=row0).astype(BF16)
        g = jnp.dot(x, wg_bf[...], preferred_element_type=F32)
        u = jnp.dot(x, wu_bf[...], preferred_element_type=F32)
        a = (_silu(g) * u).astype(BF16)
        _store_packed(y_ref, jnp.dot(a, wd_bf[...], preferred_element_type=F32), row0=row0)

    both = bv_ref[i] > MOE_CHAIN_ROWS

    @pl.when(active & both)
    def _():
        chain(0)
        chain(MOE_CHAIN_ROWS)

    @pl.when(active & jnp.logical_not(both))
    def _():
        chain(0)


def _moe_experts(x_sorted, blk_e, blk_valid, blk_next, n_used, w_e_gate, w_e_up, w_e_down):
    cap = x_sorted.shape[0] // PACK_CHUNKS
    d, de = w_e_gate.shape[1:]
    n_blocks = cap // MOE_ROWS
    row_map = lambda i, be, bv, nx, nu: (jnp.minimum(i, nu[0] - 1), 0)
    hbm = pl.BlockSpec(memory_space=pl.ANY)
    return pl.pallas_call(
        _moe_kernel,
        grid_spec=pltpu.PrefetchScalarGridSpec(
            num_scalar_prefetch=4,
            grid=(n_blocks,),
            in_specs=[pl.BlockSpec((MOE_ROWS * PACK_CHUNKS, LANES), row_map), hbm, hbm, hbm],
            out_specs=pl.BlockSpec((MOE_ROWS * PACK_CHUNKS, LANES), row_map),
            scratch_shapes=[
                pltpu.VMEM((d, de), F32), pltpu.VMEM((d, de), F32), pltpu.VMEM((de, d), F32),
                pltpu.VMEM((d, de), BF16), pltpu.VMEM((d, de), BF16), pltpu.VMEM((de, d), BF16),
                pltpu.SemaphoreType.DMA((3,)),
            ],
        ),
        out_shape=jax.ShapeDtypeStruct(x_sorted.shape, U32),
        compiler_params=_params("arbitrary"),
        name="moe_experts",
    )(blk_e, blk_valid, blk_next, n_used, x_sorted, w_e_gate, w_e_up, w_e_down)


def _sc_mesh():
    return plsc.VectorSubcoreMesh(core_axis_name="c", subcore_axis_name="s")


def _sc_worker_id():
    return lax.axis_index("s") * SC_CORES + lax.axis_index("c")


def _sc_scatter_rows(segments, n_out):
    row_shape = segments[0][0].shape[1:]
    dtype = segments[0][0].dtype
    n_seg = len(segments)
    geom = [(tok0, dw.shape[0] // SC_WORKERS, dw.shape[2]) for _, tok0, dw in segments]
    scratch = []
    for _, n_win, win in geom:
        scratch += [pltpu.VMEM((n_win, TOP_K, win), I32), pltpu.VMEM((win,) + row_shape, dtype)]

    @functools.partial(
        pl.kernel, mesh=_sc_mesh(),
        out_type=jax.ShapeDtypeStruct((n_out,) + row_shape, dtype),
        scratch_types=scratch + [pltpu.SemaphoreType.DMA],
    )
    def scatter_kernel(*refs):
        out_hbm = refs[2 * n_seg]
        sem = refs[-1]
        for s, (tok0, n_win, win) in enumerate(geom):
            rows_hbm, idx_hbm = refs[2 * s], refs[2 * s + 1]
            idx_v, rows_v = refs[2 * n_seg + 1 + 2 * s], refs[2 * n_seg + 2 + 2 * s]
            first = _sc_worker_id() * n_win
            pltpu.sync_copy(idx_hbm.at[pl.ds(first, n_win)], idx_v)

            @pl.loop(0, n_win)
            def _(w):
                pltpu.sync_copy(rows_hbm.at[pl.ds(tok0 + (first + w) * win, win)], rows_v)
                copies = [pltpu.make_async_copy(rows_v, out_hbm.at[idx_v.at[w, k]], sem) for k in range(TOP_K)]
                for c in copies:
                    c.start()
                for c in copies:
                    c.wait()

    args = []
    for rows, _, dest_win in segments:
        args += [rows, dest_win]
    return scatter_kernel(*args)


def _sc_gather_rows(table, idx_wins):
    row_shape = table.shape[1:]
    n_seg = len(idx_wins)
    geom = [(iw.shape[0] // SC_WORKERS, iw.shape[1]) for iw in idx_wins]
    scratch = []
    for n_win, win in geom:
        scratch += [pltpu.VMEM((n_win, win), I32), pltpu.VMEM((win,) + row_shape, table.dtype)]

    @functools.partial(
        pl.kernel, mesh=_sc_mesh(),
        out_type=[jax.ShapeDtypeStruct((iw.shape[0] * iw.shape[1],) + row_shape, table.dtype) for iw in idx_wins],
        scratch_types=scratch + [pltpu.SemaphoreType.DMA],
    )
    def gather_kernel(*refs):
        table_hbm = refs[0]
        sem = refs[-1]
        for s, (n_win, win) in enumerate(geom):
            idx_hbm, out_hbm = refs[1 + s], refs[1 + n_seg + s]
            idx_v, rows_v = refs[1 + 2 * n_seg + 2 * s], refs[2 + 2 * n_seg + 2 * s]
            first = _sc_worker_id() * n_win
            pltpu.sync_copy(idx_hbm.at[pl.ds(first, n_win)], idx_v)

            @pl.loop(0, n_win)
            def _(w):
                pltpu.async_copy(table_hbm.at[idx_v.at[w]], rows_v, sem).wait()
                pltpu.sync_copy(rows_v, out_hbm.at[pl.ds((first + w) * win, win)])

    return gather_kernel(table, *idx_wins)


def _final_kernel(x1_ref, h2_ref, yt_ref, gate_ref, mod_ref, wsg_ref, wsu_ref, wsd_ref, gf_ref, *rest):
    o_ref = rest[-1]
    h2 = h2_ref[...]
    g = jnp.dot(h2, wsg_ref[...], preferred_element_type=F32)
    u = jnp.dot(h2, wsu_ref[...], preferred_element_type=F32)
    f = jnp.dot((_silu(g) * u).astype(BF16), wsd_ref[...], preferred_element_type=F32)
    gate = gate_ref[...]
    tl = h2.shape[0]
    for k in range(TOP_K):
        f = f + gate[:, k:k + 1] * _load_packed(yt_ref.at[k], tl)
    x2 = x1_ref[...] + mod_ref[N_MOD - 1:N_MOD, :] * f
    ms = jnp.mean(x2 * x2, axis=-1, keepdims=True)
    o_ref[...] = x2 * lax.rsqrt(ms + EPS) * gf_ref[...]


def _final(x1, h2, b0, y_tok, gate, mod3, mod_row0, w_s_gate, w_s_up, w_s_down, g_final, tl, y_prev=None):
    b, l, d = x1.shape
    nb = gate.shape[0]
    ds = w_s_gate.shape[1]
    tile = pl.BlockSpec((None, tl, d), lambda bi, li: (b0 + bi, li, 0))
    const = lambda shape: pl.BlockSpec(shape, lambda bi, li: (0,) * len(shape), pipeline_mode=pl.Buffered(1))
    in_specs = [
        tile,
        tile,
        pl.BlockSpec((TOP_K, None, tl * PACK_CHUNKS, LANES), lambda bi, li: (0, bi, li, 0)),
        pl.BlockSpec((None, tl, TOP_K), lambda bi, li: (bi, li, 0)),
        pl.BlockSpec((None, N_MOD, d), lambda bi, li: (mod_row0 + b0 + bi, 0, 0)),
        const((d, ds)),
        const((d, ds)),
        const((ds, d)),
        const((1, d)),
    ]
    args = [x1, h2, y_tok, gate, mod3, w_s_gate, w_s_up, w_s_down, g_final.reshape(1, d)]
    aliases = {}
    if y_prev is not None:
        in_specs.append(pl.BlockSpec(memory_space=pl.ANY))
        aliases = {len(args): 0}
        args.append(y_prev)
    return pl.pallas_call(
        _final_kernel,
        grid=(nb, l // tl),
        in_specs=in_specs,
        out_specs=tile,
        out_shape=jax.ShapeDtypeStruct((b, l, d), F32),
        input_output_aliases=aliases,
        compiler_params=_params("parallel", "parallel"),
        name="combine_final",
    )(*args)


def _group_mixer(x3, mod3, mod_row0, hist, start, past, wts, tiles):
    b, l, d = x3.shape
    tl = tiles["tl"]
    dp = wts["w_pool"].shape[0] * wts["w_pool"].shape[1]
    da = wts["w_up_b"].shape[0]
    nh = da // HEAD_DIM
    t = b * l

    assert dp == da, "u, q, k, v must each be one column tile of the input projection"
    scale = HEAD_DIM ** -0.5 * (LOG2_E if past is None else 1.0)
    u, qb, k, kb, v, vb, gates, logf_t = _in_proj(x3, wts["g_mix"], mod3, mod_row0, wts["w_fl_t"], wts["b_f"],
                                                  wts["w_cat"], scale, tiles["nb_in"], tl)

    y_a = _pool_mixer(u, hist, wts["w_pool"], wts["s_pool"], start, tl)

    if past is None:
        y_b = _attn_prompt(qb, kb, vb, _cumsum_last(logf_t), tiles["tq"])
    else:
        cache_k, cache_v, cache_logf = past
        p = cache_k.shape[1]
        pc = tiles["past_chunk"]
        assert l <= LANES and p % pc == 0 and p % LANES == 0
        lf_all = jnp.concatenate([jnp.swapaxes(cache_logf.astype(F32), 1, 2), logf_t,
                                  jnp.zeros((b, nh, LANES - l), F32)], axis=2)
        f_all = _cumsum_last(lf_all)
        f_past = f_all[:, :, :p].reshape(b, nh, p // pc, pc).transpose(0, 2, 1, 3)
        y_b = _attn_sample(qb, kb, vb, cache_k, cache_v, f_past, f_all[:, :, p:], pc)

    x1, h2, h2_packed = _mixer_out(y_a, y_b, gates, x3, mod3, mod_row0, wts["g_ffn"], wts["w_up_a"], wts["w_up_b"],
                                   wts["w_out"], tiles["tl_out"])
    caches = (u[:, l - POOL_HIST:, :], k.reshape(b, l, nh, HEAD_DIM), v.reshape(b, l, nh, HEAD_DIM),
              jnp.swapaxes(logf_t, 1, 2))
    return dict(x1=x1, h2=h2, h2_packed=h2_packed, mod_row0=mod_row0, tiles=tiles), caches


def _moe_chunk(segments, wts):
    n_exp = wts["w_router_t"].shape[0]
    row_shape = (PACK_CHUNKS, LANES)
    counts_f = jnp.zeros((n_exp, LANES), F32)
    routed = []
    for grp, b0, nb in segments:
        b, l, d = grp["h2"].shape
        ntok = nb * l
        idx_t, gate_t, rank_t, counts_f = _router(grp["h2"].reshape(b * l, d), b0 * l, ntok, wts["w_router_t"],
                                                  wts["b_router"], counts_f, min(grp["tiles"]["t_route"], ntok))
        routed.append((idx_t, gate_t, rank_t, ntok))
    counts = counts_f[:, 0].astype(I32)

    n_slots = sum(r[3] for r in routed) * TOP_K
    n_blocks = -(-n_slots // MOE_ROWS) + n_exp
    cap = n_blocks * MOE_ROWS
    padded = (counts + MOE_ROWS - 1) // MOE_ROWS * MOE_ROWS
    pad_end = jnp.cumsum(padded)
    pad_start = pad_end - padded
    blk_row0 = jnp.arange(n_blocks, dtype=I32) * MOE_ROWS
    blk_e = jnp.minimum(jnp.sum(pad_end[None, :] <= blk_row0[:, None], axis=1), n_exp - 1).astype(I32)
    blk_valid = jnp.clip(pad_start[blk_e] + counts[blk_e] - blk_row0, 0, MOE_ROWS).astype(I32)
    n_used = (pad_end[-1:] // MOE_ROWS).astype(I32)
    later = lax.cummin(jnp.where(counts > 0, jnp.arange(n_exp, dtype=I32), n_exp), axis=0, reverse=True)
    next_e = jnp.concatenate([later[1:], jnp.full((1,), n_exp, I32)])
    blk_next = jnp.where(next_e < n_exp, next_e, -1)[blk_e].astype(I32)

    scatter_segs, gather_idx = [], []
    for (grp, b0, nb), (idx_t, _, rank_t, ntok) in zip(segments, routed):
        b, l, _ = grp["h2"].shape
        dest = _slot_rows(pad_start.astype(I32), idx_t, rank_t)
        win_d = min(SC_WINDOW, ntok // SC_WORKERS)
        dest_win = dest.reshape(TOP_K, ntok // win_d, win_d).transpose(1, 0, 2)
        scatter_segs.append((grp["h2_packed"].reshape((b * l,) + row_shape), b0 * l, dest_win))
        win_c = min(SC_WINDOW, ntok * TOP_K // SC_WORKERS)
        gather_idx.append(dest.reshape(ntok * TOP_K // win_c, win_c))

    x_sorted = _sc_scatter_rows(scatter_segs, cap)
    y_sorted = _moe_experts(x_sorted.reshape(cap * PACK_CHUNKS, LANES), blk_e, blk_valid, blk_next, n_used,
                            wts["w_e_gate"], wts["w_e_up"], wts["w_e_down"])
    y_toks = _sc_gather_rows(y_sorted.reshape((cap,) + row_shape), gather_idx)
    out = []
    for (grp, b0, nb), (_, gate_t, _, _), y_tok in zip(segments, routed, y_toks):
        l = grp["h2"].shape[1]
        out.append((y_tok.reshape(TOP_K, nb, l * PACK_CHUNKS, LANES), gate_t.T.reshape(nb, l, TOP_K)))
    return out


def kernel(x_prompt, x_sample, cache_pool, cache_k, cache_v, cache_logf, c_prompt, c_sample, w_ada, b_ada, g_mix, w_in, b_f, w_pool, s_pool, w_up_a, w_up_b, w_out, g_ffn, w_router, b_router, w_e_gate, w_e_up, w_e_down, w_s_gate, w_s_up, w_s_down, g_final):
    depth = w_ada.shape[0]
    assert depth == 1, "a single layer is supported"
    bp, lp, d = x_prompt.shape
    bs, ls, _ = x_sample.shape
    dp = w_pool.shape[1] * w_pool.shape[2]
    da = w_up_b.shape[1]
    nh = da // HEAD_DIM
    assert bp <= MOD_ROWS_SAMPLE

    one = lambda a: a.reshape(a.shape[1:])
    c_all = jnp.zeros((MOD_ROWS_SAMPLE + bs, d), F32).at[:bp].set(c_prompt).at[MOD_ROWS_SAMPLE:].set(c_sample)
    mod3 = _ada_mod(c_all, one(w_ada), one(b_ada)).reshape(c_all.shape[0], N_MOD, d)

    n_main = dp + 3 * da
    w_in1 = one(w_in)
    wts = dict(
        g_mix=one(g_mix), g_ffn=one(g_ffn), b_f=one(b_f), s_pool=one(s_pool), b_router=one(b_router),
        w_cat=jnp.concatenate([w_in1[:, :n_main], w_in1[:, n_main + nh:]], axis=1).astype(BF16),
        w_fl_t=w_in1[:, n_main:n_main + nh].T.astype(BF16),
        w_pool=one(w_pool).astype(BF16),
        w_up_a=one(w_up_a).astype(BF16), w_up_b=one(w_up_b).astype(BF16), w_out=one(w_out).astype(BF16),
        w_router_t=one(w_router).T.astype(BF16),
        w_e_gate=one(w_e_gate), w_e_up=one(w_e_up), w_e_down=one(w_e_down),
        w_s_gate=one(w_s_gate).astype(BF16), w_s_up=one(w_s_up).astype(BF16), w_s_down=one(w_s_down).astype(BF16),
    )

    tiles_p = dict(tl=min(512, lp), tq=min(512, lp), t_route=min(512, bp * lp), tl_out=min(256, lp),
                   tl_final=min(256, lp), nb_in=1)
    tiles_s = dict(tl=ls, t_route=bs * ls, tl_out=ls, tl_final=ls, nb_in=bs, past_chunk=min(512, cache_k.shape[2]))

    hist_p = jnp.zeros((bp, HALO, dp), F32)
    grp_p, caches_p = _group_mixer(x_prompt, mod3, MOD_ROWS_PROMPT, hist_p, 0, None, wts, tiles_p)
    hist_s = jnp.pad(one(cache_pool), ((0, 0), (HALO - POOL_HIST, 0), (0, 0)))
    past = (one(cache_k), one(cache_v), one(cache_logf))
    grp_s, caches_s = _group_mixer(x_sample, mod3, MOD_ROWS_SAMPLE, hist_s, cache_k.shape[2], past, wts, tiles_s)

    n_chunks = MOE_CHUNKS if bp % MOE_CHUNKS == 0 else 1
    nb_c = bp // n_chunks
    chunks = [[(grp_p, c * nb_c, nb_c)] for c in range(n_chunks)]
    chunks[-1].append((grp_s, 0, bs))
    outs = {id(grp_p): None, id(grp_s): None}
    for segments in chunks:
        for (grp, b0, nb), (y_tok, gate) in zip(segments, _moe_chunk(segments, wts)):
            outs[id(grp)] = _final(grp["x1"], grp["h2"], b0, y_tok, gate, mod3, grp["mod_row0"], wts["w_s_gate"],
                                   wts["w_s_up"], wts["w_s_down"], g_final, grp["tiles"]["tl_final"],
                                   y_prev=outs[id(grp)])
    stack = lambda a: a[None]
    return (outs[id(grp_p)], outs[id(grp_s)], *map(stack, caches_p), *map(stack, caches_s))
```

```python
import functools

import jax
import jax.numpy as jnp
from jax import lax
from jax.experimental import pallas as pl
from jax.experimental.pallas import tpu as pltpu
from jax.experimental.pallas import tpu_sc as plsc

F32 = jnp.float32
BF16 = jnp.bfloat16
I32 = jnp.int32
U32 = jnp.uint32

EPS = 1e-6
N_MOD = 6
POOL_WINDOWS = (2, 4, 8, 16)
POOL_HIST = max(POOL_WINDOWS) - 1
HALO = 16
HEAD_DIM = 128
TOP_K = 8
ROUTED_SCALE = 2.5
MOE_ROWS = 512
MOE_CHAIN_ROWS = MOE_ROWS // 2
MOE_CHUNKS = 2
MOD_ROWS_PROMPT = 0
MOD_ROWS_SAMPLE = 8
VMEM_LIMIT_BYTES = 52 * 1024 * 1024
NEG_INF = float("-inf")
LOG2_E = 1.4426950408889634
LANES = 128
PACK_CHUNKS = 8
SC_CORES = 2
SC_SUBCORES = 16
SC_WORKERS = SC_CORES * SC_SUBCORES
SC_WINDOW = 32
NT_DIMS = (((1,), (1,)), ((), ()))


def _params(*semantics):
    return pltpu.CompilerParams(dimension_semantics=semantics, vmem_limit_bytes=VMEM_LIMIT_BYTES)


def _silu(x):
    return x * jax.nn.sigmoid(x)


def _store_packed(ref, x, row0=0):
    rows, width = x.shape
    half = width // 2
    bits = pltpu.bitcast(x.astype(BF16).astype(F32), U32)
    packed = (bits[:, :half] >> 16) | (bits[:, half:] & jnp.uint32(0xFFFF0000))
    for j in range(PACK_CHUNKS):
        ref[pl.ds(row0 * PACK_CHUNKS + j, rows, stride=PACK_CHUNKS), :] = packed[:, j * LANES:(j + 1) * LANES]


def _load_packed(ref, rows, valid=None, row0=0):
    lo, hi = [], []
    for j in range(PACK_CHUNKS):
        p = ref[pl.ds(row0 * PACK_CHUNKS + j, rows, stride=PACK_CHUNKS), :]
        if valid is not None:
            p = jnp.where(row0 + lax.broadcasted_iota(I32, p.shape, 0) < valid, p, jnp.uint32(0))
        lo.append(pltpu.bitcast(p << 16, F32))
        hi.append(pltpu.bitcast(p & jnp.uint32(0xFFFF0000), F32))
    return jnp.concatenate(lo + hi, axis=1)


def _ada_kernel(c_ref, w_ref, b_ref, o_ref):
    a = _silu(c_ref[...]).astype(BF16)
    o_ref[...] = jnp.dot(a, w_ref[...].astype(BF16), preferred_element_type=F32) + b_ref[...]


def _ada_mod(c_all, w_ada, b_ada):
    rows, d = c_all.shape
    n = w_ada.shape[1]
    tn = 1024
    return pl.pallas_call(
        _ada_kernel,
        grid=(n // tn,),
        in_specs=[
            pl.BlockSpec((rows, d), lambda j: (0, 0)),
            pl.BlockSpec((d, tn), lambda j: (0, j)),
            pl.BlockSpec((1, tn), lambda j: (0, j)),
        ],
        out_specs=pl.BlockSpec((rows, tn), lambda j: (0, j)),
        out_shape=jax.ShapeDtypeStruct((rows, n), F32),
        compiler_params=_params("parallel"),
        name="ada_mod",
    )(c_all, w_ada, b_ada.reshape(1, n))


def _modulated_norm(x, g, m, shift_row, scale_row):
    ms = jnp.mean(x * x, axis=-1, keepdims=True)
    return x * lax.rsqrt(ms + EPS) * g * (1.0 + m[scale_row:scale_row + 1, :]) + m[shift_row:shift_row + 1, :]


def _in_proj_kernel(x_ref, g_ref, mod_ref, wfl_ref, bf_ref, w_ref,
                    u_ref, q_ref, k32_ref, kb_ref, v32_ref, vb_ref, gates_ref, lf_ref, h_ref, *, q_scale):
    j = pl.program_id(2)
    nb, tl, _ = x_ref.shape
    nh = wfl_ref.shape[0]

    @pl.when(j == 0)
    def _():
        for bi in range(nb):
            hb = _modulated_norm(x_ref[bi], g_ref[...], mod_ref[bi], 0, 1).astype(BF16)
            h_ref[bi * tl:(bi + 1) * tl, :] = hb
            fl = lax.dot_general(wfl_ref[...], hb, NT_DIMS, preferred_element_type=F32) + bf_ref[...]
            lf_ref[bi] = jnp.minimum(fl, 0.0) - jnp.log(1.0 + jnp.exp(-jnp.abs(fl)))

    def tile_rows():
        acc = jnp.dot(h_ref[...], w_ref[...], preferred_element_type=F32)
        return [acc[bi * tl:(bi + 1) * tl, :] for bi in range(nb)]

    def store_heads(f32_ref, bf_ref_):
        for bi, r in enumerate(tile_rows()):
            bf_ref_[bi] = r.astype(BF16)
            for hd in range(nh):
                f32_ref[bi, pl.ds(hd, tl, stride=nh), :] = r[:, hd * HEAD_DIM:(hd + 1) * HEAD_DIM]

    @pl.when(j == 0)
    def _():
        for bi, r in enumerate(tile_rows()):
            u_ref[bi] = r

    @pl.when(j == 1)
    def _():
        for bi, r in enumerate(tile_rows()):
            q_ref[bi] = (r * q_scale).astype(BF16)

    @pl.when(j == 2)
    def _():
        store_heads(k32_ref, kb_ref)

    @pl.when(j == 3)
    def _():
        store_heads(v32_ref, vb_ref)

    @pl.when(j >= 4)
    def _():
        for bi, r in enumerate(tile_rows()):
            gates_ref[bi] = jax.nn.sigmoid(r).astype(BF16)


def _in_proj(x3, g, mod3, mod_row0, w_fl_t, b_f, w_cat, q_scale, nb, tl):
    b, l, d = x3.shape
    nh = w_fl_t.shape[0]
    tn = nh * HEAD_DIM
    n_tiles = w_cat.shape[1] // tn
    n_gate_tiles = n_tiles - 4
    assert mod_row0 % nb == 0 and b % nb == 0
    tile = lambda width: pl.BlockSpec((nb, tl, width), lambda bb, li, j: (bb, li, 0))
    heads = pl.BlockSpec((nb, tl * nh, HEAD_DIM), lambda bb, li, j: (bb, li, 0))
    return pl.pallas_call(
        functools.partial(_in_proj_kernel, q_scale=q_scale),
        grid=(b // nb, l // tl, n_tiles),
        in_specs=[
            tile(d),
            pl.BlockSpec((1, d), lambda bb, li, j: (0, 0)),
            pl.BlockSpec((nb, N_MOD, d), lambda bb, li, j: (mod_row0 // nb + bb, 0, 0)),
            pl.BlockSpec((nh, d), lambda bb, li, j: (0, 0)),
            pl.BlockSpec((nh, 1), lambda bb, li, j: (0, 0)),
            pl.BlockSpec((d, tn), lambda bb, li, j: (0, j)),
        ],
        out_specs=[
            tile(tn), tile(tn), heads, tile(tn), heads, tile(tn),
            pl.BlockSpec((nb, tl, tn), lambda bb, li, j: (bb, li, jnp.maximum(j - 4, 0))),
            pl.BlockSpec((nb, nh, tl), lambda bb, li, j: (bb, 0, li)),
        ],
        out_shape=[
            jax.ShapeDtypeStruct((b, l, tn), F32),
            jax.ShapeDtypeStruct((b, l, tn), BF16),
            jax.ShapeDtypeStruct((b, l * nh, HEAD_DIM), F32),
            jax.ShapeDtypeStruct((b, l, tn), BF16),
            jax.ShapeDtypeStruct((b, l * nh, HEAD_DIM), F32),
            jax.ShapeDtypeStruct((b, l, tn), BF16),
            jax.ShapeDtypeStruct((b, l, n_gate_tiles * tn), BF16),
            jax.ShapeDtypeStruct((b, nh, l), F32),
        ],
        scratch_shapes=[pltpu.VMEM((nb * tl, d), BF16)],
        compiler_params=_params("parallel", "parallel", "arbitrary"),
        name="in_proj",
    )(x3, g.reshape(1, d), mod3, w_fl_t, b_f.reshape(nh, 1), w_cat)


def _mixer_out_kernel(ya_ref, yb_ref, ga_ref, gb_ref, x_ref, mod_ref, g_ref, wa_ref, wb_ref, wo_ref,
                      x1_ref, h2_ref, h2p_ref):
    merged = ga_ref[...].astype(F32) * jnp.dot(ya_ref[...], wa_ref[...], preferred_element_type=F32)
    merged = merged + gb_ref[...].astype(F32) * jnp.dot(yb_ref[...], wb_ref[...], preferred_element_type=F32)
    m = mod_ref[...]
    x1 = x_ref[...] + m[2:3, :] * jnp.dot(merged.astype(BF16), wo_ref[...], preferred_element_type=F32)
    x1_ref[...] = x1
    h2 = _modulated_norm(x1, g_ref[...], m, 3, 4)
    h2_ref[...] = h2.astype(BF16)
    _store_packed(h2p_ref, h2)


def _mixer_out(y_a, y_b, gates, x3, mod3, mod_row0, g_ffn, w_up_a, w_up_b, w_out, tl):
    b, l, d = x3.shape
    n_l = l // tl
    dp, da = y_a.shape[2], y_b.shape[2]
    tile = lambda width, col=0: pl.BlockSpec((None, tl, width), lambda bi, li: (bi, li, col))
    resident = lambda shape: pl.BlockSpec(shape, lambda bi, li: (0,) * len(shape), pipeline_mode=pl.Buffered(1))
    return pl.pallas_call(
        _mixer_out_kernel,
        grid=(b, n_l),
        in_specs=[
            tile(dp), tile(da), tile(d, 0), tile(d, 1), tile(d),
            pl.BlockSpec((None, N_MOD, d), lambda bi, li: (mod_row0 + bi, 0, 0)),
            pl.BlockSpec((1, d), lambda bi, li: (0, 0)),
            resident((dp, d)), resident((da, d)), resident((d, d)),
        ],
        out_specs=[
            tile(d), tile(d),
            pl.BlockSpec((tl * PACK_CHUNKS, LANES), lambda bi, li: (bi * n_l + li, 0)),
        ],
        out_shape=[
            jax.ShapeDtypeStruct((b, l, d), F32),
            jax.ShapeDtypeStruct((b, l, d), BF16),
            jax.ShapeDtypeStruct((b * l * PACK_CHUNKS, LANES), U32),
        ],
        compiler_params=_params("parallel", "parallel"),
        name="mixer_out",
    )(y_a, y_b, gates, gates, x3, mod3, g_ffn.reshape(1, d), w_up_a, w_up_b, w_out)


def _pool_kernel(u_ref, halo_ref, hist_ref, wp_ref, sp_ref, o_ref, full_ref, *, tl, start, gw):
    i = pl.program_id(1)
    full_ref[0:HALO, :] = jnp.where(i == 0, hist_ref[...], halo_ref[...])
    full_ref[HALO:HALO + tl, :] = u_ref[...]
    pos = start + i * tl + lax.broadcasted_iota(I32, (tl, 1), 0)
    for g, w in enumerate(POOL_WINDOWS):
        c0, c1 = g * gw, (g + 1) * gw
        cur = full_ref[HALO:HALO + tl, c0:c1]
        win = cur
        for d in range(1, w):
            win = win + full_ref[HALO - d:HALO - d + tl, c0:c1]
        cnt = jnp.minimum(pos + 1, w).astype(F32)
        dmean = win / cnt - cur
        y = jnp.dot(dmean.astype(BF16), wp_ref[g], preferred_element_type=F32) * sp_ref[:, c0:c1]
        o_ref[:, c0:c1] = y.astype(o_ref.dtype)


def _pool_mixer(u3, hist, w_pool, s_pool, start, tl):
    b, l, dp = u3.shape
    gw = dp // len(POOL_WINDOWS)
    halo_blocks = tl // HALO
    return pl.pallas_call(
        functools.partial(_pool_kernel, tl=tl, start=start, gw=gw),
        grid=(b, l // tl),
        in_specs=[
            pl.BlockSpec((None, tl, dp), lambda bi, li: (bi, li, 0)),
            pl.BlockSpec((None, HALO, dp), lambda bi, li: (bi, jnp.maximum(li * halo_blocks - 1, 0), 0)),
            pl.BlockSpec((None, HALO, dp), lambda bi, li: (bi, 0, 0)),
            pl.BlockSpec((len(POOL_WINDOWS), gw, gw), lambda bi, li: (0, 0, 0)),
            pl.BlockSpec((1, dp), lambda bi, li: (0, 0)),
        ],
        out_specs=pl.BlockSpec((None, tl, dp), lambda bi, li: (bi, li, 0)),
        out_shape=jax.ShapeDtypeStruct((b, l, dp), BF16),
        scratch_shapes=[pltpu.VMEM((HALO + tl, dp), F32)],
        compiler_params=_params("parallel", "parallel"),
        name="pool_mixer",
    )(u3, u3, hist, w_pool, s_pool.reshape(1, dp))


def _cumsum_kernel(x_ref, o_ref):
    x = x_ref[...]
    n = x.shape[1]
    idx = lax.broadcasted_iota(I32, x.shape, 1)
    s = 1
    while s < n:
        x = x + jnp.where(idx >= s, pltpu.roll(x, s, 1), 0.0)
        s *= 2
    o_ref[...] = x


def _cumsum_last(x3):
    b, h, n = x3.shape
    return pl.pallas_call(
        _cumsum_kernel,
        grid=(b,),
        in_specs=[pl.BlockSpec((None, h, n), lambda bi: (bi, 0, 0))],
        out_specs=pl.BlockSpec((None, h, n), lambda bi: (bi, 0, 0)),
        out_shape=jax.ShapeDtypeStruct((b, h, n), F32),
        compiler_params=_params("parallel"),
        name="cumsum_logf",
    )(x3)


def _attn_prompt_kernel(q_ref, k_ref, v_ref, fk_ref, o_ref, s0_ref, s1_ref, m_ref, l_ref, acc_ref, *, tq):
    qi = pl.program_id(2)
    q = q_ref[...]
    n_chunks = tq // LANES

    def produce(j, s_ref):
        off = pl.multiple_of(j * tq, tq)
        s = lax.dot_general(q, k_ref[pl.ds(off, tq), :], NT_DIMS, preferred_element_type=F32)
        s_ref[...] = s - fk_ref[j] * LOG2_E

    def consume(j, s_ref, diagonal):
        s = s_ref[...]
        if diagonal:
            row = lax.broadcasted_iota(I32, (tq, tq), 0)
            col = lax.broadcasted_iota(I32, (tq, tq), 1)
            s = jnp.where(col <= row, s, NEG_INF)
        chunks = [s[:, c * LANES:(c + 1) * LANES] for c in range(n_chunks)]
        m_old = m_ref[...]
        m_new = jnp.maximum(m_old, jnp.max(functools.reduce(jnp.maximum, chunks), axis=1, keepdims=True))
        alpha = jnp.exp2(m_old - m_new)
        p_chunks = [jnp.exp2(ch - m_new) for ch in chunks]
        l_ref[...] = alpha * l_ref[...] + functools.reduce(lambda a, c: a + c, p_chunks)
        p = jnp.concatenate(p_chunks, axis=1).astype(BF16)
        v = v_ref[pl.ds(pl.multiple_of(j * tq, tq), tq), :]
        acc_ref[...] = alpha * acc_ref[...] + jnp.dot(p, v, preferred_element_type=F32)
        m_ref[...] = m_new

    m_ref[...] = jnp.full(m_ref.shape, NEG_INF, F32)
    l_ref[...] = jnp.zeros(l_ref.shape, F32)
    acc_ref[...] = jnp.zeros(acc_ref.shape, F32)
    produce(0, s0_ref)

    def pair(jj, carry):
        j0 = 2 * jj
        produce(j0 + 1, s1_ref)
        consume(j0, s0_ref, False)
        produce(j0 + 2, s0_ref)
        consume(j0 + 1, s1_ref, False)
        return carry

    lax.fori_loop(0, qi // 2, pair, 0)

    @pl.when(qi % 2 == 1)
    def _():
        produce(qi, s1_ref)
        consume(qi - 1, s0_ref, False)
        consume(qi, s1_ref, True)

    @pl.when(qi % 2 == 0)
    def _():
        consume(qi, s0_ref, True)

    l = jnp.sum(l_ref[...], axis=1, keepdims=True)
    o_ref[...] = (acc_ref[...] / l).astype(o_ref.dtype)


def _attn_prompt(q3, k3, v3, f3, tq):
    b, l, da = q3.shape
    nh = da // HEAD_DIM
    nk = l // tq
    fk = f3.reshape(b * nh, nk, 1, tq)
    return pl.pallas_call(
        functools.partial(_attn_prompt_kernel, tq=tq),
        grid=(b, nh, nk),
        in_specs=[
            pl.BlockSpec((None, tq, HEAD_DIM), lambda bi, hi, qi: (bi, qi, hi)),
            pl.BlockSpec((None, l, HEAD_DIM), lambda bi, hi, qi: (bi, 0, hi)),
            pl.BlockSpec((None, l, HEAD_DIM), lambda bi, hi, qi: (bi, 0, hi)),
            pl.BlockSpec((None, nk, 1, tq), lambda bi, hi, qi: (bi * nh + hi, 0, 0, 0)),
        ],
        out_specs=pl.BlockSpec((None, tq, HEAD_DIM), lambda bi, hi, qi: (bi, qi, hi)),
        out_shape=jax.ShapeDtypeStruct((b, l, da), BF16),
        scratch_shapes=[
            pltpu.VMEM((tq, tq), F32),
            pltpu.VMEM((tq, tq), F32),
            pltpu.VMEM((tq, LANES), F32),
            pltpu.VMEM((tq, LANES), F32),
            pltpu.VMEM((tq, HEAD_DIM), F32),
        ],
        compiler_params=_params("parallel", "parallel", "parallel"),
        name="fox_prompt",
    )(q3, k3, v3, fk)


def _attn_sample_kernel(q_ref, kp_ref, vp_ref, kn_ref, vn_ref, fp_ref, fn_ref, o_ref, m_ref, l_ref, acc_ref, *,
                        nh, pc, lq):
    c = pl.program_id(1)

    @pl.when(c == 0)
    def _():
        m_ref[...] = jnp.full(m_ref.shape, NEG_INF, F32)
        l_ref[...] = jnp.zeros(l_ref.shape, F32)
        acc_ref[...] = jnp.zeros(acc_ref.shape, F32)

    for h in range(nh):
        cols = slice(h * HEAD_DIM, (h + 1) * HEAD_DIM)
        q = q_ref[:, cols]
        k = kp_ref[pl.ds(h, pc, stride=nh), :].astype(BF16)
        v = vp_ref[pl.ds(h, pc, stride=nh), :].astype(BF16)
        s = lax.dot_general(q, k, NT_DIMS, preferred_element_type=F32) - fp_ref[h:h + 1, :]
        chunks = [s[:, j * LANES:(j + 1) * LANES] for j in range(pc // LANES)]
        m_old = m_ref[h]
        m_new = jnp.maximum(m_old, jnp.max(functools.reduce(jnp.maximum, chunks), axis=1, keepdims=True))
        alpha = jnp.exp(m_old - m_new)
        p_chunks = [jnp.exp(ch - m_new) for ch in chunks]
        l_ref[h] = alpha * l_ref[h] + functools.reduce(lambda a, b: a + b, p_chunks)
        p = jnp.concatenate(p_chunks, axis=1).astype(BF16)
        acc_ref[h] = alpha * acc_ref[h] + jnp.dot(p, v, preferred_element_type=F32)
        m_ref[h] = m_new

    @pl.when(c == pl.num_programs(1) - 1)
    def _():
        row = lax.broadcasted_iota(I32, (lq, lq), 0)
        col = lax.broadcasted_iota(I32, (lq, lq), 1)
        for h in range(nh):
            cols = slice(h * HEAD_DIM, (h + 1) * HEAD_DIM)
            s = lax.dot_general(q_ref[:, cols], kn_ref[:, cols], NT_DIMS, preferred_element_type=F32)
            s = jnp.where(col <= row, s - fn_ref[h:h + 1, :lq], NEG_INF)
            m_old = m_ref[h]
            m_new = jnp.maximum(m_old, jnp.max(s, axis=1, keepdims=True))
            alpha = jnp.exp(m_old - m_new)
            p = jnp.exp(s - m_new[:, :lq])
            l = jnp.sum(alpha * l_ref[h], axis=1, keepdims=True) + jnp.sum(p, axis=1, keepdims=True)
            acc = alpha * acc_ref[h] + jnp.dot(p.astype(BF16), vn_ref[:, cols], preferred_element_type=F32)
            o_ref[:, cols] = (acc / l).astype(o_ref.dtype)


def _attn_sample(q3, kn3, vn3, cache_k, cache_v, f_past, f_new, pc):
    b, lq, da = q3.shape
    _, past, nh, dh = cache_k.shape
    nc = past // pc
    new_spec = pl.BlockSpec((None, lq, da), lambda bi, ci: (bi, 0, 0))
    past_spec = pl.BlockSpec((None, pc * nh, dh), lambda bi, ci: (bi, ci, 0))
    return pl.pallas_call(
        functools.partial(_attn_sample_kernel, nh=nh, pc=pc, lq=lq),
        grid=(b, nc),
        in_specs=[new_spec, past_spec, past_spec, new_spec, new_spec,
                  pl.BlockSpec((None, None, nh, pc), lambda bi, ci: (bi, ci, 0, 0)),
                  pl.BlockSpec((None, nh, LANES), lambda bi, ci: (bi, 0, 0))],
        out_specs=new_spec,
        out_shape=jax.ShapeDtypeStruct((b, lq, da), BF16),
        scratch_shapes=[pltpu.VMEM((nh, lq, LANES), F32), pltpu.VMEM((nh, lq, LANES), F32),
                        pltpu.VMEM((nh, lq, HEAD_DIM), F32)],
        compiler_params=_params("parallel", "arbitrary"),
        name="fox_sample",
    )(q3, cache_k.reshape(b, past * nh, dh), cache_v.reshape(b, past * nh, dh), kn3, vn3, f_past, f_new)


def _router_kernel(h_ref, wr_ref, br_ref, tri_ref, cnt0_ref, idx_ref, gate_ref, rank_ref, cnt_ref, carry_ref):
    @pl.when(pl.program_id(0) == 0)
    def _():
        carry_ref[...] = cnt0_ref[...]

    n_exp, tm = wr_ref.shape[0], h_ref.shape[0]
    scores = jax.nn.sigmoid(lax.dot_general(wr_ref[...], h_ref[...], NT_DIMS, preferred_element_type=F32))
    work = scores + br_ref[...]
    eidx = lax.broadcasted_iota(I32, (n_exp, tm), 0)
    picked = jnp.zeros((n_exp, tm), F32)
    onehots, gates = [], []
    for k in range(TOP_K):
        best = jnp.max(work, axis=0, keepdims=True)
        first = jnp.min(jnp.where(work == best, eidx, n_exp), axis=0, keepdims=True)
        onehot = eidx == first
        idx_ref[k:k + 1, :] = first
        gates.append(jnp.sum(jnp.where(onehot, scores, 0.0), axis=0, keepdims=True))
        onehots.append(onehot)
        picked = picked + jnp.where(onehot, 1.0, 0.0)
        work = jnp.where(onehot, NEG_INF, work)
    norm = ROUTED_SCALE / functools.reduce(lambda a, c: a + c, gates)
    cum = jnp.dot(picked.astype(BF16), tri_ref[...], preferred_element_type=F32)
    before = cum - picked + carry_ref[:, 0:1]
    for k in range(TOP_K):
        gate_ref[k:k + 1, :] = gates[k] * norm
        rank_ref[k:k + 1, :] = jnp.sum(jnp.where(onehots[k], before, 0.0), axis=0, keepdims=True).astype(I32)
    carry_ref[...] = carry_ref[...] + cum[:, tm - 1:tm]
    cnt_ref[...] = carry_ref[...]


def _router(h2, tok0, ntok, w_router_t, b_router, cnt0, tm):
    d = h2.shape[1]
    n_exp = w_router_t.shape[0]
    tri = jnp.triu(jnp.ones((tm, tm), BF16))
    tok_spec = pl.BlockSpec((TOP_K, tm), lambda i: (0, i))
    cnt_spec = pl.BlockSpec((n_exp, LANES), lambda i: (0, 0))
    return pl.pallas_call(
        _router_kernel,
        grid=(ntok // tm,),
        in_specs=[
            pl.BlockSpec((tm, d), lambda i: (tok0 // tm + i, 0)),
            pl.BlockSpec((n_exp, d), lambda i: (0, 0)),
            pl.BlockSpec((n_exp, 1), lambda i: (0, 0)),
            pl.BlockSpec((tm, tm), lambda i: (0, 0)),
            cnt_spec,
        ],
        out_specs=[tok_spec, tok_spec, tok_spec, cnt_spec],
        out_shape=[
            jax.ShapeDtypeStruct((TOP_K, ntok), I32),
            jax.ShapeDtypeStruct((TOP_K, ntok), F32),
            jax.ShapeDtypeStruct((TOP_K, ntok), I32),
            jax.ShapeDtypeStruct((n_exp, LANES), F32),
        ],
        scratch_shapes=[pltpu.VMEM((n_exp, LANES), F32)],
        compiler_params=_params("arbitrary"),
        name="moe_router",
    )(h2, w_router_t, b_router.reshape(n_exp, 1), tri, cnt0)


LAYOUT_ROWS = 8


def _layout_kernel(cnt_ref, o_ref):
    n_exp = cnt_ref.shape[0]
    lane = lax.broadcasted_iota(I32, o_ref.shape, 1)
    row0 = lane * MOE_ROWS

    def forward(e, carry):
        start, blk_e, valid, pad_start = carry
        cnt = cnt_ref[e]
        end = start + (cnt + MOE_ROWS - 1) // MOE_ROWS * MOE_ROWS
        in_span = lambda new, old: jnp.where(row0 >= start, jnp.where(row0 < end, new, old), old)
        blk_e = in_span(e, blk_e)
        valid = in_span(jnp.clip(start + cnt - row0, 0, MOE_ROWS), valid)
        return end, blk_e, valid, jnp.where(lane == e, start, pad_start)

    zeros = jnp.zeros(o_ref.shape, I32)
    end, blk_e, valid, pad_start = lax.fori_loop(0, n_exp, forward, (jnp.int32(0), zeros + (n_exp - 1), zeros, zeros))

    def backward(t, carry):
        nxt, blk_next = carry
        e = n_exp - 1 - t
        return jnp.where(cnt_ref[e] > 0, e, nxt), jnp.where(blk_e == e, nxt, blk_next)

    _, blk_next = lax.fori_loop(0, n_exp, backward, (jnp.int32(-1), zeros - 1))
    table = zeros + end // MOE_ROWS
    sub = lax.broadcasted_iota(I32, o_ref.shape, 0)
    for r, row in enumerate((blk_e, valid, blk_next, pad_start)):
        table = jnp.where(sub == r, row, table)
    o_ref[...] = table


def _layout(counts, n_blocks):
    n_exp = counts.shape[0]
    w = -(-max(n_blocks, n_exp) // LANES) * LANES
    table = pl.pallas_call(
        _layout_kernel,
        grid_spec=pltpu.PrefetchScalarGridSpec(
            num_scalar_prefetch=1, grid=(1,), in_specs=[],
            out_specs=pl.BlockSpec((LAYOUT_ROWS, w), lambda i, cnt: (0, 0))),
        out_shape=jax.ShapeDtypeStruct((LAYOUT_ROWS, w), I32),
        compiler_params=_params("arbitrary"),
        name="moe_layout",
    )(counts)
    return table[0, :n_blocks], table[1, :n_blocks], table[2, :n_blocks], table[3, :n_exp], table[4, :1]


def _slot_rows_kernel(ps_ref, idx_ref, rank_ref, o_ref):
    idx = idx_ref[...]
    rows = rank_ref[...]
    for e in range(ps_ref.shape[0]):
        rows = rows + jnp.where(idx == e, ps_ref[e], 0)
    o_ref[...] = rows


def _slot_rows(pad_start, idx_t, rank_t):
    k, ntok = idx_t.shape
    tm = min(2048, ntok)
    spec = pl.BlockSpec((k, tm), lambda i, ps: (0, i))
    return pl.pallas_call(
        _slot_rows_kernel,
        grid_spec=pltpu.PrefetchScalarGridSpec(num_scalar_prefetch=1, grid=(ntok // tm,), in_specs=[spec, spec],
                                               out_specs=spec),
        out_shape=jax.ShapeDtypeStruct((k, ntok), I32),
        compiler_params=_params("parallel"),
        name="moe_slot_rows",
    )(pad_start, idx_t, rank_t)


def _moe_kernel(be_ref, bv_ref, nx_ref, nu_ref, x_ref, wg_hbm, wu_hbm, wd_hbm, y_ref,
                wg_f32, wu_f32, wd_f32, wg_bf, wu_bf, wd_bf, sems):
    i = pl.program_id(0)
    active = i < nu_ref[0]
    expert = be_ref[i]

    def fetch(e):
        pairs = ((wg_hbm, wg_f32), (wu_hbm, wu_f32), (wd_hbm, wd_f32))
        return [pltpu.make_async_copy(w.at[e], buf, sems.at[n]) for n, (w, buf) in enumerate(pairs)]

    @pl.when(active & (i == 0))
    def _():
        for c in fetch(expert):
            c.start()

    @pl.when(active & ((i == 0) | (expert != be_ref[jnp.maximum(i - 1, 0)])))
    def _():
        for c in fetch(expert):
            c.wait()
        wg_bf[...] = wg_f32[...].astype(BF16)
        wu_bf[...] = wu_f32[...].astype(BF16)
        wd_bf[...] = wd_f32[...].astype(BF16)

        @pl.when(nx_ref[i] >= 0)
        def _():
            for c in fetch(nx_ref[i]):
                c.start()

    def chain(row0):
        x = _load_packed(x_ref, MOE_CHAIN_ROWS, valid=bv_ref[i], row0=row0).astype(BF16)
        g = jnp.dot(x, wg_bf[...], preferred_element_type=F32)
        u = jnp.dot(x, wu_bf[...], preferred_element_type=F32)
        a = (_silu(g) * u).astype(BF16)
        _store_packed(y_ref, jnp.dot(a, wd_bf[...], preferred_element_type=F32), row0=row0)

    both = bv_ref[i] > MOE_CHAIN_ROWS

    @pl.when(active & both)
    def _():
        chain(0)
        chain(MOE_CHAIN_ROWS)

    @pl.when(active & jnp.logical_not(both))
    def _():
        chain(0)


def _moe_experts(x_sorted, blk_e, blk_valid, blk_next, n_used, w_e_gate, w_e_up, w_e_down):
    cap = x_sorted.shape[0] // PACK_CHUNKS
    d, de = w_e_gate.shape[1:]
    n_blocks = cap // MOE_ROWS
    row_map = lambda i, be, bv, nx, nu: (jnp.minimum(i, nu[0] - 1), 0)
    hbm = pl.BlockSpec(memory_space=pl.ANY)
    return pl.pallas_call(
        _moe_kernel,
        grid_spec=pltpu.PrefetchScalarGridSpec(
            num_scalar_prefetch=4,
            grid=(n_blocks,),
            in_specs=[pl.BlockSpec((MOE_ROWS * PACK_CHUNKS, LANES), row_map), hbm, hbm, hbm],
            out_specs=pl.BlockSpec((MOE_ROWS * PACK_CHUNKS, LANES), row_map),
            scratch_shapes=[
                pltpu.VMEM((d, de), F32), pltpu.VMEM((d, de), F32), pltpu.VMEM((de, d), F32),
                pltpu.VMEM((d, de), BF16), pltpu.VMEM((d, de), BF16), pltpu.VMEM((de, d), BF16),
                pltpu.SemaphoreType.DMA((3,)),
            ],
        ),
        out_shape=jax.ShapeDtypeStruct(x_sorted.shape, U32),
        compiler_params=_params("arbitrary"),
        name="moe_experts",
    )(blk_e, blk_valid, blk_next, n_used, x_sorted, w_e_gate, w_e_up, w_e_down)


def _sc_mesh():
    return plsc.VectorSubcoreMesh(core_axis_name="c", subcore_axis_name="s")


def _sc_worker_id():
    return lax.axis_index("s") * SC_CORES + lax.axis_index("c")


def _sc_scatter_rows(segments, n_out):
    row_shape = segments[0][0].shape[1:]
    dtype = segments[0][0].dtype
    n_seg = len(segments)
    geom = [(tok0, dw.shape[0] // SC_WORKERS, dw.shape[2]) for _, tok0, dw in segments]
    scratch = []
    for _, n_win, win in geom:
        scratch += [pltpu.VMEM((n_win, TOP_K, win), I32), pltpu.VMEM((win,) + row_shape, dtype)]

    @functools.partial(
        pl.kernel, mesh=_sc_mesh(),
        out_type=jax.ShapeDtypeStruct((n_out,) + row_shape, dtype),
        scratch_types=scratch + [pltpu.SemaphoreType.DMA],
    )
    def scatter_kernel(*refs):
        out_hbm = refs[2 * n_seg]
        sem = refs[-1]
        for s, (tok0, n_win, win) in enumerate(geom):
            rows_hbm, idx_hbm = refs[2 * s], refs[2 * s + 1]
            idx_v, rows_v = refs[2 * n_seg + 1 + 2 * s], refs[2 * n_seg + 2 + 2 * s]
            first = _sc_worker_id() * n_win
            pltpu.sync_copy(idx_hbm.at[pl.ds(first, n_win)], idx_v)

            @pl.loop(0, n_win)
            def _(w):
                pltpu.sync_copy(rows_hbm.at[pl.ds(tok0 + (first + w) * win, win)], rows_v)
                copies = [pltpu.make_async_copy(rows_v, out_hbm.at[idx_v.at[w, k]], sem) for k in range(TOP_K)]
                for c in copies:
                    c.start()
                for c in copies:
                    c.wait()

    args = []
    for rows, _, dest_win in segments:
        args += [rows, dest_win]
    return scatter_kernel(*args)


def _sc_gather_rows(table, idx_wins):
    row_shape = table.shape[1:]
    n_seg = len(idx_wins)
    geom = [(iw.shape[0] // SC_WORKERS, iw.shape[1]) for iw in idx_wins]
    scratch = []
    for n_win, win in geom:
        scratch += [pltpu.VMEM((n_win, win), I32), pltpu.VMEM((win,) + row_shape, table.dtype)]

    @functools.partial(
        pl.kernel, mesh=_sc_mesh(),
        out_type=[jax.ShapeDtypeStruct((iw.shape[0] * iw.shape[1],) + row_shape, table.dtype) for iw in idx_wins],
        scratch_types=scratch + [pltpu.SemaphoreType.DMA],
    )
    def gather_kernel(*refs):
        table_hbm = refs[0]
        sem = refs[-1]
        for s, (n_win, win) in enumerate(geom):
            idx_hbm, out_hbm = refs[1 + s], refs[1 + n_seg + s]
            idx_v, rows_v = refs[1 + 2 * n_seg + 2 * s], refs[2 + 2 * n_seg + 2 * s]
            first = _sc_worker_id() * n_win
            pltpu.sync_copy(idx_hbm.at[pl.ds(first, n_win)], idx_v)

            @pl.loop(0, n_win)
            def _(w):
                pltpu.async_copy(table_hbm.at[idx_v.at[w]], rows_v, sem).wait()
                pltpu.sync_copy(rows_v, out_hbm.at[pl.ds((first + w) * win, win)])

    return gather_kernel(table, *idx_wins)


def _final_kernel(x1_ref, h2_ref, yt_ref, gate_ref, mod_ref, wsg_ref, wsu_ref, wsd_ref, gf_ref, *rest):
    o_ref = rest[-1]
    h2 = h2_ref[...]
    g = jnp.dot(h2, wsg_ref[...], preferred_element_type=F32)
    u = jnp.dot(h2, wsu_ref[...], preferred_element_type=F32)
    f = jnp.dot((_silu(g) * u).astype(BF16), wsd_ref[...], preferred_element_type=F32)
    gate = gate_ref[...]
    tl = h2.shape[0]
    for k in range(TOP_K):
        f = f + gate[:, k:k + 1] * _load_packed(yt_ref.at[k], tl)
    x2 = x1_ref[...] + mod_ref[N_MOD - 1:N_MOD, :] * f
    ms = jnp.mean(x2 * x2, axis=-1, keepdims=True)
    o_ref[...] = x2 * lax.rsqrt(ms + EPS) * gf_ref[...]


def _final(x1, h2, b0, y_tok, gate, mod3, mod_row0, w_s_gate, w_s_up, w_s_down, g_final, tl, y_prev=None):
    b, l, d = x1.shape
    nb = gate.shape[0]
    ds = w_s_gate.shape[1]
    tile = pl.BlockSpec((None, tl, d), lambda bi, li: (b0 + bi, li, 0))
    const = lambda shape: pl.BlockSpec(shape, lambda bi, li: (0,) * len(shape), pipeline_mode=pl.Buffered(1))
    in_specs = [
        tile,
        tile,
        pl.BlockSpec((TOP_K, None, tl * PACK_CHUNKS, LANES), lambda bi, li: (0, bi, li, 0)),
        pl.BlockSpec((None, tl, TOP_K), lambda bi, li: (bi, li, 0)),
        pl.BlockSpec((None, N_MOD, d), lambda bi, li: (mod_row0 + b0 + bi, 0, 0)),
        const((d, ds)),
        const((d, ds)),
        const((ds, d)),
        const((1, d)),
    ]
    args = [x1, h2, y_tok, gate, mod3, w_s_gate, w_s_up, w_s_down, g_final.reshape(1, d)]
    aliases = {}
    if y_prev is not None:
        in_specs.append(pl.BlockSpec(memory_space=pl.ANY))
        aliases = {len(args): 0}
        args.append(y_prev)
    return pl.pallas_call(
        _final_kernel,
        grid=(nb, l // tl),
        in_specs=in_specs,
        out_specs=tile,
        out_shape=jax.ShapeDtypeStruct((b, l, d), F32),
        input_output_aliases=aliases,
        compiler_params=_params("parallel", "parallel"),
        name="combine_final",
    )(*args)


def _group_mixer(x3, mod3, mod_row0, hist, start, past, wts, tiles):
    b, l, d = x3.shape
    tl = tiles["tl"]
    dp = wts["w_pool"].shape[0] * wts["w_pool"].shape[1]
    da = wts["w_up_b"].shape[0]
    nh = da // HEAD_DIM
    t = b * l

    assert dp == da, "u, q, k, v must each be one column tile of the input projection"
    scale = HEAD_DIM ** -0.5 * (LOG2_E if past is None else 1.0)
    u, qb, k, kb, v, vb, gates, logf_t = _in_proj(x3, wts["g_mix"], mod3, mod_row0, wts["w_fl_t"], wts["b_f"],
                                                  wts["w_cat"], scale, tiles["nb_in"], tl)

    y_a = _pool_mixer(u, hist, wts["w_pool"], wts["s_pool"], start, tl)

    if past is None:
        y_b = _attn_prompt(qb, kb, vb, _cumsum_last(logf_t), tiles["tq"])
    else:
        cache_k, cache_v, cache_logf = past
        p = cache_k.shape[1]
        pc = tiles["past_chunk"]
        assert l <= LANES and p % pc == 0 and p % LANES == 0
        lf_all = jnp.concatenate([jnp.swapaxes(cache_logf.astype(F32), 1, 2), logf_t,
                                  jnp.zeros((b, nh, LANES - l), F32)], axis=2)
        f_all = _cumsum_last(lf_all)
        f_past = f_all[:, :, :p].reshape(b, nh, p // pc, pc).transpose(0, 2, 1, 3)
        y_b = _attn_sample(qb, kb, vb, cache_k, cache_v, f_past, f_all[:, :, p:], pc)

    x1, h2, h2_packed = _mixer_out(y_a, y_b, gates, x3, mod3, mod_row0, wts["g_ffn"], wts["w_up_a"], wts["w_up_b"],
                                   wts["w_out"], tiles["tl_out"])
    caches = (u[:, l - POOL_HIST:, :], k.reshape(b, l, nh, HEAD_DIM), v.reshape(b, l, nh, HEAD_DIM),
              jnp.swapaxes(logf_t, 1, 2))
    return dict(x1=x1, h2=h2, h2_packed=h2_packed, mod_row0=mod_row0, tiles=tiles), caches


def _moe_chunk(segments, wts):
    n_exp = wts["w_router_t"].shape[0]
    row_shape = (PACK_CHUNKS, LANES)
    counts_f = jnp.zeros((n_exp, LANES), F32)
    routed = []
    for grp, b0, nb in segments:
        b, l, d = grp["h2"].shape
        ntok = nb * l
        idx_t, gate_t, rank_t, counts_f = _router(grp["h2"].reshape(b * l, d), b0 * l, ntok, wts["w_router_t"],
                                                  wts["b_router"], counts_f, min(grp["tiles"]["t_route"], ntok))
        routed.append((idx_t, gate_t, rank_t, ntok))
    counts = counts_f[:, 0].astype(I32)

    n_slots = sum(r[3] for r in routed) * TOP_K
    n_blocks = -(-n_slots // MOE_ROWS) + n_exp
    cap = n_blocks * MOE_ROWS
    blk_e, blk_valid, blk_next, pad_start, n_used = _layout(counts, n_blocks)

    scatter_segs, gather_idx = [], []
    for (grp, b0, nb), (idx_t, _, rank_t, ntok) in zip(segments, routed):
        b, l, _ = grp["h2"].shape
        dest = _slot_rows(pad_start, idx_t, rank_t)
        win_d = min(SC_WINDOW, ntok // SC_WORKERS)
        dest_win = dest.reshape(TOP_K, ntok // win_d, win_d).transpose(1, 0, 2)
        scatter_segs.append((grp["h2_packed"].reshape((b * l,) + row_shape), b0 * l, dest_win))
        win_c = min(SC_WINDOW, ntok * TOP_K // SC_WORKERS)
        gather_idx.append(dest.reshape(ntok * TOP_K // win_c, win_c))

    x_sorted = _sc_scatter_rows(scatter_segs, cap)
    y_sorted = _moe_experts(x_sorted.reshape(cap * PACK_CHUNKS, LANES), blk_e, blk_valid, blk_next, n_used,
                            wts["w_e_gate"], wts["w_e_up"], wts["w_e_down"])
    y_toks = _sc_gather_rows(y_sorted.reshape((cap,) + row_shape), gather_idx)
    out = []
    for (grp, b0, nb), (_, gate_t, _, _), y_tok in zip(segments, routed, y_toks):
        l = grp["h2"].shape[1]
        out.append((y_tok.reshape(TOP_K, nb, l * PACK_CHUNKS, LANES), gate_t.T.reshape(nb, l, TOP_K)))
    return out


def kernel(x_prompt, x_sample, cache_pool, cache_k, cache_v, cache_logf, c_prompt, c_sample, w_ada, b_ada, g_mix, w_in, b_f, w_pool, s_pool, w_up_a, w_up_b, w_out, g_ffn, w_router, b_router, w_e_gate, w_e_up, w_e_down, w_s_gate, w_s_up, w_s_down, g_final):
    depth = w_ada.shape[0]
    assert depth == 1, "a single layer is supported"
    bp, lp, d = x_prompt.shape
    bs, ls, _ = x_sample.shape
    dp = w_pool.shape[1] * w_pool.shape[2]
    da = w_up_b.shape[1]
    nh = da // HEAD_DIM
    assert bp <= MOD_ROWS_SAMPLE

    one = lambda a: a.reshape(a.shape[1:])
    c_all = jnp.zeros((MOD_ROWS_SAMPLE + bs, d), F32).at[:bp].set(c_prompt).at[MOD_ROWS_SAMPLE:].set(c_sample)
    mod3 = _ada_mod(c_all, one(w_ada), one(b_ada)).reshape(c_all.shape[0], N_MOD, d)

    n_main = dp + 3 * da
    w_in1 = one(w_in)
    wts = dict(
        g_mix=one(g_mix), g_ffn=one(g_ffn), b_f=one(b_f), s_pool=one(s_pool), b_router=one(b_router),
        w_cat=jnp.concatenate([w_in1[:, :n_main], w_in1[:, n_main + nh:]], axis=1).astype(BF16),
        w_fl_t=w_in1[:, n_main:n_main + nh].T.astype(BF16),
        w_pool=one(w_pool).astype(BF16),
        w_up_a=one(w_up_a).astype(BF16), w_up_b=one(w_up_b).astype(BF16), w_out=one(w_out).astype(BF16),
        w_router_t=one(w_router).T.astype(BF16),
        w_e_gate=one(w_e_gate), w_e_up=one(w_e_up), w_e_down=one(w_e_down),
        w_s_gate=one(w_s_gate).astype(BF16), w_s_up=one(w_s_up).astype(BF16), w_s_down=one(w_s_down).astype(BF16),
    )

    tiles_p = dict(tl=min(512, lp), tq=min(512, lp), t_route=min(512, bp * lp), tl_out=min(256, lp),
                   tl_final=min(256, lp), nb_in=1)
    tiles_s = dict(tl=ls, t_route=bs * ls, tl_out=ls, tl_final=ls, nb_in=bs, past_chunk=min(512, cache_k.shape[2]))

    hist_p = jnp.zeros((bp, HALO, dp), F32)
    grp_p, caches_p = _group_mixer(x_prompt, mod3, MOD_ROWS_PROMPT, hist_p, 0, None, wts, tiles_p)
    hist_s = jnp.pad(one(cache_pool), ((0, 0), (HALO - POOL_HIST, 0), (0, 0)))
    past = (one(cache_k), one(cache_v), one(cache_logf))
    grp_s, caches_s = _group_mixer(x_sample, mod3, MOD_ROWS_SAMPLE, hist_s, cache_k.shape[2], past, wts, tiles_s)

    n_chunks = MOE_CHUNKS if bp % MOE_CHUNKS == 0 else 1
    nb_c = bp // n_chunks
    chunks = [[(grp_p, c * nb_c, nb_c)] for c in range(n_chunks)]
    chunks[-1].append((grp_s, 0, bs))
    outs = {id(grp_p): None, id(grp_s): None}
    for segments in chunks:
        for (grp, b0, nb), (y_tok, gate) in zip(segments, _moe_chunk(segments, wts)):
            outs[id(grp)] = _final(grp["x1"], grp["h2"], b0, y_tok, gate, mod3, grp["mod_row0"], wts["w_s_gate"],
                                   wts["w_s_up"], wts["w_s_down"], g_final, grp["tiles"]["tl_final"],
                                   y_prev=outs[id(grp)])
    stack = lambda a: a[None]
    return (outs[id(grp_p)], outs[id(grp_s)], *map(stack, caches_p), *map(stack, caches_s))
```

```python
import functools

import jax
import jax.numpy as jnp
from jax import lax
from jax.experimental import pallas as pl
from jax.experimental.pallas import tpu as pltpu
from jax.experimental.pallas import tpu_sc as plsc

F32 = jnp.float32
BF16 = jnp.bfloat16
I32 = jnp.int32
U32 = jnp.uint32

EPS = 1e-6
N_MOD = 6
POOL_WINDOWS = (2, 4, 8, 16)
POOL_HIST = max(POOL_WINDOWS) - 1
HALO = 16
HEAD_DIM = 128
TOP_K = 8
ROUTED_SCALE = 2.5
MOE_ROWS = 512
MOE_CHAIN_ROWS = MOE_ROWS // 2
MOE_CHUNKS = 2
MOD_ROWS_PROMPT = 0
MOD_ROWS_SAMPLE = 8
VMEM_LIMIT_BYTES = 52 * 1024 * 1024
NEG_INF = float("-inf")
LOG2_E = 1.4426950408889634
LANES = 128
PACK_CHUNKS = 8
SC_CORES = 2
SC_SUBCORES = 16
SC_WORKERS = SC_CORES * SC_SUBCORES
SC_WINDOW = 32
NT_DIMS = (((1,), (1,)), ((), ()))


def _params(*semantics):
    return pltpu.CompilerParams(dimension_semantics=semantics, vmem_limit_bytes=VMEM_LIMIT_BYTES)


def _silu(x):
    return x * jax.nn.sigmoid(x)


def _store_packed(ref, x, row0=0):
    rows, width = x.shape
    half = width // 2
    bits = pltpu.bitcast(x.astype(BF16).astype(F32), U32)
    packed = (bits[:, :half] >> 16) | (bits[:, half:] & jnp.uint32(0xFFFF0000))
    for j in range(PACK_CHUNKS):
        ref[pl.ds(row0 * PACK_CHUNKS + j, rows, stride=PACK_CHUNKS), :] = packed[:, j * LANES:(j + 1) * LANES]


def _load_packed(ref, rows, valid=None, row0=0):
    lo, hi = [], []
    for j in range(PACK_CHUNKS):
        p = ref[pl.ds(row0 * PACK_CHUNKS + j, rows, stride=PACK_CHUNKS), :]
        if valid is not None:
            p = jnp.where(row0 + lax.broadcasted_iota(I32, p.shape, 0) < valid, p, jnp.uint32(0))
        lo.append(pltpu.bitcast(p << 16, F32))
        hi.append(pltpu.bitcast(p & jnp.uint32(0xFFFF0000), F32))
    return jnp.concatenate(lo + hi, axis=1)


def _ada_kernel(c_ref, w_ref, b_ref, o_ref):
    a = _silu(c_ref[...]).astype(BF16)
    o_ref[...] = jnp.dot(a, w_ref[...].astype(BF16), preferred_element_type=F32) + b_ref[...]


def _ada_mod(c_all, w_ada, b_ada):
    rows, d = c_all.shape
    n = w_ada.shape[1]
    tn = 1024
    return pl.pallas_call(
        _ada_kernel,
        grid=(n // tn,),
        in_specs=[
            pl.BlockSpec((rows, d), lambda j: (0, 0)),
            pl.BlockSpec((d, tn), lambda j: (0, j)),
            pl.BlockSpec((1, tn), lambda j: (0, j)),
        ],
        out_specs=pl.BlockSpec((rows, tn), lambda j: (0, j)),
        out_shape=jax.ShapeDtypeStruct((rows, n), F32),
        compiler_params=_params("parallel"),
        name="ada_mod",
    )(c_all, w_ada, b_ada.reshape(1, n))


def _modulated_norm(x, g, m, shift_row, scale_row):
    ms = jnp.mean(x * x, axis=-1, keepdims=True)
    return x * lax.rsqrt(ms + EPS) * g * (1.0 + m[scale_row:scale_row + 1, :]) + m[shift_row:shift_row + 1, :]


def _in_proj_kernel(x_ref, g_ref, mod_ref, wfl_ref, bf_ref, w_ref, hist_ref, wp_ref, sp_ref,
                    ya_ref, tail_ref, q_ref, k32_ref, kb_ref, v32_ref, vb_ref, gates_ref, lf_ref,
                    h_ref, full_ref, carry_ref, *, q_scale, start):
    li = pl.program_id(1)
    j = pl.program_id(2)
    nb, tl, _ = x_ref.shape
    nh = wfl_ref.shape[0]
    gw = wp_ref.shape[1]

    @pl.when(j == 0)
    def _():
        for bi in range(nb):
            hb = _modulated_norm(x_ref[bi], g_ref[...], mod_ref[bi], 0, 1).astype(BF16)
            h_ref[bi * tl:(bi + 1) * tl, :] = hb
            fl = lax.dot_general(wfl_ref[...], hb, NT_DIMS, preferred_element_type=F32) + bf_ref[...]
            lf_ref[bi] = jnp.minimum(fl, 0.0) - jnp.log(1.0 + jnp.exp(-jnp.abs(fl)))

    def tile_rows():
        acc = jnp.dot(h_ref[...], w_ref[...], preferred_element_type=F32)
        return [acc[bi * tl:(bi + 1) * tl, :] for bi in range(nb)]

    def store_heads(f32_ref, bf_ref_):
        for bi, r in enumerate(tile_rows()):
            bf_ref_[bi] = r.astype(BF16)
            for hd in range(nh):
                f32_ref[bi, pl.ds(hd, tl, stride=nh), :] = r[:, hd * HEAD_DIM:(hd + 1) * HEAD_DIM]

    @pl.when(j == 0)
    def _():
        pos = start + li * tl + lax.broadcasted_iota(I32, (tl, 1), 0)
        for bi, u in enumerate(tile_rows()):
            full_ref[0:HALO, :] = jnp.where(li == 0, hist_ref[bi], carry_ref[bi])
            full_ref[HALO:HALO + tl, :] = u
            carry_ref[bi] = u[tl - HALO:, :]
            tail_ref[bi] = u[tl - HALO:, :]
            for g, w in enumerate(POOL_WINDOWS):
                c0, c1 = g * gw, (g + 1) * gw
                cur = full_ref[HALO:HALO + tl, c0:c1]
                win = cur
                for d in range(1, w):
                    win = win + full_ref[HALO - d:HALO - d + tl, c0:c1]
                dmean = win / jnp.minimum(pos + 1, w).astype(F32) - cur
                y = jnp.dot(dmean.astype(BF16), wp_ref[g], preferred_element_type=F32) * sp_ref[:, c0:c1]
                ya_ref[bi, :, c0:c1] = y.astype(BF16)

    @pl.when(j == 1)
    def _():
        for bi, r in enumerate(tile_rows()):
            q_ref[bi] = (r * q_scale).astype(BF16)

    @pl.when(j == 2)
    def _():
        store_heads(k32_ref, kb_ref)

    @pl.when(j == 3)
    def _():
        store_heads(v32_ref, vb_ref)

    @pl.when(j >= 4)
    def _():
        for bi, r in enumerate(tile_rows()):
            gates_ref[bi] = jax.nn.sigmoid(r).astype(BF16)


def _in_proj(x3, g, mod3, mod_row0, w_fl_t, b_f, w_cat, hist, w_pool, s_pool, start, q_scale, nb, tl):
    b, l, d = x3.shape
    nh = w_fl_t.shape[0]
    tn = nh * HEAD_DIM
    n_tiles = w_cat.shape[1] // tn
    n_gate_tiles = n_tiles - 4
    n_win, gw, _ = w_pool.shape
    assert mod_row0 % nb == 0 and b % nb == 0 and n_win * gw == tn and tl >= HALO
    tile = lambda width: pl.BlockSpec((nb, tl, width), lambda bb, li, j: (bb, li, 0))
    heads = pl.BlockSpec((nb, tl * nh, HEAD_DIM), lambda bb, li, j: (bb, li, 0))
    halo = pl.BlockSpec((nb, HALO, tn), lambda bb, li, j: (bb, 0, 0))
    const = lambda shape: pl.BlockSpec(shape, lambda bb, li, j: (0,) * len(shape))
    return pl.pallas_call(
        functools.partial(_in_proj_kernel, q_scale=q_scale, start=start),
        grid=(b // nb, l // tl, n_tiles),
        in_specs=[
            tile(d),
            const((1, d)),
            pl.BlockSpec((nb, N_MOD, d), lambda bb, li, j: (mod_row0 // nb + bb, 0, 0)),
            const((nh, d)),
            const((nh, 1)),
            pl.BlockSpec((d, tn), lambda bb, li, j: (0, j)),
            halo,
            const((n_win, gw, gw)),
            const((1, tn)),
        ],
        out_specs=[
            tile(tn), halo, tile(tn), heads, tile(tn), heads, tile(tn),
            pl.BlockSpec((nb, tl, tn), lambda bb, li, j: (bb, li, jnp.maximum(j - 4, 0))),
            pl.BlockSpec((nb, nh, tl), lambda bb, li, j: (bb, 0, li)),
        ],
        out_shape=[
            jax.ShapeDtypeStruct((b, l, tn), BF16),
            jax.ShapeDtypeStruct((b, HALO, tn), F32),
            jax.ShapeDtypeStruct((b, l, tn), BF16),
            jax.ShapeDtypeStruct((b, l * nh, HEAD_DIM), F32),
            jax.ShapeDtypeStruct((b, l, tn), BF16),
            jax.ShapeDtypeStruct((b, l * nh, HEAD_DIM), F32),
            jax.ShapeDtypeStruct((b, l, tn), BF16),
            jax.ShapeDtypeStruct((b, l, n_gate_tiles * tn), BF16),
            jax.ShapeDtypeStruct((b, nh, l), F32),
        ],
        scratch_shapes=[pltpu.VMEM((nb * tl, d), BF16), pltpu.VMEM((HALO + tl, tn), F32),
                        pltpu.VMEM((nb, HALO, tn), F32)],
        compiler_params=_params("parallel", "arbitrary", "arbitrary"),
        name="in_proj",
    )(x3, g.reshape(1, d), mod3, w_fl_t, b_f.reshape(nh, 1), w_cat, hist, w_pool, s_pool.reshape(1, tn))


def _mixer_out_kernel(ya_ref, yb_ref, ga_ref, gb_ref, x_ref, mod_ref, g_ref, wa_ref, wb_ref, wo_ref,
                      x1_ref, h2_ref, h2p_ref):
    merged = ga_ref[...].astype(F32) * jnp.dot(ya_ref[...], wa_ref[...], preferred_element_type=F32)
    merged = merged + gb_ref[...].astype(F32) * jnp.dot(yb_ref[...], wb_ref[...], preferred_element_type=F32)
    m = mod_ref[...]
    x1 = x_ref[...] + m[2:3, :] * jnp.dot(merged.astype(BF16), wo_ref[...], preferred_element_type=F32)
    x1_ref[...] = x1
    h2 = _modulated_norm(x1, g_ref[...], m, 3, 4)
    h2_ref[...] = h2.astype(BF16)
    _store_packed(h2p_ref, h2)


def _mixer_out(y_a, y_b, gates, x3, mod3, mod_row0, g_ffn, w_up_a, w_up_b, w_out, tl):
    b, l, d = x3.shape
    n_l = l // tl
    dp, da = y_a.shape[2], y_b.shape[2]
    tile = lambda width, col=0: pl.BlockSpec((None, tl, width), lambda bi, li: (bi, li, col))
    resident = lambda shape: pl.BlockSpec(shape, lambda bi, li: (0,) * len(shape), pipeline_mode=pl.Buffered(1))
    return pl.pallas_call(
        _mixer_out_kernel,
        grid=(b, n_l),
        in_specs=[
            tile(dp), tile(da), tile(d, 0), tile(d, 1), tile(d),
            pl.BlockSpec((None, N_MOD, d), lambda bi, li: (mod_row0 + bi, 0, 0)),
            pl.BlockSpec((1, d), lambda bi, li: (0, 0)),
            resident((dp, d)), resident((da, d)), resident((d, d)),
        ],
        out_specs=[
            tile(d), tile(d),
            pl.BlockSpec((tl * PACK_CHUNKS, LANES), lambda bi, li: (bi * n_l + li, 0)),
        ],
        out_shape=[
            jax.ShapeDtypeStruct((b, l, d), F32),
            jax.ShapeDtypeStruct((b, l, d), BF16),
            jax.ShapeDtypeStruct((b * l * PACK_CHUNKS, LANES), U32),
        ],
        compiler_params=_params("parallel", "parallel"),
        name="mixer_out",
    )(y_a, y_b, gates, gates, x3, mod3, g_ffn.reshape(1, d), w_up_a, w_up_b, w_out)


def _cumsum_kernel(x_ref, o_ref):
    x = x_ref[...]
    n = x.shape[1]
    idx = lax.broadcasted_iota(I32, x.shape, 1)
    s = 1
    while s < n:
        x = x + jnp.where(idx >= s, pltpu.roll(x, s, 1), 0.0)
        s *= 2
    o_ref[...] = x


def _cumsum_last(x3):
    b, h, n = x3.shape
    return pl.pallas_call(
        _cumsum_kernel,
        grid=(b,),
        in_specs=[pl.BlockSpec((None, h, n), lambda bi: (bi, 0, 0))],
        out_specs=pl.BlockSpec((None, h, n), lambda bi: (bi, 0, 0)),
        out_shape=jax.ShapeDtypeStruct((b, h, n), F32),
        compiler_params=_params("parallel"),
        name="cumsum_logf",
    )(x3)


def _attn_prompt_kernel(q_ref, k_ref, v_ref, fk_ref, o_ref, s0_ref, s1_ref, m_ref, l_ref, acc_ref, *, tq):
    qi = pl.program_id(2)
    tk = tq // 2
    q = q_ref[...]
    n_chunks = tk // LANES

    def produce(j, s_ref):
        off = pl.multiple_of(j * tk, tk)
        s = lax.dot_general(q, k_ref[pl.ds(off, tk), :], NT_DIMS, preferred_element_type=F32)
        s_ref[...] = s - fk_ref[j] * LOG2_E

    def consume(j, s_ref, first_col=None):
        s = s_ref[...]
        if first_col is not None:
            row = lax.broadcasted_iota(I32, (tq, tk), 0)
            col = lax.broadcasted_iota(I32, (tq, tk), 1)
            s = jnp.where(col + first_col <= row, s, NEG_INF)
        chunks = [s[:, c * LANES:(c + 1) * LANES] for c in range(n_chunks)]
        m_old = m_ref[...]
        m_new = jnp.maximum(m_old, jnp.max(functools.reduce(jnp.maximum, chunks), axis=1, keepdims=True))
        alpha = jnp.exp2(m_old - m_new)
        p_chunks = [jnp.exp2(ch - m_new) for ch in chunks]
        l_ref[...] = alpha * l_ref[...] + functools.reduce(lambda a, c: a + c, p_chunks)
        p = jnp.concatenate(p_chunks, axis=1).astype(BF16)
        v = v_ref[pl.ds(pl.multiple_of(j * tk, tk), tk), :]
        acc_ref[...] = alpha * acc_ref[...] + jnp.dot(p, v, preferred_element_type=F32)
        m_ref[...] = m_new

    m_ref[...] = jnp.full(m_ref.shape, NEG_INF, F32)
    l_ref[...] = jnp.zeros(l_ref.shape, F32)
    acc_ref[...] = jnp.zeros(acc_ref.shape, F32)
    produce(0, s0_ref)

    def pair(jj, carry):
        j0 = 2 * jj
        produce(j0 + 1, s1_ref)
        consume(j0, s0_ref)
        produce(j0 + 2, s0_ref)
        consume(j0 + 1, s1_ref)
        return carry

    lax.fori_loop(0, qi, pair, 0)
    produce(2 * qi + 1, s1_ref)
    consume(2 * qi, s0_ref, first_col=0)
    consume(2 * qi + 1, s1_ref, first_col=tk)

    l = jnp.sum(l_ref[...], axis=1, keepdims=True)
    o_ref[...] = (acc_ref[...] / l).astype(o_ref.dtype)


def _attn_prompt(q3, k3, v3, f3, tq):
    b, l, da = q3.shape
    nh = da // HEAD_DIM
    tk = tq // 2
    nk = l // tk
    fk = f3.reshape(b * nh, nk, 1, tk)
    return pl.pallas_call(
        functools.partial(_attn_prompt_kernel, tq=tq),
        grid=(b, nh, l // tq),
        in_specs=[
            pl.BlockSpec((None, tq, HEAD_DIM), lambda bi, hi, qi: (bi, qi, hi)),
            pl.BlockSpec((None, l, HEAD_DIM), lambda bi, hi, qi: (bi, 0, hi)),
            pl.BlockSpec((None, l, HEAD_DIM), lambda bi, hi, qi: (bi, 0, hi)),
            pl.BlockSpec((None, nk, 1, tk), lambda bi, hi, qi: (bi * nh + hi, 0, 0, 0)),
        ],
        out_specs=pl.BlockSpec((None, tq, HEAD_DIM), lambda bi, hi, qi: (bi, qi, hi)),
        out_shape=jax.ShapeDtypeStruct((b, l, da), BF16),
        scratch_shapes=[
            pltpu.VMEM((tq, tk), F32),
            pltpu.VMEM((tq, tk), F32),
            pltpu.VMEM((tq, LANES), F32),
            pltpu.VMEM((tq, LANES), F32),
            pltpu.VMEM((tq, HEAD_DIM), F32),
        ],
        compiler_params=_params("parallel", "parallel", "parallel"),
        name="fox_prompt",
    )(q3, k3, v3, fk)


def _attn_sample_kernel(q_ref, kp_ref, vp_ref, kn_ref, vn_ref, fp_ref, fn_ref, o_ref, m_ref, l_ref, acc_ref, *,
                        nh, pc, lq):
    c = pl.program_id(1)

    @pl.when(c == 0)
    def _():
        m_ref[...] = jnp.full(m_ref.shape, NEG_INF, F32)
        l_ref[...] = jnp.zeros(l_ref.shape, F32)
        acc_ref[...] = jnp.zeros(acc_ref.shape, F32)

    for h in range(nh):
        cols = slice(h * HEAD_DIM, (h + 1) * HEAD_DIM)
        q = q_ref[:, cols]
        k = kp_ref[pl.ds(h, pc, stride=nh), :].astype(BF16)
        v = vp_ref[pl.ds(h, pc, stride=nh), :].astype(BF16)
        s = lax.dot_general(q, k, NT_DIMS, preferred_element_type=F32) - fp_ref[h:h + 1, :]
        chunks = [s[:, j * LANES:(j + 1) * LANES] for j in range(pc // LANES)]
        m_old = m_ref[h]
        m_new = jnp.maximum(m_old, jnp.max(functools.reduce(jnp.maximum, chunks), axis=1, keepdims=True))
        alpha = jnp.exp(m_old - m_new)
        p_chunks = [jnp.exp(ch - m_new) for ch in chunks]
        l_ref[h] = alpha * l_ref[h] + functools.reduce(lambda a, b: a + b, p_chunks)
        p = jnp.concatenate(p_chunks, axis=1).astype(BF16)
        acc_ref[h] = alpha * acc_ref[h] + jnp.dot(p, v, preferred_element_type=F32)
        m_ref[h] = m_new

    @pl.when(c == pl.num_programs(1) - 1)
    def _():
        row = lax.broadcasted_iota(I32, (lq, lq), 0)
        col = lax.broadcasted_iota(I32, (lq, lq), 1)
        for h in range(nh):
            cols = slice(h * HEAD_DIM, (h + 1) * HEAD_DIM)
            s = lax.dot_general(q_ref[:, cols], kn_ref[:, cols], NT_DIMS, preferred_element_type=F32)
            s = jnp.where(col <= row, s - fn_ref[h:h + 1, :lq], NEG_INF)
            m_old = m_ref[h]
            m_new = jnp.maximum(m_old, jnp.max(s, axis=1, keepdims=True))
            alpha = jnp.exp(m_old - m_new)
            p = jnp.exp(s - m_new[:, :lq])
            l = jnp.sum(alpha * l_ref[h], axis=1, keepdims=True) + jnp.sum(p, axis=1, keepdims=True)
            acc = alpha * acc_ref[h] + jnp.dot(p.astype(BF16), vn_ref[:, cols], preferred_element_type=F32)
            o_ref[:, cols] = (acc / l).astype(o_ref.dtype)


def _attn_sample(q3, kn3, vn3, cache_k, cache_v, f_past, f_new, pc):
    b, lq, da = q3.shape
    _, past, nh, dh = cache_k.shape
    nc = past // pc
    new_spec = pl.BlockSpec((None, lq, da), lambda bi, ci: (bi, 0, 0))
    past_spec = pl.BlockSpec((None, pc * nh, dh), lambda bi, ci: (bi, ci, 0))
    return pl.pallas_call(
        functools.partial(_attn_sample_kernel, nh=nh, pc=pc, lq=lq),
        grid=(b, nc),
        in_specs=[new_spec, past_spec, past_spec, new_spec, new_spec,
                  pl.BlockSpec((None, None, nh, pc), lambda bi, ci: (bi, ci, 0, 0)),
                  pl.BlockSpec((None, nh, LANES), lambda bi, ci: (bi, 0, 0))],
        out_specs=new_spec,
        out_shape=jax.ShapeDtypeStruct((b, lq, da), BF16),
        scratch_shapes=[pltpu.VMEM((nh, lq, LANES), F32), pltpu.VMEM((nh, lq, LANES), F32),
                        pltpu.VMEM((nh, lq, HEAD_DIM), F32)],
        compiler_params=_params("parallel", "arbitrary"),
        name="fox_sample",
    )(q3, cache_k.reshape(b, past * nh, dh), cache_v.reshape(b, past * nh, dh), kn3, vn3, f_past, f_new)


def _router_kernel(h_ref, wr_ref, br_ref, tri_ref, cnt0_ref, idx_ref, gate_ref, rank_ref, cnt_ref, carry_ref):
    @pl.when(pl.program_id(0) == 0)
    def _():
        carry_ref[...] = cnt0_ref[...]

    n_exp, tm = wr_ref.shape[0], h_ref.shape[0]
    scores = jax.nn.sigmoid(lax.dot_general(wr_ref[...], h_ref[...], NT_DIMS, preferred_element_type=F32))
    work = scores + br_ref[...]
    eidx = lax.broadcasted_iota(I32, (n_exp, tm), 0)
    picked = jnp.zeros((n_exp, tm), F32)
    onehots, gates = [], []
    for k in range(TOP_K):
        best = jnp.max(work, axis=0, keepdims=True)
        first = jnp.min(jnp.where(work == best, eidx, n_exp), axis=0, keepdims=True)
        onehot = eidx == first
        idx_ref[k:k + 1, :] = first
        gates.append(jnp.sum(jnp.where(onehot, scores, 0.0), axis=0, keepdims=True))
        onehots.append(onehot)
        picked = picked + jnp.where(onehot, 1.0, 0.0)
        work = jnp.where(onehot, NEG_INF, work)
    norm = ROUTED_SCALE / functools.reduce(lambda a, c: a + c, gates)
    cum = jnp.dot(picked.astype(BF16), tri_ref[...], preferred_element_type=F32)
    before = cum - picked + carry_ref[:, 0:1]
    for k in range(TOP_K):
        gate_ref[k:k + 1, :] = gates[k] * norm
        rank_ref[k:k + 1, :] = jnp.sum(jnp.where(onehots[k], before, 0.0), axis=0, keepdims=True).astype(I32)
    carry_ref[...] = carry_ref[...] + cum[:, tm - 1:tm]
    cnt_ref[...] = carry_ref[...]


def _router(h2, tok0, ntok, w_router_t, b_router, cnt0, tm):
    d = h2.shape[1]
    n_exp = w_router_t.shape[0]
    tri = jnp.triu(jnp.ones((tm, tm), BF16))
    tok_spec = pl.BlockSpec((TOP_K, tm), lambda i: (0, i))
    cnt_spec = pl.BlockSpec((n_exp, LANES), lambda i: (0, 0))
    return pl.pallas_call(
        _router_kernel,
        grid=(ntok // tm,),
        in_specs=[
            pl.BlockSpec((tm, d), lambda i: (tok0 // tm + i, 0)),
            pl.BlockSpec((n_exp, d), lambda i: (0, 0)),
            pl.BlockSpec((n_exp, 1), lambda i: (0, 0)),
            pl.BlockSpec((tm, tm), lambda i: (0, 0)),
            cnt_spec,
        ],
        out_specs=[tok_spec, tok_spec, tok_spec, cnt_spec],
        out_shape=[
            jax.ShapeDtypeStruct((TOP_K, ntok), I32),
            jax.ShapeDtypeStruct((TOP_K, ntok), F32),
            jax.ShapeDtypeStruct((TOP_K, ntok), I32),
            jax.ShapeDtypeStruct((n_exp, LANES), F32),
        ],
        scratch_shapes=[pltpu.VMEM((n_exp, LANES), F32)],
        compiler_params=_params("arbitrary"),
        name="moe_router",
    )(h2, w_router_t, b_router.reshape(n_exp, 1), tri, cnt0)


LAYOUT_ROWS = 8


def _layout_kernel(cnt_ref, o_ref):
    n_exp = cnt_ref.shape[0]
    lane = lax.broadcasted_iota(I32, o_ref.shape, 1)
    row0 = lane * MOE_ROWS

    def forward(e, carry):
        start, blk_e, valid, pad_start = carry
        cnt = cnt_ref[e]
        end = start + (cnt + MOE_ROWS - 1) // MOE_ROWS * MOE_ROWS
        in_span = lambda new, old: jnp.where(row0 >= start, jnp.where(row0 < end, new, old), old)
        blk_e = in_span(e, blk_e)
        valid = in_span(jnp.clip(start + cnt - row0, 0, MOE_ROWS), valid)
        return end, blk_e, valid, jnp.where(lane == e, start, pad_start)

    zeros = jnp.zeros(o_ref.shape, I32)
    end, blk_e, valid, pad_start = lax.fori_loop(0, n_exp, forward, (jnp.int32(0), zeros + (n_exp - 1), zeros, zeros))

    def backward(t, carry):
        nxt, blk_next = carry
        e = n_exp - 1 - t
        return jnp.where(cnt_ref[e] > 0, e, nxt), jnp.where(blk_e == e, nxt, blk_next)

    _, blk_next = lax.fori_loop(0, n_exp, backward, (jnp.int32(-1), zeros - 1))
    table = zeros + end // MOE_ROWS
    sub = lax.broadcasted_iota(I32, o_ref.shape, 0)
    for r, row in enumerate((blk_e, valid, blk_next, pad_start)):
        table = jnp.where(sub == r, row, table)
    o_ref[...] = table


def _layout(counts, n_blocks):
    n_exp = counts.shape[0]
    w = -(-max(n_blocks, n_exp) // LANES) * LANES
    table = pl.pallas_call(
        _layout_kernel,
        grid_spec=pltpu.PrefetchScalarGridSpec(
            num_scalar_prefetch=1, grid=(1,), in_specs=[],
            out_specs=pl.BlockSpec((LAYOUT_ROWS, w), lambda i, cnt: (0, 0))),
        out_shape=jax.ShapeDtypeStruct((LAYOUT_ROWS, w), I32),
        compiler_params=_params("arbitrary"),
        name="moe_layout",
    )(counts)
    return table[0, :n_blocks], table[1, :n_blocks], table[2, :n_blocks], table[3, :n_exp], table[4, :1]


def _slot_rows_kernel(ps_ref, idx_ref, rank_ref, o_ref):
    idx = idx_ref[...]
    rows = rank_ref[...]
    for e in range(ps_ref.shape[0]):
        rows = rows + jnp.where(idx == e, ps_ref[e], 0)
    o_ref[...] = rows


def _slot_rows(pad_start, idx_t, rank_t):
    k, ntok = idx_t.shape
    tm = min(2048, ntok)
    spec = pl.BlockSpec((k, tm), lambda i, ps: (0, i))
    return pl.pallas_call(
        _slot_rows_kernel,
        grid_spec=pltpu.PrefetchScalarGridSpec(num_scalar_prefetch=1, grid=(ntok // tm,), in_specs=[spec, spec],
                                               out_specs=spec),
        out_shape=jax.ShapeDtypeStruct((k, ntok), I32),
        compiler_params=_params("parallel"),
        name="moe_slot_rows",
    )(pad_start, idx_t, rank_t)


def _moe_kernel(be_ref, bv_ref, nx_ref, nu_ref, x_ref, wg_hbm, wu_hbm, wd_hbm, y_ref,
                wg_f32, wu_f32, wd_f32, wg_bf, wu_bf, wd_bf, sems):
    i = pl.program_id(0)
    active = i < nu_ref[0]
    expert = be_ref[i]

    def fetch(e):
        pairs = ((wg_hbm, wg_f32), (wu_hbm, wu_f32), (wd_hbm, wd_f32))
        return [pltpu.make_async_copy(w.at[e], buf, sems.at[n]) for n, (w, buf) in enumerate(pairs)]

    @pl.when(active & (i == 0))
    def _():
        for c in fetch(expert):
            c.start()

    @pl.when(active & ((i == 0) | (expert != be_ref[jnp.maximum(i - 1, 0)])))
    def _():
        for c in fetch(expert):
            c.wait()
        wg_bf[...] = wg_f32[...].astype(BF16)
        wu_bf[...] = wu_f32[...].astype(BF16)
        wd_bf[...] = wd_f32[...].astype(BF16)

        @pl.when(nx_ref[i] >= 0)
        def _():
            for c in fetch(nx_ref[i]):
                c.start()

    def chain(row0):
        x = _load_packed(x_ref, MOE_CHAIN_ROWS, valid=bv_ref[i], row0=row0).astype(BF16)
        g = jnp.dot(x, wg_bf[...], preferred_element_type=F32)
        u = jnp.dot(x, wu_bf[...], preferred_element_type=F32)
        a = (_silu(g) * u).astype(BF16)
        _store_packed(y_ref, jnp.dot(a, wd_bf[...], preferred_element_type=F32), row0=row0)

    both = bv_ref[i] > MOE_CHAIN_ROWS

    @pl.when(active & both)
    def _():
        chain(0)
        chain(MOE_CHAIN_ROWS)

    @pl.when(active & jnp.logical_not(both))
    def _():
        chain(0)


def _moe_experts(x_sorted, blk_e, blk_valid, blk_next, n_used, w_e_gate, w_e_up, w_e_down):
    cap = x_sorted.shape[0] // PACK_CHUNKS
    d, de = w_e_gate.shape[1:]
    n_blocks = cap // MOE_ROWS
    row_map = lambda i, be, bv, nx, nu: (jnp.minimum(i, nu[0] - 1), 0)
    hbm = pl.BlockSpec(memory_space=pl.ANY)
    return pl.pallas_call(
        _moe_kernel,
        grid_spec=pltpu.PrefetchScalarGridSpec(
            num_scalar_prefetch=4,
            grid=(n_blocks,),
            in_specs=[pl.BlockSpec((MOE_ROWS * PACK_CHUNKS, LANES), row_map), hbm, hbm, hbm],
            out_specs=pl.BlockSpec((MOE_ROWS * PACK_CHUNKS, LANES), row_map),
            scratch_shapes=[
                pltpu.VMEM((d, de), F32), pltpu.VMEM((d, de), F32), pltpu.VMEM((de, d), F32),
                pltpu.VMEM((d, de), BF16), pltpu.VMEM((d, de), BF16), pltpu.VMEM((de, d), BF16),
                pltpu.SemaphoreType.DMA((3,)),
            ],
        ),
        out_shape=jax.ShapeDtypeStruct(x_sorted.shape, U32),
        compiler_params=_params("arbitrary"),
        name="moe_experts",
    )(blk_e, blk_valid, blk_next, n_used, x_sorted, w_e_gate, w_e_up, w_e_down)


def _sc_mesh():
    return plsc.VectorSubcoreMesh(core_axis_name="c", subcore_axis_name="s")


def _sc_worker_id():
    return lax.axis_index("s") * SC_CORES + lax.axis_index("c")


def _sc_scatter_rows(segments, n_out):
    row_shape = segments[0][0].shape[1:]
    dtype = segments[0][0].dtype
    n_seg = len(segments)
    geom = [(tok0, dw.shape[0] // SC_WORKERS, dw.shape[2]) for _, tok0, dw in segments]
    scratch = []
    for _, n_win, win in geom:
        scratch += [pltpu.VMEM((n_win, TOP_K, win), I32), pltpu.VMEM((win,) + row_shape, dtype)]

    @functools.partial(
        pl.kernel, mesh=_sc_mesh(),
        out_type=jax.ShapeDtypeStruct((n_out,) + row_shape, dtype),
        scratch_types=scratch + [pltpu.SemaphoreType.DMA],
    )
    def scatter_kernel(*refs):
        out_hbm = refs[2 * n_seg]
        sem = refs[-1]
        for s, (tok0, n_win, win) in enumerate(geom):
            rows_hbm, idx_hbm = refs[2 * s], refs[2 * s + 1]
            idx_v, rows_v = refs[2 * n_seg + 1 + 2 * s], refs[2 * n_seg + 2 + 2 * s]
            first = _sc_worker_id() * n_win
            pltpu.sync_copy(idx_hbm.at[pl.ds(first, n_win)], idx_v)

            @pl.loop(0, n_win)
            def _(w):
                pltpu.sync_copy(rows_hbm.at[pl.ds(tok0 + (first + w) * win, win)], rows_v)
                copies = [pltpu.make_async_copy(rows_v, out_hbm.at[idx_v.at[w, k]], sem) for k in range(TOP_K)]
                for c in copies:
                    c.start()
                for c in copies:
                    c.wait()

    args = []
    for rows, _, dest_win in segments:
        args += [rows, dest_win]
    return scatter_kernel(*args)


def _sc_gather_rows(table, idx_wins):
    row_shape = table.shape[1:]
    n_seg = len(idx_wins)
    geom = [(iw.shape[0] // SC_WORKERS, iw.shape[1]) for iw in idx_wins]
    scratch = []
    for n_win, win in geom:
        scratch += [pltpu.VMEM((n_win, win), I32), pltpu.VMEM((win,) + row_shape, table.dtype)]

    @functools.partial(
        pl.kernel, mesh=_sc_mesh(),
        out_type=[jax.ShapeDtypeStruct((iw.shape[0] * iw.shape[1],) + row_shape, table.dtype) for iw in idx_wins],
        scratch_types=scratch + [pltpu.SemaphoreType.DMA],
    )
    def gather_kernel(*refs):
        table_hbm = refs[0]
        sem = refs[-1]
        for s, (n_win, win) in enumerate(geom):
            idx_hbm, out_hbm = refs[1 + s], refs[1 + n_seg + s]
            idx_v, rows_v = refs[1 + 2 * n_seg + 2 * s], refs[2 + 2 * n_seg + 2 * s]
            first = _sc_worker_id() * n_win
            pltpu.sync_copy(idx_hbm.at[pl.ds(first, n_win)], idx_v)

            @pl.loop(0, n_win)
            def _(w):
                pltpu.async_copy(table_hbm.at[idx_v.at[w]], rows_v, sem).wait()
                pltpu.sync_copy(rows_v, out_hbm.at[pl.ds((first + w) * win, win)])

    return gather_kernel(table, *idx_wins)


def _final_kernel(x1_ref, h2_ref, yt_ref, gate_ref, mod_ref, wsg_ref, wsu_ref, wsd_ref, gf_ref, *rest):
    o_ref = rest[-1]
    h2 = h2_ref[...]
    g = jnp.dot(h2, wsg_ref[...], preferred_element_type=F32)
    u = jnp.dot(h2, wsu_ref[...], preferred_element_type=F32)
    f = jnp.dot((_silu(g) * u).astype(BF16), wsd_ref[...], preferred_element_type=F32)
    gate = gate_ref[...]
    tl = h2.shape[0]
    for k in range(TOP_K):
        f = f + gate[:, k:k + 1] * _load_packed(yt_ref.at[k], tl)
    x2 = x1_ref[...] + mod_ref[N_MOD - 1:N_MOD, :] * f
    ms = jnp.mean(x2 * x2, axis=-1, keepdims=True)
    o_ref[...] = x2 * lax.rsqrt(ms + EPS) * gf_ref[...]


def _final(x1, h2, b0, y_tok, gate, mod3, mod_row0, w_s_gate, w_s_up, w_s_down, g_final, tl, y_prev=None):
    b, l, d = x1.shape
    nb = gate.shape[0]
    ds = w_s_gate.shape[1]
    tile = pl.BlockSpec((None, tl, d), lambda bi, li: (b0 + bi, li, 0))
    const = lambda shape: pl.BlockSpec(shape, lambda bi, li: (0,) * len(shape), pipeline_mode=pl.Buffered(1))
    in_specs = [
        tile,
        tile,
        pl.BlockSpec((TOP_K, None, tl * PACK_CHUNKS, LANES), lambda bi, li: (0, bi, li, 0)),
        pl.BlockSpec((None, tl, TOP_K), lambda bi, li: (bi, li, 0)),
        pl.BlockSpec((None, N_MOD, d), lambda bi, li: (mod_row0 + b0 + bi, 0, 0)),
        const((d, ds)),
        const((d, ds)),
        const((ds, d)),
        const((1, d)),
    ]
    args = [x1, h2, y_tok, gate, mod3, w_s_gate, w_s_up, w_s_down, g_final.reshape(1, d)]
    aliases = {}
    if y_prev is not None:
        in_specs.append(pl.BlockSpec(memory_space=pl.ANY))
        aliases = {len(args): 0}
        args.append(y_prev)
    return pl.pallas_call(
        _final_kernel,
        grid=(nb, l // tl),
        in_specs=in_specs,
        out_specs=tile,
        out_shape=jax.ShapeDtypeStruct((b, l, d), F32),
        input_output_aliases=aliases,
        compiler_params=_params("parallel", "parallel"),
        name="combine_final",
    )(*args)


def _group_mixer(x3, mod3, mod_row0, hist, start, past, wts, tiles):
    b, l, d = x3.shape
    tl = tiles["tl"]
    dp = wts["w_pool"].shape[0] * wts["w_pool"].shape[1]
    da = wts["w_up_b"].shape[0]
    nh = da // HEAD_DIM
    t = b * l

    assert dp == da, "u, q, k, v must each be one column tile of the input projection"
    scale = HEAD_DIM ** -0.5 * (LOG2_E if past is None else 1.0)
    y_a, u_tail, qb, k, kb, v, vb, gates, logf_t = _in_proj(
        x3, wts["g_mix"], mod3, mod_row0, wts["w_fl_t"], wts["b_f"], wts["w_cat"], hist, wts["w_pool"],
        wts["s_pool"], start, scale, tiles["nb_in"], tl)

    if past is None:
        y_b = _attn_prompt(qb, kb, vb, _cumsum_last(logf_t), tiles["tq"])
    else:
        cache_k, cache_v, cache_logf = past
        p = cache_k.shape[1]
        pc = tiles["past_chunk"]
        assert l <= LANES and p % pc == 0 and p % LANES == 0
        lf_all = jnp.concatenate([jnp.swapaxes(cache_logf.astype(F32), 1, 2), logf_t,
                                  jnp.zeros((b, nh, LANES - l), F32)], axis=2)
        f_all = _cumsum_last(lf_all)
        f_past = f_all[:, :, :p].reshape(b, nh, p // pc, pc).transpose(0, 2, 1, 3)
        y_b = _attn_sample(qb, kb, vb, cache_k, cache_v, f_past, f_all[:, :, p:], pc)

    x1, h2, h2_packed = _mixer_out(y_a, y_b, gates, x3, mod3, mod_row0, wts["g_ffn"], wts["w_up_a"], wts["w_up_b"],
                                   wts["w_out"], tiles["tl_out"])
    caches = (u_tail[:, HALO - POOL_HIST:, :], k.reshape(b, l, nh, HEAD_DIM), v.reshape(b, l, nh, HEAD_DIM),
              jnp.swapaxes(logf_t, 1, 2))
    return dict(x1=x1, h2=h2, h2_packed=h2_packed, mod_row0=mod_row0, tiles=tiles), caches


def _moe_chunk(segments, wts):
    n_exp = wts["w_router_t"].shape[0]
    row_shape = (PACK_CHUNKS, LANES)
    counts_f = jnp.zeros((n_exp, LANES), F32)
    routed = []
    for grp, b0, nb in segments:
        b, l, d = grp["h2"].shape
        ntok = nb * l
        idx_t, gate_t, rank_t, counts_f = _router(grp["h2"].reshape(b * l, d), b0 * l, ntok, wts["w_router_t"],
                                                  wts["b_router"], counts_f, min(grp["tiles"]["t_route"], ntok))
        routed.append((idx_t, gate_t, rank_t, ntok))
    counts = counts_f[:, 0].astype(I32)

    n_slots = sum(r[3] for r in routed) * TOP_K
    n_blocks = -(-n_slots // MOE_ROWS) + n_exp
    cap = n_blocks * MOE_ROWS
    blk_e, blk_valid, blk_next, pad_start, n_used = _layout(counts, n_blocks)

    scatter_segs, gather_idx = [], []
    for (grp, b0, nb), (idx_t, _, rank_t, ntok) in zip(segments, routed):
        b, l, _ = grp["h2"].shape
        dest = _slot_rows(pad_start, idx_t, rank_t)
        win_d = min(SC_WINDOW, ntok // SC_WORKERS)
        dest_win = dest.reshape(TOP_K, ntok // win_d, win_d).transpose(1, 0, 2)
        scatter_segs.append((grp["h2_packed"].reshape((b * l,) + row_shape), b0 * l, dest_win))
        win_c = min(SC_WINDOW, ntok * TOP_K // SC_WORKERS)
        gather_idx.append(dest.reshape(ntok * TOP_K // win_c, win_c))

    x_sorted = _sc_scatter_rows(scatter_segs, cap)
    y_sorted = _moe_experts(x_sorted.reshape(cap * PACK_CHUNKS, LANES), blk_e, blk_valid, blk_next, n_used,
                            wts["w_e_gate"], wts["w_e_up"], wts["w_e_down"])
    y_toks = _sc_gather_rows(y_sorted.reshape((cap,) + row_shape), gather_idx)
    out = []
    for (grp, b0, nb), (_, gate_t, _, _), y_tok in zip(segments, routed, y_toks):
        l = grp["h2"].shape[1]
        out.append((y_tok.reshape(TOP_K, nb, l * PACK_CHUNKS, LANES), gate_t.T.reshape(nb, l, TOP_K)))
    return out


def kernel(x_prompt, x_sample, cache_pool, cache_k, cache_v, cache_logf, c_prompt, c_sample, w_ada, b_ada, g_mix, w_in, b_f, w_pool, s_pool, w_up_a, w_up_b, w_out, g_ffn, w_router, b_router, w_e_gate, w_e_up, w_e_down, w_s_gate, w_s_up, w_s_down, g_final):
    depth = w_ada.shape[0]
    assert depth == 1, "a single layer is supported"
    bp, lp, d = x_prompt.shape
    bs, ls, _ = x_sample.shape
    dp = w_pool.shape[1] * w_pool.shape[2]
    da = w_up_b.shape[1]
    nh = da // HEAD_DIM
    assert bp <= MOD_ROWS_SAMPLE

    one = lambda a: a.reshape(a.shape[1:])
    c_all = jnp.zeros((MOD_ROWS_SAMPLE + bs, d), F32).at[:bp].set(c_prompt).at[MOD_ROWS_SAMPLE:].set(c_sample)
    mod3 = _ada_mod(c_all, one(w_ada), one(b_ada)).reshape(c_all.shape[0], N_MOD, d)

    n_main = dp + 3 * da
    w_in1 = one(w_in)
    wts = dict(
        g_mix=one(g_mix), g_ffn=one(g_ffn), b_f=one(b_f), s_pool=one(s_pool), b_router=one(b_router),
        w_cat=jnp.concatenate([w_in1[:, :n_main], w_in1[:, n_main + nh:]], axis=1).astype(BF16),
        w_fl_t=w_in1[:, n_main:n_main + nh].T.astype(BF16),
        w_pool=one(w_pool).astype(BF16),
        w_up_a=one(w_up_a).astype(BF16), w_up_b=one(w_up_b).astype(BF16), w_out=one(w_out).astype(BF16),
        w_router_t=one(w_router).T.astype(BF16),
        w_e_gate=one(w_e_gate), w_e_up=one(w_e_up), w_e_down=one(w_e_down),
        w_s_gate=one(w_s_gate).astype(BF16), w_s_up=one(w_s_up).astype(BF16), w_s_down=one(w_s_down).astype(BF16),
    )

    tiles_p = dict(tl=min(512, lp), tq=min(1024, lp), t_route=min(512, bp * lp), tl_out=min(256, lp),
                   tl_final=min(256, lp), nb_in=1)
    tiles_s = dict(tl=ls, t_route=bs * ls, tl_out=ls, tl_final=ls, nb_in=bs, past_chunk=min(512, cache_k.shape[2]))

    hist_p = jnp.zeros((bp, HALO, dp), F32)
    grp_p, caches_p = _group_mixer(x_prompt, mod3, MOD_ROWS_PROMPT, hist_p, 0, None, wts, tiles_p)
    hist_s = jnp.pad(one(cache_pool), ((0, 0), (HALO - POOL_HIST, 0), (0, 0)))
    past = (one(cache_k), one(cache_v), one(cache_logf))
    grp_s, caches_s = _group_mixer(x_sample, mod3, MOD_ROWS_SAMPLE, hist_s, cache_k.shape[2], past, wts, tiles_s)

    n_chunks = MOE_CHUNKS if bp % MOE_CHUNKS == 0 else 1
    nb_c = bp // n_chunks
    chunks = [[(grp_p, c * nb_c, nb_c)] for c in range(n_chunks)]
    chunks[-1].append((grp_s, 0, bs))
    outs = {id(grp_p): None, id(grp_s): None}
    for segments in chunks:
        for (grp, b0, nb), (y_tok, gate) in zip(segments, _moe_chunk(segments, wts)):
            outs[id(grp)] = _final(grp["x1"], grp["h2"], b0, y_tok, gate, mod3, grp["mod_row0"], wts["w_s_gate"],
                                   wts["w_s_up"], wts["w_s_down"], g_final, grp["tiles"]["tl_final"],
                                   y_prev=outs[id(grp)])
    stack = lambda a: a[None]
    return (outs[id(grp_p)], outs[id(grp_s)], *map(stack, caches_p), *map(stack, caches_s))
```

```python
import functools

import jax
import jax.numpy as jnp
from jax import lax
from jax.experimental import pallas as pl
from jax.experimental.pallas import tpu as pltpu
from jax.experimental.pallas import tpu_sc as plsc

F32 = jnp.float32
BF16 = jnp.bfloat16
I32 = jnp.int32
U32 = jnp.uint32

EPS = 1e-6
N_MOD = 6
POOL_WINDOWS = (2, 4, 8, 16)
POOL_HIST = max(POOL_WINDOWS) - 1
HALO = 16
HEAD_DIM = 128
TOP_K = 8
ROUTED_SCALE = 2.5
MOE_ROWS = 1024
MOE_CHAIN_ROWS = 256
MOE_CHUNKS = 2
MOD_ROWS_PROMPT = 0
MOD_ROWS_SAMPLE = 8
VMEM_LIMIT_BYTES = 52 * 1024 * 1024
NEG_INF = float("-inf")
LOG2_E = 1.4426950408889634
LANES = 128
PACK_CHUNKS = 8
SC_CORES = 2
SC_SUBCORES = 16
SC_WORKERS = SC_CORES * SC_SUBCORES
SC_WINDOW = 32
NT_DIMS = (((1,), (1,)), ((), ()))


def _params(*semantics):
    return pltpu.CompilerParams(dimension_semantics=semantics, vmem_limit_bytes=VMEM_LIMIT_BYTES)


def _silu(x):
    return x * jax.nn.sigmoid(x)


def _store_packed(ref, x, row0=0):
    rows, width = x.shape
    half = width // 2
    bits = pltpu.bitcast(x.astype(BF16).astype(F32), U32)
    packed = (bits[:, :half] >> 16) | (bits[:, half:] & jnp.uint32(0xFFFF0000))
    for j in range(PACK_CHUNKS):
        ref[pl.ds(row0 * PACK_CHUNKS + j, rows, stride=PACK_CHUNKS), :] = packed[:, j * LANES:(j + 1) * LANES]


def _load_packed(ref, rows, valid=None, row0=0):
    lo, hi = [], []
    for j in range(PACK_CHUNKS):
        p = ref[pl.ds(row0 * PACK_CHUNKS + j, rows, stride=PACK_CHUNKS), :]
        if valid is not None:
            p = jnp.where(row0 + lax.broadcasted_iota(I32, p.shape, 0) < valid, p, jnp.uint32(0))
        lo.append(pltpu.bitcast(p << 16, F32))
        hi.append(pltpu.bitcast(p & jnp.uint32(0xFFFF0000), F32))
    return jnp.concatenate(lo + hi, axis=1)


def _ada_kernel(c_ref, w_ref, b_ref, o_ref):
    a = _silu(c_ref[...]).astype(BF16)
    o_ref[...] = jnp.dot(a, w_ref[...].astype(BF16), preferred_element_type=F32) + b_ref[...]


def _ada_mod(c_all, w_ada, b_ada):
    rows, d = c_all.shape
    n = w_ada.shape[1]
    tn = 1024
    return pl.pallas_call(
        _ada_kernel,
        grid=(n // tn,),
        in_specs=[
            pl.BlockSpec((rows, d), lambda j: (0, 0)),
            pl.BlockSpec((d, tn), lambda j: (0, j)),
            pl.BlockSpec((1, tn), lambda j: (0, j)),
        ],
        out_specs=pl.BlockSpec((rows, tn), lambda j: (0, j)),
        out_shape=jax.ShapeDtypeStruct((rows, n), F32),
        compiler_params=_params("parallel"),
        name="ada_mod",
    )(c_all, w_ada, b_ada.reshape(1, n))


def _modulated_norm(x, g, m, shift_row, scale_row):
    ms = jnp.mean(x * x, axis=-1, keepdims=True)
    return x * lax.rsqrt(ms + EPS) * g * (1.0 + m[scale_row:scale_row + 1, :]) + m[shift_row:shift_row + 1, :]


def _in_proj_kernel(x_ref, g_ref, mod_ref, wfl_ref, bf_ref, w_ref, hist_ref, wp_ref, sp_ref,
                    ya_ref, tail_ref, q_ref, k32_ref, kb_ref, v32_ref, vb_ref, gates_ref, lf_ref,
                    h_ref, full_ref, carry_ref, *, q_scale, start):
    li = pl.program_id(1)
    j = pl.program_id(2)
    nb, tl, _ = x_ref.shape
    nh = wfl_ref.shape[0]
    gw = wp_ref.shape[1]

    @pl.when(j == 0)
    def _():
        for bi in range(nb):
            hb = _modulated_norm(x_ref[bi], g_ref[...], mod_ref[bi], 0, 1).astype(BF16)
            h_ref[bi * tl:(bi + 1) * tl, :] = hb
            fl = lax.dot_general(wfl_ref[...], hb, NT_DIMS, preferred_element_type=F32) + bf_ref[...]
            lf_ref[bi] = jnp.minimum(fl, 0.0) - jnp.log(1.0 + jnp.exp(-jnp.abs(fl)))

    def tile_rows():
        acc = jnp.dot(h_ref[...], w_ref[...], preferred_element_type=F32)
        return [acc[bi * tl:(bi + 1) * tl, :] for bi in range(nb)]

    def store_heads(f32_ref, bf_ref_):
        for bi, r in enumerate(tile_rows()):
            bf_ref_[bi] = r.astype(BF16)
            for hd in range(nh):
                f32_ref[bi, pl.ds(hd, tl, stride=nh), :] = r[:, hd * HEAD_DIM:(hd + 1) * HEAD_DIM]

    @pl.when(j == 0)
    def _():
        pos = start + li * tl + lax.broadcasted_iota(I32, (tl, 1), 0)
        for bi, u in enumerate(tile_rows()):
            full_ref[0:HALO, :] = jnp.where(li == 0, hist_ref[bi], carry_ref[bi])
            full_ref[HALO:HALO + tl, :] = u
            carry_ref[bi] = u[tl - HALO:, :]
            tail_ref[bi] = u[tl - HALO:, :]
            for g, w in enumerate(POOL_WINDOWS):
                c0, c1 = g * gw, (g + 1) * gw
                cur = full_ref[HALO:HALO + tl, c0:c1]
                win = cur
                for d in range(1, w):
                    win = win + full_ref[HALO - d:HALO - d + tl, c0:c1]
                dmean = win / jnp.minimum(pos + 1, w).astype(F32) - cur
                y = jnp.dot(dmean.astype(BF16), wp_ref[g], preferred_element_type=F32) * sp_ref[:, c0:c1]
                ya_ref[bi, :, c0:c1] = y.astype(BF16)

    @pl.when(j == 1)
    def _():
        for bi, r in enumerate(tile_rows()):
            q_ref[bi] = (r * q_scale).astype(BF16)

    @pl.when(j == 2)
    def _():
        store_heads(k32_ref, kb_ref)

    @pl.when(j == 3)
    def _():
        store_heads(v32_ref, vb_ref)

    @pl.when(j >= 4)
    def _():
        for bi, r in enumerate(tile_rows()):
            gates_ref[bi] = jax.nn.sigmoid(r).astype(BF16)


def _in_proj(x3, g, mod3, mod_row0, w_fl_t, b_f, w_cat, hist, w_pool, s_pool, start, q_scale, nb, tl):
    b, l, d = x3.shape
    nh = w_fl_t.shape[0]
    tn = nh * HEAD_DIM
    n_tiles = w_cat.shape[1] // tn
    n_gate_tiles = n_tiles - 4
    n_win, gw, _ = w_pool.shape
    assert mod_row0 % nb == 0 and b % nb == 0 and n_win * gw == tn and tl >= HALO
    tile = lambda width: pl.BlockSpec((nb, tl, width), lambda bb, li, j: (bb, li, 0))
    heads = pl.BlockSpec((nb, tl * nh, HEAD_DIM), lambda bb, li, j: (bb, li, 0))
    halo = pl.BlockSpec((nb, HALO, tn), lambda bb, li, j: (bb, 0, 0))
    const = lambda shape: pl.BlockSpec(shape, lambda bb, li, j: (0,) * len(shape))
    return pl.pallas_call(
        functools.partial(_in_proj_kernel, q_scale=q_scale, start=start),
        grid=(b // nb, l // tl, n_tiles),
        in_specs=[
            tile(d),
            const((1, d)),
            pl.BlockSpec((nb, N_MOD, d), lambda bb, li, j: (mod_row0 // nb + bb, 0, 0)),
            const((nh, d)),
            const((nh, 1)),
            pl.BlockSpec((d, tn), lambda bb, li, j: (0, j)),
            halo,
            const((n_win, gw, gw)),
            const((1, tn)),
        ],
        out_specs=[
            tile(tn), halo, tile(tn), heads, tile(tn), heads, tile(tn),
            pl.BlockSpec((nb, tl, tn), lambda bb, li, j: (bb, li, jnp.maximum(j - 4, 0))),
            pl.BlockSpec((nb, nh, tl), lambda bb, li, j: (bb, 0, li)),
        ],
        out_shape=[
            jax.ShapeDtypeStruct((b, l, tn), BF16),
            jax.ShapeDtypeStruct((b, HALO, tn), F32),
            jax.ShapeDtypeStruct((b, l, tn), BF16),
            jax.ShapeDtypeStruct((b, l * nh, HEAD_DIM), F32),
            jax.ShapeDtypeStruct((b, l, tn), BF16),
            jax.ShapeDtypeStruct((b, l * nh, HEAD_DIM), F32),
            jax.ShapeDtypeStruct((b, l, tn), BF16),
            jax.ShapeDtypeStruct((b, l, n_gate_tiles * tn), BF16),
            jax.ShapeDtypeStruct((b, nh, l), F32),
        ],
        scratch_shapes=[pltpu.VMEM((nb * tl, d), BF16), pltpu.VMEM((HALO + tl, tn), F32),
                        pltpu.VMEM((nb, HALO, tn), F32)],
        compiler_params=_params("parallel", "arbitrary", "arbitrary"),
        name="in_proj",
    )(x3, g.reshape(1, d), mod3, w_fl_t, b_f.reshape(nh, 1), w_cat, hist, w_pool, s_pool.reshape(1, tn))


def _mixer_out_kernel(ya_ref, yb_ref, ga_ref, gb_ref, x_ref, mod_ref, g_ref, wa_ref, wb_ref, wo_ref,
                      x1_ref, h2_ref, h2p_ref):
    merged = ga_ref[...].astype(F32) * jnp.dot(ya_ref[...], wa_ref[...], preferred_element_type=F32)
    merged = merged + gb_ref[...].astype(F32) * jnp.dot(yb_ref[...], wb_ref[...], preferred_element_type=F32)
    m = mod_ref[...]
    x1 = x_ref[...] + m[2:3, :] * jnp.dot(merged.astype(BF16), wo_ref[...], preferred_element_type=F32)
    x1_ref[...] = x1
    h2 = _modulated_norm(x1, g_ref[...], m, 3, 4)
    h2_ref[...] = h2.astype(BF16)
    _store_packed(h2p_ref, h2)


def _mixer_out(y_a, y_b, gates, x3, mod3, mod_row0, g_ffn, w_up_a, w_up_b, w_out, tl):
    b, l, d = x3.shape
    n_l = l // tl
    dp, da = y_a.shape[2], y_b.shape[2]
    tile = lambda width, col=0: pl.BlockSpec((None, tl, width), lambda bi, li: (bi, li, col))
    resident = lambda shape: pl.BlockSpec(shape, lambda bi, li: (0,) * len(shape), pipeline_mode=pl.Buffered(1))
    return pl.pallas_call(
        _mixer_out_kernel,
        grid=(b, n_l),
        in_specs=[
            tile(dp), tile(da), tile(d, 0), tile(d, 1), tile(d),
            pl.BlockSpec((None, N_MOD, d), lambda bi, li: (mod_row0 + bi, 0, 0)),
            pl.BlockSpec((1, d), lambda bi, li: (0, 0)),
            resident((dp, d)), resident((da, d)), resident((d, d)),
        ],
        out_specs=[
            tile(d), tile(d),
            pl.BlockSpec((tl * PACK_CHUNKS, LANES), lambda bi, li: (bi * n_l + li, 0)),
        ],
        out_shape=[
            jax.ShapeDtypeStruct((b, l, d), F32),
            jax.ShapeDtypeStruct((b, l, d), BF16),
            jax.ShapeDtypeStruct((b * l * PACK_CHUNKS, LANES), U32),
        ],
        compiler_params=_params("parallel", "parallel"),
        name="mixer_out",
    )(y_a, y_b, gates, gates, x3, mod3, g_ffn.reshape(1, d), w_up_a, w_up_b, w_out)


def _cumsum_kernel(x_ref, o_ref):
    x = x_ref[...]
    n = x.shape[1]
    idx = lax.broadcasted_iota(I32, x.shape, 1)
    s = 1
    while s < n:
        x = x + jnp.where(idx >= s, pltpu.roll(x, s, 1), 0.0)
        s *= 2
    o_ref[...] = x


def _cumsum_last(x3):
    b, h, n = x3.shape
    return pl.pallas_call(
        _cumsum_kernel,
        grid=(b,),
        in_specs=[pl.BlockSpec((None, h, n), lambda bi: (bi, 0, 0))],
        out_specs=pl.BlockSpec((None, h, n), lambda bi: (bi, 0, 0)),
        out_shape=jax.ShapeDtypeStruct((b, h, n), F32),
        compiler_params=_params("parallel"),
        name="cumsum_logf",
    )(x3)


def _attn_prompt_kernel(q_ref, k_ref, v_ref, fk_ref, o_ref, s0_ref, s1_ref, m_ref, l_ref, acc_ref, *, tq):
    qi = pl.program_id(2)
    tk = tq // 2
    n_chunks = tk // LANES

    def produce(j, s_ref, r0=0):
        off = pl.multiple_of(j * tk, tk)
        s = lax.dot_general(q_ref[r0:, :], k_ref[pl.ds(off, tk), :], NT_DIMS, preferred_element_type=F32)
        s_ref[r0:, :] = s - fk_ref[j] * LOG2_E

    def consume(j, s_ref, first_col=None, r0=0):
        s = s_ref[r0:, :]
        if first_col is not None:
            row = r0 + lax.broadcasted_iota(I32, s.shape, 0)
            col = first_col + lax.broadcasted_iota(I32, s.shape, 1)
            s = jnp.where(col <= row, s, NEG_INF)
        chunks = [s[:, c * LANES:(c + 1) * LANES] for c in range(n_chunks)]
        m_old = m_ref[r0:, :]
        m_new = jnp.maximum(m_old, jnp.max(functools.reduce(jnp.maximum, chunks), axis=1, keepdims=True))
        alpha = jnp.exp2(m_old - m_new)
        p_chunks = [jnp.exp2(ch - m_new) for ch in chunks]
        l_ref[r0:, :] = alpha * l_ref[r0:, :] + functools.reduce(lambda a, c: a + c, p_chunks)
        p = jnp.concatenate(p_chunks, axis=1).astype(BF16)
        v = v_ref[pl.ds(pl.multiple_of(j * tk, tk), tk), :]
        acc_ref[r0:, :] = alpha * acc_ref[r0:, :] + jnp.dot(p, v, preferred_element_type=F32)
        m_ref[r0:, :] = m_new

    m_ref[...] = jnp.full(m_ref.shape, NEG_INF, F32)
    l_ref[...] = jnp.zeros(l_ref.shape, F32)
    acc_ref[...] = jnp.zeros(acc_ref.shape, F32)
    produce(0, s0_ref)

    def pair(jj, carry):
        j0 = 2 * jj
        produce(j0 + 1, s1_ref)
        consume(j0, s0_ref)
        produce(j0 + 2, s0_ref)
        consume(j0 + 1, s1_ref)
        return carry

    lax.fori_loop(0, qi, pair, 0)
    produce(2 * qi + 1, s1_ref, r0=tk)
    consume(2 * qi, s0_ref, first_col=0)
    consume(2 * qi + 1, s1_ref, first_col=tk, r0=tk)

    l = jnp.sum(l_ref[...], axis=1, keepdims=True)
    o_ref[...] = (acc_ref[...] / l).astype(o_ref.dtype)


def _attn_prompt(q3, k3, v3, f3, tq):
    b, l, da = q3.shape
    nh = da // HEAD_DIM
    tk = tq // 2
    nk = l // tk
    fk = f3.reshape(b * nh, nk, 1, tk)
    return pl.pallas_call(
        functools.partial(_attn_prompt_kernel, tq=tq),
        grid=(b, nh, l // tq),
        in_specs=[
            pl.BlockSpec((None, tq, HEAD_DIM), lambda bi, hi, qi: (bi, qi, hi)),
            pl.BlockSpec((None, l, HEAD_DIM), lambda bi, hi, qi: (bi, 0, hi)),
            pl.BlockSpec((None, l, HEAD_DIM), lambda bi, hi, qi: (bi, 0, hi)),
            pl.BlockSpec((None, nk, 1, tk), lambda bi, hi, qi: (bi * nh + hi, 0, 0, 0)),
        ],
        out_specs=pl.BlockSpec((None, tq, HEAD_DIM), lambda bi, hi, qi: (bi, qi, hi)),
        out_shape=jax.ShapeDtypeStruct((b, l, da), BF16),
        scratch_shapes=[
            pltpu.VMEM((tq, tk), F32),
            pltpu.VMEM((tq, tk), F32),
            pltpu.VMEM((tq, LANES), F32),
            pltpu.VMEM((tq, LANES), F32),
            pltpu.VMEM((tq, HEAD_DIM), F32),
        ],
        compiler_params=_params("parallel", "parallel", "parallel"),
        name="fox_prompt",
    )(q3, k3, v3, fk)


def _attn_sample_kernel(q_ref, kp_ref, vp_ref, kn_ref, vn_ref, fp_ref, fn_ref, o_ref, m_ref, l_ref, acc_ref, *,
                        nh, pc, lq):
    c = pl.program_id(1)

    @pl.when(c == 0)
    def _():
        m_ref[...] = jnp.full(m_ref.shape, NEG_INF, F32)
        l_ref[...] = jnp.zeros(l_ref.shape, F32)
        acc_ref[...] = jnp.zeros(acc_ref.shape, F32)

    for h in range(nh):
        cols = slice(h * HEAD_DIM, (h + 1) * HEAD_DIM)
        q = q_ref[:, cols]
        k = kp_ref[pl.ds(h, pc, stride=nh), :].astype(BF16)
        v = vp_ref[pl.ds(h, pc, stride=nh), :].astype(BF16)
        s = lax.dot_general(q, k, NT_DIMS, preferred_element_type=F32) - fp_ref[h:h + 1, :]
        chunks = [s[:, j * LANES:(j + 1) * LANES] for j in range(pc // LANES)]
        m_old = m_ref[h]
        m_new = jnp.maximum(m_old, jnp.max(functools.reduce(jnp.maximum, chunks), axis=1, keepdims=True))
        alpha = jnp.exp(m_old - m_new)
        p_chunks = [jnp.exp(ch - m_new) for ch in chunks]
        l_ref[h] = alpha * l_ref[h] + functools.reduce(lambda a, b: a + b, p_chunks)
        p = jnp.concatenate(p_chunks, axis=1).astype(BF16)
        acc_ref[h] = alpha * acc_ref[h] + jnp.dot(p, v, preferred_element_type=F32)
        m_ref[h] = m_new

    @pl.when(c == pl.num_programs(1) - 1)
    def _():
        row = lax.broadcasted_iota(I32, (lq, lq), 0)
        col = lax.broadcasted_iota(I32, (lq, lq), 1)
        for h in range(nh):
            cols = slice(h * HEAD_DIM, (h + 1) * HEAD_DIM)
            s = lax.dot_general(q_ref[:, cols], kn_ref[:, cols], NT_DIMS, preferred_element_type=F32)
            s = jnp.where(col <= row, s - fn_ref[h:h + 1, :lq], NEG_INF)
            m_old = m_ref[h]
            m_new = jnp.maximum(m_old, jnp.max(s, axis=1, keepdims=True))
            alpha = jnp.exp(m_old - m_new)
            p = jnp.exp(s - m_new[:, :lq])
            l = jnp.sum(alpha * l_ref[h], axis=1, keepdims=True) + jnp.sum(p, axis=1, keepdims=True)
            acc = alpha * acc_ref[h] + jnp.dot(p.astype(BF16), vn_ref[:, cols], preferred_element_type=F32)
            o_ref[:, cols] = (acc / l).astype(o_ref.dtype)


def _attn_sample(q3, kn3, vn3, cache_k, cache_v, f_past, f_new, pc):
    b, lq, da = q3.shape
    _, past, nh, dh = cache_k.shape
    nc = past // pc
    new_spec = pl.BlockSpec((None, lq, da), lambda bi, ci: (bi, 0, 0))
    past_spec = pl.BlockSpec((None, pc * nh, dh), lambda bi, ci: (bi, ci, 0))
    return pl.pallas_call(
        functools.partial(_attn_sample_kernel, nh=nh, pc=pc, lq=lq),
        grid=(b, nc),
        in_specs=[new_spec, past_spec, past_spec, new_spec, new_spec,
                  pl.BlockSpec((None, None, nh, pc), lambda bi, ci: (bi, ci, 0, 0)),
                  pl.BlockSpec((None, nh, LANES), lambda bi, ci: (bi, 0, 0))],
        out_specs=new_spec,
        out_shape=jax.ShapeDtypeStruct((b, lq, da), BF16),
        scratch_shapes=[pltpu.VMEM((nh, lq, LANES), F32), pltpu.VMEM((nh, lq, LANES), F32),
                        pltpu.VMEM((nh, lq, HEAD_DIM), F32)],
        compiler_params=_params("parallel", "arbitrary"),
        name="fox_sample",
    )(q3, cache_k.reshape(b, past * nh, dh), cache_v.reshape(b, past * nh, dh), kn3, vn3, f_past, f_new)


def _router_kernel(h_ref, wr_ref, br_ref, tri_ref, cnt0_ref, idx_ref, gate_ref, rank_ref, cnt_ref, carry_ref):
    @pl.when(pl.program_id(0) == 0)
    def _():
        carry_ref[...] = cnt0_ref[...]

    n_exp, tm = wr_ref.shape[0], h_ref.shape[0]
    scores = jax.nn.sigmoid(lax.dot_general(wr_ref[...], h_ref[...], NT_DIMS, preferred_element_type=F32))
    work = scores + br_ref[...]
    eidx = lax.broadcasted_iota(I32, (n_exp, tm), 0)
    picked = jnp.zeros((n_exp, tm), F32)
    onehots, gates = [], []
    for k in range(TOP_K):
        best = jnp.max(work, axis=0, keepdims=True)
        first = jnp.min(jnp.where(work == best, eidx, n_exp), axis=0, keepdims=True)
        onehot = eidx == first
        idx_ref[k:k + 1, :] = first
        gates.append(jnp.sum(jnp.where(onehot, scores, 0.0), axis=0, keepdims=True))
        onehots.append(onehot)
        picked = picked + jnp.where(onehot, 1.0, 0.0)
        work = jnp.where(onehot, NEG_INF, work)
    norm = ROUTED_SCALE / functools.reduce(lambda a, c: a + c, gates)
    cum = jnp.dot(picked.astype(BF16), tri_ref[...], preferred_element_type=F32)
    before = cum - picked + carry_ref[:, 0:1]
    for k in range(TOP_K):
        gate_ref[k:k + 1, :] = gates[k] * norm
        rank_ref[k:k + 1, :] = jnp.sum(jnp.where(onehots[k], before, 0.0), axis=0, keepdims=True).astype(I32)
    carry_ref[...] = carry_ref[...] + cum[:, tm - 1:tm]
    cnt_ref[...] = carry_ref[...]


def _router(h2, tok0, ntok, w_router_t, b_router, cnt0, tm):
    d = h2.shape[1]
    n_exp = w_router_t.shape[0]
    tri = jnp.triu(jnp.ones((tm, tm), BF16))
    tok_spec = pl.BlockSpec((TOP_K, tm), lambda i: (0, i))
    cnt_spec = pl.BlockSpec((n_exp, LANES), lambda i: (0, 0))
    return pl.pallas_call(
        _router_kernel,
        grid=(ntok // tm,),
        in_specs=[
            pl.BlockSpec((tm, d), lambda i: (tok0 // tm + i, 0)),
            pl.BlockSpec((n_exp, d), lambda i: (0, 0)),
            pl.BlockSpec((n_exp, 1), lambda i: (0, 0)),
            pl.BlockSpec((tm, tm), lambda i: (0, 0)),
            cnt_spec,
        ],
        out_specs=[tok_spec, tok_spec, tok_spec, cnt_spec],
        out_shape=[
            jax.ShapeDtypeStruct((TOP_K, ntok), I32),
            jax.ShapeDtypeStruct((TOP_K, ntok), F32),
            jax.ShapeDtypeStruct((TOP_K, ntok), I32),
            jax.ShapeDtypeStruct((n_exp, LANES), F32),
        ],
        scratch_shapes=[pltpu.VMEM((n_exp, LANES), F32)],
        compiler_params=_params("arbitrary"),
        name="moe_router",
    )(h2, w_router_t, b_router.reshape(n_exp, 1), tri, cnt0)


LAYOUT_ROWS = 8


def _layout_kernel(cnt_ref, o_ref):
    n_exp = cnt_ref.shape[0]
    lane = lax.broadcasted_iota(I32, o_ref.shape, 1)
    row0 = lane * MOE_ROWS

    def forward(e, carry):
        start, blk_e, valid, pad_start = carry
        cnt = cnt_ref[e]
        end = start + (cnt + MOE_ROWS - 1) // MOE_ROWS * MOE_ROWS
        in_span = lambda new, old: jnp.where(row0 >= start, jnp.where(row0 < end, new, old), old)
        blk_e = in_span(e, blk_e)
        valid = in_span(jnp.clip(start + cnt - row0, 0, MOE_ROWS), valid)
        return end, blk_e, valid, jnp.where(lane == e, start, pad_start)

    zeros = jnp.zeros(o_ref.shape, I32)
    end, blk_e, valid, pad_start = lax.fori_loop(0, n_exp, forward, (jnp.int32(0), zeros + (n_exp - 1), zeros, zeros))

    def backward(t, carry):
        nxt, blk_next = carry
        e = n_exp - 1 - t
        return jnp.where(cnt_ref[e] > 0, e, nxt), jnp.where(blk_e == e, nxt, blk_next)

    _, blk_next = lax.fori_loop(0, n_exp, backward, (jnp.int32(-1), zeros - 1))
    table = zeros + end // MOE_ROWS
    sub = lax.broadcasted_iota(I32, o_ref.shape, 0)
    for r, row in enumerate((blk_e, valid, blk_next, pad_start)):
        table = jnp.where(sub == r, row, table)
    o_ref[...] = table


def _layout(counts, n_blocks):
    n_exp = counts.shape[0]
    w = -(-max(n_blocks, n_exp) // LANES) * LANES
    table = pl.pallas_call(
        _layout_kernel,
        grid_spec=pltpu.PrefetchScalarGridSpec(
            num_scalar_prefetch=1, grid=(1,), in_specs=[],
            out_specs=pl.BlockSpec((LAYOUT_ROWS, w), lambda i, cnt: (0, 0))),
        out_shape=jax.ShapeDtypeStruct((LAYOUT_ROWS, w), I32),
        compiler_params=_params("arbitrary"),
        name="moe_layout",
    )(counts)
    return table[0, :n_blocks], table[1, :n_blocks], table[2, :n_blocks], table[3, :n_exp], table[4, :1]


def _slot_rows_kernel(ps_ref, idx_ref, rank_ref, o_ref):
    idx = idx_ref[...]
    rows = rank_ref[...]
    for e in range(ps_ref.shape[0]):
        rows = rows + jnp.where(idx == e, ps_ref[e], 0)
    o_ref[...] = rows


def _slot_rows(pad_start, idx_t, rank_t):
    k, ntok = idx_t.shape
    tm = min(2048, ntok)
    spec = pl.BlockSpec((k, tm), lambda i, ps: (0, i))
    return pl.pallas_call(
        _slot_rows_kernel,
        grid_spec=pltpu.PrefetchScalarGridSpec(num_scalar_prefetch=1, grid=(ntok // tm,), in_specs=[spec, spec],
                                               out_specs=spec),
        out_shape=jax.ShapeDtypeStruct((k, ntok), I32),
        compiler_params=_params("parallel"),
        name="moe_slot_rows",
    )(pad_start, idx_t, rank_t)


def _moe_kernel(be_ref, bv_ref, nx_ref, nu_ref, x_ref, wg_hbm, wu_hbm, wd_hbm, y_ref,
                wg_f32, wu_f32, wd_f32, wg_bf, wu_bf, wd_bf, sems):
    i = pl.program_id(0)
    active = i < nu_ref[0]
    expert = be_ref[i]

    def fetch(e):
        pairs = ((wg_hbm, wg_f32), (wu_hbm, wu_f32), (wd_hbm, wd_f32))
        return [pltpu.make_async_copy(w.at[e], buf, sems.at[n]) for n, (w, buf) in enumerate(pairs)]

    @pl.when(active & (i == 0))
    def _():
        for c in fetch(expert):
            c.start()

    @pl.when(active & ((i == 0) | (expert != be_ref[jnp.maximum(i - 1, 0)])))
    def _():
        for c in fetch(expert):
            c.wait()
        wg_bf[...] = wg_f32[...].astype(BF16)
        wu_bf[...] = wu_f32[...].astype(BF16)
        wd_bf[...] = wd_f32[...].astype(BF16)

        @pl.when(nx_ref[i] >= 0)
        def _():
            for c in fetch(nx_ref[i]):
                c.start()

    def chain(row0):
        x = _load_packed(x_ref, MOE_CHAIN_ROWS, valid=bv_ref[i], row0=row0).astype(BF16)
        g = jnp.dot(x, wg_bf[...], preferred_element_type=F32)
        u = jnp.dot(x, wu_bf[...], preferred_element_type=F32)
        a = (_silu(g) * u).astype(BF16)
        _store_packed(y_ref, jnp.dot(a, wd_bf[...], preferred_element_type=F32), row0=row0)

    starts = range(0, MOE_ROWS, MOE_CHAIN_ROWS)
    full = bv_ref[i] > starts[-1]

    @pl.when(active & full)
    def _():
        for row0 in starts:
            chain(row0)

    for row0 in starts[:-1]:
        pl.when(active & jnp.logical_not(full) & (bv_ref[i] > row0))(functools.partial(chain, row0))


def _moe_experts(x_sorted, blk_e, blk_valid, blk_next, n_used, w_e_gate, w_e_up, w_e_down):
    cap = x_sorted.shape[0] // PACK_CHUNKS
    d, de = w_e_gate.shape[1:]
    n_blocks = cap // MOE_ROWS
    row_map = lambda i, be, bv, nx, nu: (jnp.minimum(i, nu[0] - 1), 0)
    hbm = pl.BlockSpec(memory_space=pl.ANY)
    return pl.pallas_call(
        _moe_kernel,
        grid_spec=pltpu.PrefetchScalarGridSpec(
            num_scalar_prefetch=4,
            grid=(n_blocks,),
            in_specs=[pl.BlockSpec((MOE_ROWS * PACK_CHUNKS, LANES), row_map), hbm, hbm, hbm],
            out_specs=pl.BlockSpec((MOE_ROWS * PACK_CHUNKS, LANES), row_map),
            scratch_shapes=[
                pltpu.VMEM((d, de), F32), pltpu.VMEM((d, de), F32), pltpu.VMEM((de, d), F32),
                pltpu.VMEM((d, de), BF16), pltpu.VMEM((d, de), BF16), pltpu.VMEM((de, d), BF16),
                pltpu.SemaphoreType.DMA((3,)),
            ],
        ),
        out_shape=jax.ShapeDtypeStruct(x_sorted.shape, U32),
        compiler_params=_params("arbitrary"),
        name="moe_experts",
    )(blk_e, blk_valid, blk_next, n_used, x_sorted, w_e_gate, w_e_up, w_e_down)


def _sc_mesh():
    return plsc.VectorSubcoreMesh(core_axis_name="c", subcore_axis_name="s")


def _sc_worker_id():
    return lax.axis_index("s") * SC_CORES + lax.axis_index("c")


def _sc_scatter_rows(segments, n_out):
    row_shape = segments[0][0].shape[1:]
    dtype = segments[0][0].dtype
    n_seg = len(segments)
    geom = [(tok0, dw.shape[0] // SC_WORKERS, dw.shape[2]) for _, tok0, dw in segments]
    scratch = []
    for _, n_win, win in geom:
        scratch += [pltpu.VMEM((n_win, TOP_K, win), I32), pltpu.VMEM((win,) + row_shape, dtype)]

    @functools.partial(
        pl.kernel, mesh=_sc_mesh(),
        out_type=jax.ShapeDtypeStruct((n_out,) + row_shape, dtype),
        scratch_types=scratch + [pltpu.SemaphoreType.DMA],
    )
    def scatter_kernel(*refs):
        out_hbm = refs[2 * n_seg]
        sem = refs[-1]
        for s, (tok0, n_win, win) in enumerate(geom):
            rows_hbm, idx_hbm = refs[2 * s], refs[2 * s + 1]
            idx_v, rows_v = refs[2 * n_seg + 1 + 2 * s], refs[2 * n_seg + 2 + 2 * s]
            first = _sc_worker_id() * n_win
            pltpu.sync_copy(idx_hbm.at[pl.ds(first, n_win)], idx_v)

            @pl.loop(0, n_win)
            def _(w):
                pltpu.sync_copy(rows_hbm.at[pl.ds(tok0 + (first + w) * win, win)], rows_v)
                copies = [pltpu.make_async_copy(rows_v, out_hbm.at[idx_v.at[w, k]], sem) for k in range(TOP_K)]
                for c in copies:
                    c.start()
                for c in copies:
                    c.wait()

    args = []
    for rows, _, dest_win in segments:
        args += [rows, dest_win]
    return scatter_kernel(*args)


def _sc_gather_rows(table, idx_wins):
    row_shape = table.shape[1:]
    n_seg = len(idx_wins)
    geom = [(iw.shape[0] // SC_WORKERS, iw.shape[1]) for iw in idx_wins]
    scratch = []
    for n_win, win in geom:
        scratch += [pltpu.VMEM((n_win, win), I32), pltpu.VMEM((win,) + row_shape, table.dtype)]

    @functools.partial(
        pl.kernel, mesh=_sc_mesh(),
        out_type=[jax.ShapeDtypeStruct((iw.shape[0] * iw.shape[1],) + row_shape, table.dtype) for iw in idx_wins],
        scratch_types=scratch + [pltpu.SemaphoreType.DMA],
    )
    def gather_kernel(*refs):
        table_hbm = refs[0]
        sem = refs[-1]
        for s, (n_win, win) in enumerate(geom):
            idx_hbm, out_hbm = refs[1 + s], refs[1 + n_seg + s]
            idx_v, rows_v = refs[1 + 2 * n_seg + 2 * s], refs[2 + 2 * n_seg + 2 * s]
            first = _sc_worker_id() * n_win
            pltpu.sync_copy(idx_hbm.at[pl.ds(first, n_win)], idx_v)

            @pl.loop(0, n_win)
            def _(w):
                pltpu.async_copy(table_hbm.at[idx_v.at[w]], rows_v, sem).wait()
                pltpu.sync_copy(rows_v, out_hbm.at[pl.ds((first + w) * win, win)])

    return gather_kernel(table, *idx_wins)


def _final_kernel(x1_ref, h2_ref, yt_ref, gate_ref, mod_ref, wsg_ref, wsu_ref, wsd_ref, gf_ref, *rest):
    o_ref = rest[-1]
    h2 = h2_ref[...]
    g = jnp.dot(h2, wsg_ref[...], preferred_element_type=F32)
    u = jnp.dot(h2, wsu_ref[...], preferred_element_type=F32)
    f = jnp.dot((_silu(g) * u).astype(BF16), wsd_ref[...], preferred_element_type=F32)
    gate = gate_ref[...]
    tl = h2.shape[0]
    for k in range(TOP_K):
        f = f + gate[:, k:k + 1] * _load_packed(yt_ref.at[k], tl)
    x2 = x1_ref[...] + mod_ref[N_MOD - 1:N_MOD, :] * f
    ms = jnp.mean(x2 * x2, axis=-1, keepdims=True)
    o_ref[...] = x2 * lax.rsqrt(ms + EPS) * gf_ref[...]


def _final(x1, h2, b0, y_tok, gate, mod3, mod_row0, w_s_gate, w_s_up, w_s_down, g_final, tl, y_prev=None):
    b, l, d = x1.shape
    nb = gate.shape[0]
    ds = w_s_gate.shape[1]
    tile = pl.BlockSpec((None, tl, d), lambda bi, li: (b0 + bi, li, 0))
    const = lambda shape: pl.BlockSpec(shape, lambda bi, li: (0,) * len(shape), pipeline_mode=pl.Buffered(1))
    in_specs = [
        tile,
        tile,
        pl.BlockSpec((TOP_K, None, tl * PACK_CHUNKS, LANES), lambda bi, li: (0, bi, li, 0)),
        pl.BlockSpec((None, tl, TOP_K), lambda bi, li: (bi, li, 0)),
        pl.BlockSpec((None, N_MOD, d), lambda bi, li: (mod_row0 + b0 + bi, 0, 0)),
        const((d, ds)),
        const((d, ds)),
        const((ds, d)),
        const((1, d)),
    ]
    args = [x1, h2, y_tok, gate, mod3, w_s_gate, w_s_up, w_s_down, g_final.reshape(1, d)]
    aliases = {}
    if y_prev is not None:
        in_specs.append(pl.BlockSpec(memory_space=pl.ANY))
        aliases = {len(args): 0}
        args.append(y_prev)
    return pl.pallas_call(
        _final_kernel,
        grid=(nb, l // tl),
        in_specs=in_specs,
        out_specs=tile,
        out_shape=jax.ShapeDtypeStruct((b, l, d), F32),
        input_output_aliases=aliases,
        compiler_params=_params("parallel", "parallel"),
        name="combine_final",
    )(*args)


def _group_mixer(x3, mod3, mod_row0, hist, start, past, wts, tiles):
    b, l, d = x3.shape
    tl = tiles["tl"]
    dp = wts["w_pool"].shape[0] * wts["w_pool"].shape[1]
    da = wts["w_up_b"].shape[0]
    nh = da // HEAD_DIM
    t = b * l

    assert dp == da, "u, q, k, v must each be one column tile of the input projection"
    scale = HEAD_DIM ** -0.5 * (LOG2_E if past is None else 1.0)
    y_a, u_tail, qb, k, kb, v, vb, gates, logf_t = _in_proj(
        x3, wts["g_mix"], mod3, mod_row0, wts["w_fl_t"], wts["b_f"], wts["w_cat"], hist, wts["w_pool"],
        wts["s_pool"], start, scale, tiles["nb_in"], tl)

    if past is None:
        y_b = _attn_prompt(qb, kb, vb, _cumsum_last(logf_t), tiles["tq"])
    else:
        cache_k, cache_v, cache_logf = past
        p = cache_k.shape[1]
        pc = tiles["past_chunk"]
        assert l <= LANES and p % pc == 0 and p % LANES == 0
        lf_all = jnp.concatenate([jnp.swapaxes(cache_logf.astype(F32), 1, 2), logf_t,
                                  jnp.zeros((b, nh, LANES - l), F32)], axis=2)
        f_all = _cumsum_last(lf_all)
        f_past = f_all[:, :, :p].reshape(b, nh, p // pc, pc).transpose(0, 2, 1, 3)
        y_b = _attn_sample(qb, kb, vb, cache_k, cache_v, f_past, f_all[:, :, p:], pc)

    x1, h2, h2_packed = _mixer_out(y_a, y_b, gates, x3, mod3, mod_row0, wts["g_ffn"], wts["w_up_a"], wts["w_up_b"],
                                   wts["w_out"], tiles["tl_out"])
    caches = (u_tail[:, HALO - POOL_HIST:, :], k.reshape(b, l, nh, HEAD_DIM), v.reshape(b, l, nh, HEAD_DIM),
              jnp.swapaxes(logf_t, 1, 2))
    return dict(x1=x1, h2=h2, h2_packed=h2_packed, mod_row0=mod_row0, tiles=tiles), caches


def _moe_chunk(segments, wts):
    n_exp = wts["w_router_t"].shape[0]
    row_shape = (PACK_CHUNKS, LANES)
    counts_f = jnp.zeros((n_exp, LANES), F32)
    routed = []
    for grp, b0, nb in segments:
        b, l, d = grp["h2"].shape
        ntok = nb * l
        idx_t, gate_t, rank_t, counts_f = _router(grp["h2"].reshape(b * l, d), b0 * l, ntok, wts["w_router_t"],
                                                  wts["b_router"], counts_f, min(grp["tiles"]["t_route"], ntok))
        routed.append((idx_t, gate_t, rank_t, ntok))
    counts = counts_f[:, 0].astype(I32)

    n_slots = sum(r[3] for r in routed) * TOP_K
    n_blocks = -(-n_slots // MOE_ROWS) + n_exp
    cap = n_blocks * MOE_ROWS
    blk_e, blk_valid, blk_next, pad_start, n_used = _layout(counts, n_blocks)

    scatter_segs, gather_idx = [], []
    for (grp, b0, nb), (idx_t, _, rank_t, ntok) in zip(segments, routed):
        b, l, _ = grp["h2"].shape
        dest = _slot_rows(pad_start, idx_t, rank_t)
        win_d = min(SC_WINDOW, ntok // SC_WORKERS)
        dest_win = dest.reshape(TOP_K, ntok // win_d, win_d).transpose(1, 0, 2)
        scatter_segs.append((grp["h2_packed"].reshape((b * l,) + row_shape), b0 * l, dest_win))
        win_c = min(SC_WINDOW, ntok * TOP_K // SC_WORKERS)
        gather_idx.append(dest.reshape(ntok * TOP_K // win_c, win_c))

    x_sorted = _sc_scatter_rows(scatter_segs, cap)
    y_sorted = _moe_experts(x_sorted.reshape(cap * PACK_CHUNKS, LANES), blk_e, blk_valid, blk_next, n_used,
                            wts["w_e_gate"], wts["w_e_up"], wts["w_e_down"])
    y_toks = _sc_gather_rows(y_sorted.reshape((cap,) + row_shape), gather_idx)
    out = []
    for (grp, b0, nb), (_, gate_t, _, _), y_tok in zip(segments, routed, y_toks):
        l = grp["h2"].shape[1]
        out.append((y_tok.reshape(TOP_K, nb, l * PACK_CHUNKS, LANES), gate_t.T.reshape(nb, l, TOP_K)))
    return out


def kernel(x_prompt, x_sample, cache_pool, cache_k, cache_v, cache_logf, c_prompt, c_sample, w_ada, b_ada, g_mix, w_in, b_f, w_pool, s_pool, w_up_a, w_up_b, w_out, g_ffn, w_router, b_router, w_e_gate, w_e_up, w_e_down, w_s_gate, w_s_up, w_s_down, g_final):
    depth = w_ada.shape[0]
    assert depth == 1, "a single layer is supported"
    bp, lp, d = x_prompt.shape
    bs, ls, _ = x_sample.shape
    dp = w_pool.shape[1] * w_pool.shape[2]
    da = w_up_b.shape[1]
    nh = da // HEAD_DIM
    assert bp <= MOD_ROWS_SAMPLE

    one = lambda a: a.reshape(a.shape[1:])
    c_all = jnp.zeros((MOD_ROWS_SAMPLE + bs, d), F32).at[:bp].set(c_prompt).at[MOD_ROWS_SAMPLE:].set(c_sample)
    mod3 = _ada_mod(c_all, one(w_ada), one(b_ada)).reshape(c_all.shape[0], N_MOD, d)

    n_main = dp + 3 * da
    w_in1 = one(w_in)
    wts = dict(
        g_mix=one(g_mix), g_ffn=one(g_ffn), b_f=one(b_f), s_pool=one(s_pool), b_router=one(b_router),
        w_cat=jnp.concatenate([w_in1[:, :n_main], w_in1[:, n_main + nh:]], axis=1).astype(BF16),
        w_fl_t=w_in1[:, n_main:n_main + nh].T.astype(BF16),
        w_pool=one(w_pool).astype(BF16),
        w_up_a=one(w_up_a).astype(BF16), w_up_b=one(w_up_b).astype(BF16), w_out=one(w_out).astype(BF16),
        w_router_t=one(w_router).T.astype(BF16),
        w_e_gate=one(w_e_gate), w_e_up=one(w_e_up), w_e_down=one(w_e_down),
        w_s_gate=one(w_s_gate).astype(BF16), w_s_up=one(w_s_up).astype(BF16), w_s_down=one(w_s_down).astype(BF16),
    )

    tiles_p = dict(tl=min(512, lp), tq=min(1024, lp), t_route=min(512, bp * lp), tl_out=min(256, lp),
                   tl_final=min(256, lp), nb_in=1)
    tiles_s = dict(tl=ls, t_route=bs * ls, tl_out=ls, tl_final=ls, nb_in=bs, past_chunk=min(512, cache_k.shape[2]))

    hist_p = jnp.zeros((bp, HALO, dp), F32)
    grp_p, caches_p = _group_mixer(x_prompt, mod3, MOD_ROWS_PROMPT, hist_p, 0, None, wts, tiles_p)
    hist_s = jnp.pad(one(cache_pool), ((0, 0), (HALO - POOL_HIST, 0), (0, 0)))
    past = (one(cache_k), one(cache_v), one(cache_logf))
    grp_s, caches_s = _group_mixer(x_sample, mod3, MOD_ROWS_SAMPLE, hist_s, cache_k.shape[2], past, wts, tiles_s)

    n_chunks = MOE_CHUNKS if bp % MOE_CHUNKS == 0 else 1
    nb_c = bp // n_chunks
    chunks = [[(grp_p, c * nb_c, nb_c)] for c in range(n_chunks)]
    chunks[-1].append((grp_s, 0, bs))
    outs = {id(grp_p): None, id(grp_s): None}
    for segments in chunks:
        for (grp, b0, nb), (y_tok, gate) in zip(segments, _moe_chunk(segments, wts)):
            outs[id(grp)] = _final(grp["x1"], grp["h2"], b0, y_tok, gate, mod3, grp["mod_row0"], wts["w_s_gate"],
                                   wts["w_s_up"], wts["w_s_down"], g_final, grp["tiles"]["tl_final"],
                                   y_prev=outs[id(grp)])
    stack = lambda a: a[None]
    return (outs[id(grp_p)], outs[id(grp_s)], *map(stack, caches_p), *map(stack, caches_s))
```

```python
import functools

import jax
import jax.numpy as jnp
from jax import lax
from jax.experimental import pallas as pl
from jax.experimental.pallas import tpu as pltpu
from jax.experimental.pallas import tpu_sc as plsc

F32 = jnp.float32
BF16 = jnp.bfloat16
I32 = jnp.int32
U32 = jnp.uint32

EPS = 1e-6
N_MOD = 6
POOL_WINDOWS = (2, 4, 8, 16)
POOL_HIST = max(POOL_WINDOWS) - 1
HALO = 16
HEAD_DIM = 128
TOP_K = 8
ROUTED_SCALE = 2.5
MOE_ROWS = 1024
MOE_CHAIN_ROWS = 256
MOE_CHUNKS = 2
MOD_ROWS_PROMPT = 0
MOD_ROWS_SAMPLE = 8
VMEM_LIMIT_BYTES = 52 * 1024 * 1024
NEG_INF = float("-inf")
LOG2_E = 1.4426950408889634
LANES = 128
PACK_CHUNKS = 8
SC_CORES = 2
SC_SUBCORES = 16
SC_WORKERS = SC_CORES * SC_SUBCORES
SC_WINDOW = 32
NT_DIMS = (((1,), (1,)), ((), ()))


def _params(*semantics):
    return pltpu.CompilerParams(dimension_semantics=semantics, vmem_limit_bytes=VMEM_LIMIT_BYTES)


def _silu(x):
    return x * jax.nn.sigmoid(x)


def _store_packed(ref, x, row0=0):
    rows, width = x.shape
    half = width // 2
    bits = pltpu.bitcast(x.astype(BF16).astype(F32), U32)
    packed = (bits[:, :half] >> 16) | (bits[:, half:] & jnp.uint32(0xFFFF0000))
    for j in range(PACK_CHUNKS):
        ref[pl.ds(row0 * PACK_CHUNKS + j, rows, stride=PACK_CHUNKS), :] = packed[:, j * LANES:(j + 1) * LANES]


def _load_packed(ref, rows, valid=None, row0=0):
    lo, hi = [], []
    for j in range(PACK_CHUNKS):
        p = ref[pl.ds(row0 * PACK_CHUNKS + j, rows, stride=PACK_CHUNKS), :]
        if valid is not None:
            p = jnp.where(row0 + lax.broadcasted_iota(I32, p.shape, 0) < valid, p, jnp.uint32(0))
        lo.append(pltpu.bitcast(p << 16, F32))
        hi.append(pltpu.bitcast(p & jnp.uint32(0xFFFF0000), F32))
    return jnp.concatenate(lo + hi, axis=1)


def _ada_kernel(c_ref, w_ref, b_ref, o_ref):
    a = _silu(c_ref[...]).astype(BF16)
    o_ref[...] = jnp.dot(a, w_ref[...].astype(BF16), preferred_element_type=F32) + b_ref[...]


def _ada_mod(c_all, w_ada, b_ada):
    rows, d = c_all.shape
    n = w_ada.shape[1]
    tn = 1024
    return pl.pallas_call(
        _ada_kernel,
        grid=(n // tn,),
        in_specs=[
            pl.BlockSpec((rows, d), lambda j: (0, 0)),
            pl.BlockSpec((d, tn), lambda j: (0, j)),
            pl.BlockSpec((1, tn), lambda j: (0, j)),
        ],
        out_specs=pl.BlockSpec((rows, tn), lambda j: (0, j)),
        out_shape=jax.ShapeDtypeStruct((rows, n), F32),
        compiler_params=_params("parallel"),
        name="ada_mod",
    )(c_all, w_ada, b_ada.reshape(1, n))


def _modulated_norm(x, g, m, shift_row, scale_row):
    ms = jnp.mean(x * x, axis=-1, keepdims=True)
    return x * lax.rsqrt(ms + EPS) * g * (1.0 + m[scale_row:scale_row + 1, :]) + m[shift_row:shift_row + 1, :]


def _in_proj_kernel(x_ref, g_ref, mod_ref, wfl_ref, bf_ref, w_ref, hist_ref, wp_ref, sp_ref,
                    ya_ref, tail_ref, q_ref, k32_ref, kb_ref, v32_ref, vb_ref, gates_ref, lf_ref,
                    h_ref, full_ref, carry_ref, *, q_scale, start):
    li = pl.program_id(1)
    j = pl.program_id(2)
    nb, tl, _ = x_ref.shape
    nh = wfl_ref.shape[0]
    gw = wp_ref.shape[1]

    @pl.when(j == 0)
    def _():
        for bi in range(nb):
            hb = _modulated_norm(x_ref[bi], g_ref[...], mod_ref[bi], 0, 1).astype(BF16)
            h_ref[bi * tl:(bi + 1) * tl, :] = hb
            fl = lax.dot_general(wfl_ref[...], hb, NT_DIMS, preferred_element_type=F32) + bf_ref[...]
            lf_ref[bi] = jnp.minimum(fl, 0.0) - jnp.log(1.0 + jnp.exp(-jnp.abs(fl)))

    def tile_rows():
        acc = jnp.dot(h_ref[...], w_ref[...], preferred_element_type=F32)
        return [acc[bi * tl:(bi + 1) * tl, :] for bi in range(nb)]

    def store_heads(f32_ref, bf_ref_):
        for bi, r in enumerate(tile_rows()):
            bf_ref_[bi] = r.astype(BF16)
            for hd in range(nh):
                f32_ref[bi, pl.ds(hd, tl, stride=nh), :] = r[:, hd * HEAD_DIM:(hd + 1) * HEAD_DIM]

    @pl.when(j == 0)
    def _():
        pos = start + li * tl + lax.broadcasted_iota(I32, (tl, 1), 0)
        for bi, u in enumerate(tile_rows()):
            full_ref[0:HALO, :] = jnp.where(li == 0, hist_ref[bi], carry_ref[bi])
            full_ref[HALO:HALO + tl, :] = u
            carry_ref[bi] = u[tl - HALO:, :]
            tail_ref[bi] = u[tl - HALO:, :]
            for g, w in enumerate(POOL_WINDOWS):
                c0, c1 = g * gw, (g + 1) * gw
                cur = full_ref[HALO:HALO + tl, c0:c1]
                win = cur
                for d in range(1, w):
                    win = win + full_ref[HALO - d:HALO - d + tl, c0:c1]
                dmean = win / jnp.minimum(pos + 1, w).astype(F32) - cur
                y = jnp.dot(dmean.astype(BF16), wp_ref[g], preferred_element_type=F32) * sp_ref[:, c0:c1]
                ya_ref[bi, :, c0:c1] = y.astype(BF16)

    @pl.when(j == 1)
    def _():
        for bi, r in enumerate(tile_rows()):
            q_ref[bi] = (r * q_scale).astype(BF16)

    @pl.when(j == 2)
    def _():
        store_heads(k32_ref, kb_ref)

    @pl.when(j == 3)
    def _():
        store_heads(v32_ref, vb_ref)

    @pl.when(j >= 4)
    def _():
        for bi, r in enumerate(tile_rows()):
            gates_ref[bi] = jax.nn.sigmoid(r).astype(BF16)


def _in_proj(x3, g, mod3, mod_row0, w_fl_t, b_f, w_cat, hist, w_pool, s_pool, start, q_scale, nb, tl):
    b, l, d = x3.shape
    nh = w_fl_t.shape[0]
    tn = nh * HEAD_DIM
    n_tiles = w_cat.shape[1] // tn
    n_gate_tiles = n_tiles - 4
    n_win, gw, _ = w_pool.shape
    assert mod_row0 % nb == 0 and b % nb == 0 and n_win * gw == tn and tl >= HALO
    tile = lambda width: pl.BlockSpec((nb, tl, width), lambda bb, li, j: (bb, li, 0))
    heads = pl.BlockSpec((nb, tl * nh, HEAD_DIM), lambda bb, li, j: (bb, li, 0))
    halo = pl.BlockSpec((nb, HALO, tn), lambda bb, li, j: (bb, 0, 0))
    const = lambda shape: pl.BlockSpec(shape, lambda bb, li, j: (0,) * len(shape))
    return pl.pallas_call(
        functools.partial(_in_proj_kernel, q_scale=q_scale, start=start),
        grid=(b // nb, l // tl, n_tiles),
        in_specs=[
            tile(d),
            const((1, d)),
            pl.BlockSpec((nb, N_MOD, d), lambda bb, li, j: (mod_row0 // nb + bb, 0, 0)),
            const((nh, d)),
            const((nh, 1)),
            pl.BlockSpec((d, tn), lambda bb, li, j: (0, j)),
            halo,
            const((n_win, gw, gw)),
            const((1, tn)),
        ],
        out_specs=[
            tile(tn), halo, tile(tn), heads, tile(tn), heads, tile(tn),
            pl.BlockSpec((nb, tl, tn), lambda bb, li, j: (bb, li, jnp.maximum(j - 4, 0))),
            pl.BlockSpec((nb, nh, tl), lambda bb, li, j: (bb, 0, li)),
        ],
        out_shape=[
            jax.ShapeDtypeStruct((b, l, tn), BF16),
            jax.ShapeDtypeStruct((b, HALO, tn), F32),
            jax.ShapeDtypeStruct((b, l, tn), BF16),
            jax.ShapeDtypeStruct((b, l * nh, HEAD_DIM), F32),
            jax.ShapeDtypeStruct((b, l, tn), BF16),
            jax.ShapeDtypeStruct((b, l * nh, HEAD_DIM), F32),
            jax.ShapeDtypeStruct((b, l, tn), BF16),
            jax.ShapeDtypeStruct((b, l, n_gate_tiles * tn), BF16),
            jax.ShapeDtypeStruct((b, nh, l), F32),
        ],
        scratch_shapes=[pltpu.VMEM((nb * tl, d), BF16), pltpu.VMEM((HALO + tl, tn), F32),
                        pltpu.VMEM((nb, HALO, tn), F32)],
        compiler_params=_params("parallel", "arbitrary", "arbitrary"),
        name="in_proj",
    )(x3, g.reshape(1, d), mod3, w_fl_t, b_f.reshape(nh, 1), w_cat, hist, w_pool, s_pool.reshape(1, tn))


def _mixer_out_kernel(ya_ref, yb_ref, ga_ref, gb_ref, x_ref, mod_ref, g_ref, wa_ref, wb_ref, wo_ref,
                      x1_ref, h2_ref, h2p_ref):
    merged = ga_ref[...].astype(F32) * jnp.dot(ya_ref[...], wa_ref[...], preferred_element_type=F32)
    merged = merged + gb_ref[...].astype(F32) * jnp.dot(yb_ref[...], wb_ref[...], preferred_element_type=F32)
    m = mod_ref[...]
    x1 = x_ref[...] + m[2:3, :] * jnp.dot(merged.astype(BF16), wo_ref[...], preferred_element_type=F32)
    x1_ref[...] = x1
    h2 = _modulated_norm(x1, g_ref[...], m, 3, 4)
    h2_ref[...] = h2.astype(BF16)
    _store_packed(h2p_ref, h2)


def _mixer_out(y_a, y_b, gates, x3, mod3, mod_row0, g_ffn, w_up_a, w_up_b, w_out, tl):
    b, l, d = x3.shape
    n_l = l // tl
    dp, da = y_a.shape[2], y_b.shape[2]
    tile = lambda width, col=0: pl.BlockSpec((None, tl, width), lambda bi, li: (bi, li, col))
    resident = lambda shape: pl.BlockSpec(shape, lambda bi, li: (0,) * len(shape), pipeline_mode=pl.Buffered(1))
    return pl.pallas_call(
        _mixer_out_kernel,
        grid=(b, n_l),
        in_specs=[
            tile(dp), tile(da), tile(d, 0), tile(d, 1), tile(d),
            pl.BlockSpec((None, N_MOD, d), lambda bi, li: (mod_row0 + bi, 0, 0)),
            pl.BlockSpec((1, d), lambda bi, li: (0, 0)),
            resident((dp, d)), resident((da, d)), resident((d, d)),
        ],
        out_specs=[
            tile(d), tile(d),
            pl.BlockSpec((tl * PACK_CHUNKS, LANES), lambda bi, li: (bi * n_l + li, 0)),
        ],
        out_shape=[
            jax.ShapeDtypeStruct((b, l, d), F32),
            jax.ShapeDtypeStruct((b, l, d), BF16),
            jax.ShapeDtypeStruct((b * l * PACK_CHUNKS, LANES), U32),
        ],
        compiler_params=_params("parallel", "parallel"),
        name="mixer_out",
    )(y_a, y_b, gates, gates, x3, mod3, g_ffn.reshape(1, d), w_up_a, w_up_b, w_out)


def _cumsum_kernel(x_ref, o_ref):
    x = x_ref[...]
    n = x.shape[1]
    idx = lax.broadcasted_iota(I32, x.shape, 1)
    s = 1
    while s < n:
        x = x + jnp.where(idx >= s, pltpu.roll(x, s, 1), 0.0)
        s *= 2
    o_ref[...] = x


def _cumsum_last(x3):
    b, h, n = x3.shape
    return pl.pallas_call(
        _cumsum_kernel,
        grid=(b,),
        in_specs=[pl.BlockSpec((None, h, n), lambda bi: (bi, 0, 0))],
        out_specs=pl.BlockSpec((None, h, n), lambda bi: (bi, 0, 0)),
        out_shape=jax.ShapeDtypeStruct((b, h, n), F32),
        compiler_params=_params("parallel"),
        name="cumsum_logf",
    )(x3)


def _attn_prompt_kernel(q_ref, k_ref, v_ref, fk_ref, o_ref, s0_ref, s1_ref, m_ref, l_ref, acc_ref, *, tq):
    qi = pl.program_id(2)
    tk = tq // 2
    n_chunks = tk // LANES

    def produce(j, s_ref, r0=0):
        off = pl.multiple_of(j * tk, tk)
        s = lax.dot_general(q_ref[r0:, :], k_ref[pl.ds(off, tk), :], NT_DIMS, preferred_element_type=F32)
        s_ref[r0:, :] = s - fk_ref[j] * LOG2_E

    def consume(j, s_ref, first_col=None, r0=0):
        s = s_ref[r0:, :]
        if first_col is not None:
            row = r0 + lax.broadcasted_iota(I32, s.shape, 0)
            col = first_col + lax.broadcasted_iota(I32, s.shape, 1)
            s = jnp.where(col <= row, s, NEG_INF)
        chunks = [s[:, c * LANES:(c + 1) * LANES] for c in range(n_chunks)]
        m_old = m_ref[r0:, :]
        m_new = jnp.maximum(m_old, jnp.max(functools.reduce(jnp.maximum, chunks), axis=1, keepdims=True))
        alpha = jnp.exp2(m_old - m_new)
        p_chunks = [jnp.exp2(ch - m_new) for ch in chunks]
        l_ref[r0:, :] = alpha * l_ref[r0:, :] + functools.reduce(lambda a, c: a + c, p_chunks)
        p = jnp.concatenate(p_chunks, axis=1).astype(BF16)
        v = v_ref[pl.ds(pl.multiple_of(j * tk, tk), tk), :]
        acc_ref[r0:, :] = alpha * acc_ref[r0:, :] + jnp.dot(p, v, preferred_element_type=F32)
        m_ref[r0:, :] = m_new

    m_ref[...] = jnp.full(m_ref.shape, NEG_INF, F32)
    l_ref[...] = jnp.zeros(l_ref.shape, F32)
    acc_ref[...] = jnp.zeros(acc_ref.shape, F32)
    produce(0, s0_ref)

    def pair(jj, carry):
        j0 = 2 * jj
        produce(j0 + 1, s1_ref)
        consume(j0, s0_ref)
        produce(j0 + 2, s0_ref)
        consume(j0 + 1, s1_ref)
        return carry

    lax.fori_loop(0, qi, pair, 0)
    produce(2 * qi + 1, s1_ref, r0=tk)
    consume(2 * qi, s0_ref, first_col=0)
    consume(2 * qi + 1, s1_ref, first_col=tk, r0=tk)

    l = jnp.sum(l_ref[...], axis=1, keepdims=True)
    o_ref[...] = (acc_ref[...] / l).astype(o_ref.dtype)


def _attn_prompt(q3, k3, v3, f3, tq):
    b, l, da = q3.shape
    nh = da // HEAD_DIM
    tk = tq // 2
    nk = l // tk
    fk = f3.reshape(b * nh, nk, 1, tk)
    return pl.pallas_call(
        functools.partial(_attn_prompt_kernel, tq=tq),
        grid=(b, nh, l // tq),
        in_specs=[
            pl.BlockSpec((None, tq, HEAD_DIM), lambda bi, hi, qi: (bi, qi, hi)),
            pl.BlockSpec((None, l, HEAD_DIM), lambda bi, hi, qi: (bi, 0, hi)),
            pl.BlockSpec((None, l, HEAD_DIM), lambda bi, hi, qi: (bi, 0, hi)),
            pl.BlockSpec((None, nk, 1, tk), lambda bi, hi, qi: (bi * nh + hi, 0, 0, 0)),
        ],
        out_specs=pl.BlockSpec((None, tq, HEAD_DIM), lambda bi, hi, qi: (bi, qi, hi)),
        out_shape=jax.ShapeDtypeStruct((b, l, da), BF16),
        scratch_shapes=[
            pltpu.VMEM((tq, tk), F32),
            pltpu.VMEM((tq, tk), F32),
            pltpu.VMEM((tq, LANES), F32),
            pltpu.VMEM((tq, LANES), F32),
            pltpu.VMEM((tq, HEAD_DIM), F32),
        ],
        compiler_params=_params("parallel", "parallel", "parallel"),
        name="fox_prompt",
    )(q3, k3, v3, fk)


def _attn_sample_kernel(q_ref, kp_ref, vp_ref, kn_ref, vn_ref, fp_ref, fn_ref, o_ref, m_ref, l_ref, acc_ref, *,
                        nh, pc, lq):
    c = pl.program_id(1)

    @pl.when(c == 0)
    def _():
        m_ref[...] = jnp.full(m_ref.shape, NEG_INF, F32)
        l_ref[...] = jnp.zeros(l_ref.shape, F32)
        acc_ref[...] = jnp.zeros(acc_ref.shape, F32)

    scores = []
    for h in range(nh):
        k = kp_ref[pl.ds(h, pc, stride=nh), :].astype(BF16)
        s = lax.dot_general(q_ref[:, h * HEAD_DIM:(h + 1) * HEAD_DIM], k, NT_DIMS, preferred_element_type=F32)
        scores.append(s - fp_ref[h:h + 1, :])
    probs, alphas = [], []
    for h, s in enumerate(scores):
        chunks = [s[:, j * LANES:(j + 1) * LANES] for j in range(pc // LANES)]
        m_old = m_ref[h]
        m_new = jnp.maximum(m_old, jnp.max(functools.reduce(jnp.maximum, chunks), axis=1, keepdims=True))
        alpha = jnp.exp(m_old - m_new)
        p_chunks = [jnp.exp(ch - m_new) for ch in chunks]
        l_ref[h] = alpha * l_ref[h] + functools.reduce(lambda a, b: a + b, p_chunks)
        m_ref[h] = m_new
        probs.append(jnp.concatenate(p_chunks, axis=1).astype(BF16))
        alphas.append(alpha)
    for h in range(nh):
        v = vp_ref[pl.ds(h, pc, stride=nh), :].astype(BF16)
        acc_ref[h] = alphas[h] * acc_ref[h] + jnp.dot(probs[h], v, preferred_element_type=F32)

    @pl.when(c == pl.num_programs(1) - 1)
    def _():
        row = lax.broadcasted_iota(I32, (lq, lq), 0)
        col = lax.broadcasted_iota(I32, (lq, lq), 1)
        for h in range(nh):
            cols = slice(h * HEAD_DIM, (h + 1) * HEAD_DIM)
            s = lax.dot_general(q_ref[:, cols], kn_ref[:, cols], NT_DIMS, preferred_element_type=F32)
            s = jnp.where(col <= row, s - fn_ref[h:h + 1, :lq], NEG_INF)
            m_old = m_ref[h]
            m_new = jnp.maximum(m_old, jnp.max(s, axis=1, keepdims=True))
            alpha = jnp.exp(m_old - m_new)
            p = jnp.exp(s - m_new[:, :lq])
            l = jnp.sum(alpha * l_ref[h], axis=1, keepdims=True) + jnp.sum(p, axis=1, keepdims=True)
            acc = alpha * acc_ref[h] + jnp.dot(p.astype(BF16), vn_ref[:, cols], preferred_element_type=F32)
            o_ref[:, cols] = (acc / l).astype(o_ref.dtype)


def _attn_sample(q3, kn3, vn3, cache_k, cache_v, f_past, f_new, pc):
    b, lq, da = q3.shape
    _, past, nh, dh = cache_k.shape
    nc = past // pc
    new_spec = pl.BlockSpec((None, lq, da), lambda bi, ci: (bi, 0, 0))
    past_spec = pl.BlockSpec((None, pc * nh, dh), lambda bi, ci: (bi, ci, 0))
    return pl.pallas_call(
        functools.partial(_attn_sample_kernel, nh=nh, pc=pc, lq=lq),
        grid=(b, nc),
        in_specs=[new_spec, past_spec, past_spec, new_spec, new_spec,
                  pl.BlockSpec((None, None, nh, pc), lambda bi, ci: (bi, ci, 0, 0)),
                  pl.BlockSpec((None, nh, LANES), lambda bi, ci: (bi, 0, 0))],
        out_specs=new_spec,
        out_shape=jax.ShapeDtypeStruct((b, lq, da), BF16),
        scratch_shapes=[pltpu.VMEM((nh, lq, LANES), F32), pltpu.VMEM((nh, lq, LANES), F32),
                        pltpu.VMEM((nh, lq, HEAD_DIM), F32)],
        compiler_params=_params("parallel", "arbitrary"),
        name="fox_sample",
    )(q3, cache_k.reshape(b, past * nh, dh), cache_v.reshape(b, past * nh, dh), kn3, vn3, f_past, f_new)


def _router_kernel(h_ref, wr_ref, br_ref, tri_ref, cnt0_ref, idx_ref, gate_ref, rank_ref, cnt_ref, carry_ref):
    @pl.when(pl.program_id(0) == 0)
    def _():
        carry_ref[...] = cnt0_ref[...]

    n_exp, tm = wr_ref.shape[0], h_ref.shape[0]
    scores = jax.nn.sigmoid(lax.dot_general(wr_ref[...], h_ref[...], NT_DIMS, preferred_element_type=F32))
    work = scores + br_ref[...]
    eidx = lax.broadcasted_iota(I32, (n_exp, tm), 0)
    picked = jnp.zeros((n_exp, tm), F32)
    onehots, gates = [], []
    for k in range(TOP_K):
        best = jnp.max(work, axis=0, keepdims=True)
        first = jnp.min(jnp.where(work == best, eidx, n_exp), axis=0, keepdims=True)
        onehot = eidx == first
        idx_ref[k:k + 1, :] = first
        gates.append(jnp.sum(jnp.where(onehot, scores, 0.0), axis=0, keepdims=True))
        onehots.append(onehot)
        picked = picked + jnp.where(onehot, 1.0, 0.0)
        work = jnp.where(onehot, NEG_INF, work)
    norm = ROUTED_SCALE / functools.reduce(lambda a, c: a + c, gates)
    cum = jnp.dot(picked.astype(BF16), tri_ref[...], preferred_element_type=F32)
    before = cum - picked + carry_ref[:, 0:1]
    for k in range(TOP_K):
        gate_ref[k:k + 1, :] = gates[k] * norm
        rank_ref[k:k + 1, :] = jnp.sum(jnp.where(onehots[k], before, 0.0), axis=0, keepdims=True).astype(I32)
    carry_ref[...] = carry_ref[...] + cum[:, tm - 1:tm]
    cnt_ref[...] = carry_ref[...]


def _router(h2, tok0, ntok, w_router_t, b_router, cnt0, tm):
    d = h2.shape[1]
    n_exp = w_router_t.shape[0]
    tri = jnp.triu(jnp.ones((tm, tm), BF16))
    tok_spec = pl.BlockSpec((TOP_K, tm), lambda i: (0, i))
    cnt_spec = pl.BlockSpec((n_exp, LANES), lambda i: (0, 0))
    return pl.pallas_call(
        _router_kernel,
        grid=(ntok // tm,),
        in_specs=[
            pl.BlockSpec((tm, d), lambda i: (tok0 // tm + i, 0)),
            pl.BlockSpec((n_exp, d), lambda i: (0, 0)),
            pl.BlockSpec((n_exp, 1), lambda i: (0, 0)),
            pl.BlockSpec((tm, tm), lambda i: (0, 0)),
            cnt_spec,
        ],
        out_specs=[tok_spec, tok_spec, tok_spec, cnt_spec],
        out_shape=[
            jax.ShapeDtypeStruct((TOP_K, ntok), I32),
            jax.ShapeDtypeStruct((TOP_K, ntok), F32),
            jax.ShapeDtypeStruct((TOP_K, ntok), I32),
            jax.ShapeDtypeStruct((n_exp, LANES), F32),
        ],
        scratch_shapes=[pltpu.VMEM((n_exp, LANES), F32)],
        compiler_params=_params("arbitrary"),
        name="moe_router",
    )(h2, w_router_t, b_router.reshape(n_exp, 1), tri, cnt0)


LAYOUT_ROWS = 8


def _layout_kernel(cnt_ref, o_ref):
    n_exp = cnt_ref.shape[0]
    lane = lax.broadcasted_iota(I32, o_ref.shape, 1)
    row0 = lane * MOE_ROWS

    def forward(e, carry):
        start, blk_e, valid, pad_start = carry
        cnt = cnt_ref[e]
        end = start + (cnt + MOE_ROWS - 1) // MOE_ROWS * MOE_ROWS
        in_span = lambda new, old: jnp.where(row0 >= start, jnp.where(row0 < end, new, old), old)
        blk_e = in_span(e, blk_e)
        valid = in_span(jnp.clip(start + cnt - row0, 0, MOE_ROWS), valid)
        return end, blk_e, valid, jnp.where(lane == e, start, pad_start)

    zeros = jnp.zeros(o_ref.shape, I32)
    end, blk_e, valid, pad_start = lax.fori_loop(0, n_exp, forward, (jnp.int32(0), zeros + (n_exp - 1), zeros, zeros))

    def backward(t, carry):
        nxt, blk_next = carry
        e = n_exp - 1 - t
        return jnp.where(cnt_ref[e] > 0, e, nxt), jnp.where(blk_e == e, nxt, blk_next)

    _, blk_next = lax.fori_loop(0, n_exp, backward, (jnp.int32(-1), zeros - 1))
    table = zeros + end // MOE_ROWS
    sub = lax.broadcasted_iota(I32, o_ref.shape, 0)
    for r, row in enumerate((blk_e, valid, blk_next, pad_start)):
        table = jnp.where(sub == r, row, table)
    o_ref[...] = table


def _layout(counts, n_blocks):
    n_exp = counts.shape[0]
    w = -(-max(n_blocks, n_exp) // LANES) * LANES
    table = pl.pallas_call(
        _layout_kernel,
        grid_spec=pltpu.PrefetchScalarGridSpec(
            num_scalar_prefetch=1, grid=(1,), in_specs=[],
            out_specs=pl.BlockSpec((LAYOUT_ROWS, w), lambda i, cnt: (0, 0))),
        out_shape=jax.ShapeDtypeStruct((LAYOUT_ROWS, w), I32),
        compiler_params=_params("arbitrary"),
        name="moe_layout",
    )(counts)
    return table[0, :n_blocks], table[1, :n_blocks], table[2, :n_blocks], table[3, :n_exp], table[4, :1]


def _slot_rows_kernel(ps_ref, idx_ref, rank_ref, o_ref):
    idx = idx_ref[...]
    rows = rank_ref[...]
    for e in range(ps_ref.shape[0]):
        rows = rows + jnp.where(idx == e, ps_ref[e], 0)
    o_ref[...] = rows


def _slot_rows(pad_start, idx_t, rank_t):
    k, ntok = idx_t.shape
    tm = min(2048, ntok)
    spec = pl.BlockSpec((k, tm), lambda i, ps: (0, i))
    return pl.pallas_call(
        _slot_rows_kernel,
        grid_spec=pltpu.PrefetchScalarGridSpec(num_scalar_prefetch=1, grid=(ntok // tm,), in_specs=[spec, spec],
                                               out_specs=spec),
        out_shape=jax.ShapeDtypeStruct((k, ntok), I32),
        compiler_params=_params("parallel"),
        name="moe_slot_rows",
    )(pad_start, idx_t, rank_t)


def _moe_kernel(be_ref, bv_ref, nx_ref, nu_ref, x_ref, wg_hbm, wu_hbm, wd_hbm, y_ref,
                wg_f32, wu_f32, wd_f32, wg_bf, wu_bf, wd_bf, sems):
    i = pl.program_id(0)
    active = i < nu_ref[0]
    expert = be_ref[i]

    def fetch(e):
        pairs = ((wg_hbm, wg_f32), (wu_hbm, wu_f32), (wd_hbm, wd_f32))
        return [pltpu.make_async_copy(w.at[e], buf, sems.at[n]) for n, (w, buf) in enumerate(pairs)]

    @pl.when(active & (i == 0))
    def _():
        for c in fetch(expert):
            c.start()

    @pl.when(active & ((i == 0) | (expert != be_ref[jnp.maximum(i - 1, 0)])))
    def _():
        for c in fetch(expert):
            c.wait()
        wg_bf[...] = wg_f32[...].astype(BF16)
        wu_bf[...] = wu_f32[...].astype(BF16)
        wd_bf[...] = wd_f32[...].astype(BF16)

        @pl.when(nx_ref[i] >= 0)
        def _():
            for c in fetch(nx_ref[i]):
                c.start()

    def chain(row0, rows):
        x = _load_packed(x_ref, rows, valid=bv_ref[i], row0=row0).astype(BF16)
        g = jnp.dot(x, wg_bf[...], preferred_element_type=F32)
        u = jnp.dot(x, wu_bf[...], preferred_element_type=F32)
        a = (_silu(g) * u).astype(BF16)
        _store_packed(y_ref, jnp.dot(a, wd_bf[...], preferred_element_type=F32), row0=row0)

    starts = range(0, MOE_ROWS, MOE_CHAIN_ROWS)
    full = bv_ref[i] > starts[-1]
    pl.when(active & full)(functools.partial(chain, 0, MOE_ROWS))
    for row0 in starts[:-1]:
        pl.when(active & jnp.logical_not(full) & (bv_ref[i] > row0))(functools.partial(chain, row0, MOE_CHAIN_ROWS))


def _moe_experts(x_sorted, blk_e, blk_valid, blk_next, n_used, w_e_gate, w_e_up, w_e_down):
    cap = x_sorted.shape[0] // PACK_CHUNKS
    d, de = w_e_gate.shape[1:]
    n_blocks = cap // MOE_ROWS
    row_map = lambda i, be, bv, nx, nu: (jnp.minimum(i, nu[0] - 1), 0)
    hbm = pl.BlockSpec(memory_space=pl.ANY)
    return pl.pallas_call(
        _moe_kernel,
        grid_spec=pltpu.PrefetchScalarGridSpec(
            num_scalar_prefetch=4,
            grid=(n_blocks,),
            in_specs=[pl.BlockSpec((MOE_ROWS * PACK_CHUNKS, LANES), row_map), hbm, hbm, hbm],
            out_specs=pl.BlockSpec((MOE_ROWS * PACK_CHUNKS, LANES), row_map),
            scratch_shapes=[
                pltpu.VMEM((d, de), F32), pltpu.VMEM((d, de), F32), pltpu.VMEM((de, d), F32),
                pltpu.VMEM((d, de), BF16), pltpu.VMEM((d, de), BF16), pltpu.VMEM((de, d), BF16),
                pltpu.SemaphoreType.DMA((3,)),
            ],
        ),
        out_shape=jax.ShapeDtypeStruct(x_sorted.shape, U32),
        compiler_params=_params("arbitrary"),
        name="moe_experts",
    )(blk_e, blk_valid, blk_next, n_used, x_sorted, w_e_gate, w_e_up, w_e_down)


def _sc_mesh():
    return plsc.VectorSubcoreMesh(core_axis_name="c", subcore_axis_name="s")


def _sc_worker_id():
    return lax.axis_index("s") * SC_CORES + lax.axis_index("c")


def _sc_scatter_rows(segments, n_out):
    row_shape = segments[0][0].shape[1:]
    dtype = segments[0][0].dtype
    n_seg = len(segments)
    geom = [(tok0, dw.shape[0] // SC_WORKERS, dw.shape[2]) for _, tok0, dw in segments]
    scratch = []
    for _, n_win, win in geom:
        scratch += [pltpu.VMEM((n_win, TOP_K, win), I32), pltpu.VMEM((win,) + row_shape, dtype)]

    @functools.partial(
        pl.kernel, mesh=_sc_mesh(),
        out_type=jax.ShapeDtypeStruct((n_out,) + row_shape, dtype),
        scratch_types=scratch + [pltpu.SemaphoreType.DMA],
    )
    def scatter_kernel(*refs):
        out_hbm = refs[2 * n_seg]
        sem = refs[-1]
        for s, (tok0, n_win, win) in enumerate(geom):
            rows_hbm, idx_hbm = refs[2 * s], refs[2 * s + 1]
            idx_v, rows_v = refs[2 * n_seg + 1 + 2 * s], refs[2 * n_seg + 2 + 2 * s]
            first = _sc_worker_id() * n_win
            pltpu.sync_copy(idx_hbm.at[pl.ds(first, n_win)], idx_v)

            @pl.loop(0, n_win)
            def _(w):
                pltpu.sync_copy(rows_hbm.at[pl.ds(tok0 + (first + w) * win, win)], rows_v)
                copies = [pltpu.make_async_copy(rows_v, out_hbm.at[idx_v.at[w, k]], sem) for k in range(TOP_K)]
                for c in copies:
                    c.start()
                for c in copies:
                    c.wait()

    args = []
    for rows, _, dest_win in segments:
        args += [rows, dest_win]
    return scatter_kernel(*args)


def _sc_gather_rows(table, idx_wins):
    row_shape = table.shape[1:]
    n_seg = len(idx_wins)
    geom = [(iw.shape[0] // SC_WORKERS, iw.shape[1]) for iw in idx_wins]
    scratch = []
    for n_win, win in geom:
        scratch += [pltpu.VMEM((n_win, win), I32), pltpu.VMEM((win,) + row_shape, table.dtype)]

    @functools.partial(
        pl.kernel, mesh=_sc_mesh(),
        out_type=[jax.ShapeDtypeStruct((iw.shape[0] * iw.shape[1],) + row_shape, table.dtype) for iw in idx_wins],
        scratch_types=scratch + [pltpu.SemaphoreType.DMA],
    )
    def gather_kernel(*refs):
        table_hbm = refs[0]
        sem = refs[-1]
        for s, (n_win, win) in enumerate(geom):
            idx_hbm, out_hbm = refs[1 + s], refs[1 + n_seg + s]
            idx_v, rows_v = refs[1 + 2 * n_seg + 2 * s], refs[2 + 2 * n_seg + 2 * s]
            first = _sc_worker_id() * n_win
            pltpu.sync_copy(idx_hbm.at[pl.ds(first, n_win)], idx_v)

            @pl.loop(0, n_win)
            def _(w):
                pltpu.async_copy(table_hbm.at[idx_v.at[w]], rows_v, sem).wait()
                pltpu.sync_copy(rows_v, out_hbm.at[pl.ds((first + w) * win, win)])

    return gather_kernel(table, *idx_wins)


def _final_kernel(x1_ref, h2_ref, yt_ref, gate_ref, mod_ref, wsg_ref, wsu_ref, wsd_ref, gf_ref, *rest):
    o_ref = rest[-1]
    h2 = h2_ref[...]
    g = jnp.dot(h2, wsg_ref[...], preferred_element_type=F32)
    u = jnp.dot(h2, wsu_ref[...], preferred_element_type=F32)
    f = jnp.dot((_silu(g) * u).astype(BF16), wsd_ref[...], preferred_element_type=F32)
    gate = gate_ref[...]
    tl = h2.shape[0]
    for k in range(TOP_K):
        f = f + gate[:, k:k + 1] * _load_packed(yt_ref.at[k], tl)
    x2 = x1_ref[...] + mod_ref[N_MOD - 1:N_MOD, :] * f
    ms = jnp.mean(x2 * x2, axis=-1, keepdims=True)
    o_ref[...] = x2 * lax.rsqrt(ms + EPS) * gf_ref[...]


def _final(x1, h2, b0, y_tok, gate, mod3, mod_row0, w_s_gate, w_s_up, w_s_down, g_final, tl, y_prev=None):
    b, l, d = x1.shape
    nb = gate.shape[0]
    ds = w_s_gate.shape[1]
    tile = pl.BlockSpec((None, tl, d), lambda bi, li: (b0 + bi, li, 0))
    const = lambda shape: pl.BlockSpec(shape, lambda bi, li: (0,) * len(shape), pipeline_mode=pl.Buffered(1))
    in_specs = [
        tile,
        tile,
        pl.BlockSpec((TOP_K, None, tl * PACK_CHUNKS, LANES), lambda bi, li: (0, bi, li, 0)),
        pl.BlockSpec((None, tl, TOP_K), lambda bi, li: (bi, li, 0)),
        pl.BlockSpec((None, N_MOD, d), lambda bi, li: (mod_row0 + b0 + bi, 0, 0)),
        const((d, ds)),
        const((d, ds)),
        const((ds, d)),
        const((1, d)),
    ]
    args = [x1, h2, y_tok, gate, mod3, w_s_gate, w_s_up, w_s_down, g_final.reshape(1, d)]
    aliases = {}
    if y_prev is not None:
        in_specs.append(pl.BlockSpec(memory_space=pl.ANY))
        aliases = {len(args): 0}
        args.append(y_prev)
    return pl.pallas_call(
        _final_kernel,
        grid=(nb, l // tl),
        in_specs=in_specs,
        out_specs=tile,
        out_shape=jax.ShapeDtypeStruct((b, l, d), F32),
        input_output_aliases=aliases,
        compiler_params=_params("parallel", "parallel"),
        name="combine_final",
    )(*args)


def _group_mixer(x3, mod3, mod_row0, hist, start, past, wts, tiles):
    b, l, d = x3.shape
    tl = tiles["tl"]
    dp = wts["w_pool"].shape[0] * wts["w_pool"].shape[1]
    da = wts["w_up_b"].shape[0]
    nh = da // HEAD_DIM
    t = b * l

    assert dp == da, "u, q, k, v must each be one column tile of the input projection"
    scale = HEAD_DIM ** -0.5 * (LOG2_E if past is None else 1.0)
    y_a, u_tail, qb, k, kb, v, vb, gates, logf_t = _in_proj(
        x3, wts["g_mix"], mod3, mod_row0, wts["w_fl_t"], wts["b_f"], wts["w_cat"], hist, wts["w_pool"],
        wts["s_pool"], start, scale, tiles["nb_in"], tl)

    if past is None:
        y_b = _attn_prompt(qb, kb, vb, _cumsum_last(logf_t), tiles["tq"])
    else:
        cache_k, cache_v, cache_logf = past
        p = cache_k.shape[1]
        pc = tiles["past_chunk"]
        assert l <= LANES and p % pc == 0 and p % LANES == 0
        lf_all = jnp.concatenate([jnp.swapaxes(cache_logf.astype(F32), 1, 2), logf_t,
                                  jnp.zeros((b, nh, LANES - l), F32)], axis=2)
        f_all = _cumsum_last(lf_all)
        f_past = f_all[:, :, :p].reshape(b, nh, p // pc, pc).transpose(0, 2, 1, 3)
        y_b = _attn_sample(qb, kb, vb, cache_k, cache_v, f_past, f_all[:, :, p:], pc)

    x1, h2, h2_packed = _mixer_out(y_a, y_b, gates, x3, mod3, mod_row0, wts["g_ffn"], wts["w_up_a"], wts["w_up_b"],
                                   wts["w_out"], tiles["tl_out"])
    caches = (u_tail[:, HALO - POOL_HIST:, :], k.reshape(b, l, nh, HEAD_DIM), v.reshape(b, l, nh, HEAD_DIM),
              jnp.swapaxes(logf_t, 1, 2))
    return dict(x1=x1, h2=h2, h2_packed=h2_packed, mod_row0=mod_row0, tiles=tiles), caches


def _moe_chunk(segments, wts):
    n_exp = wts["w_router_t"].shape[0]
    row_shape = (PACK_CHUNKS, LANES)
    counts_f = jnp.zeros((n_exp, LANES), F32)
    routed = []
    for grp, b0, nb in segments:
        b, l, d = grp["h2"].shape
        ntok = nb * l
        idx_t, gate_t, rank_t, counts_f = _router(grp["h2"].reshape(b * l, d), b0 * l, ntok, wts["w_router_t"],
                                                  wts["b_router"], counts_f, min(grp["tiles"]["t_route"], ntok))
        routed.append((idx_t, gate_t, rank_t, ntok))
    counts = counts_f[:, 0].astype(I32)

    n_slots = sum(r[3] for r in routed) * TOP_K
    n_blocks = -(-n_slots // MOE_ROWS) + n_exp
    cap = n_blocks * MOE_ROWS
    blk_e, blk_valid, blk_next, pad_start, n_used = _layout(counts, n_blocks)

    scatter_segs, gather_idx = [], []
    for (grp, b0, nb), (idx_t, _, rank_t, ntok) in zip(segments, routed):
        b, l, _ = grp["h2"].shape
        dest = _slot_rows(pad_start, idx_t, rank_t)
        win_d = min(SC_WINDOW, ntok // SC_WORKERS)
        dest_win = dest.reshape(TOP_K, ntok // win_d, win_d).transpose(1, 0, 2)
        scatter_segs.append((grp["h2_packed"].reshape((b * l,) + row_shape), b0 * l, dest_win))
        win_c = min(SC_WINDOW, ntok * TOP_K // SC_WORKERS)
        gather_idx.append(dest.reshape(ntok * TOP_K // win_c, win_c))

    x_sorted = _sc_scatter_rows(scatter_segs, cap)
    y_sorted = _moe_experts(x_sorted.reshape(cap * PACK_CHUNKS, LANES), blk_e, blk_valid, blk_next, n_used,
                            wts["w_e_gate"], wts["w_e_up"], wts["w_e_down"])
    y_toks = _sc_gather_rows(y_sorted.reshape((cap,) + row_shape), gather_idx)
    out = []
    for (grp, b0, nb), (_, gate_t, _, _), y_tok in zip(segments, routed, y_toks):
        l = grp["h2"].shape[1]
        out.append((y_tok.reshape(TOP_K, nb, l * PACK_CHUNKS, LANES), gate_t.T.reshape(nb, l, TOP_K)))
    return out


def kernel(x_prompt, x_sample, cache_pool, cache_k, cache_v, cache_logf, c_prompt, c_sample, w_ada, b_ada, g_mix, w_in, b_f, w_pool, s_pool, w_up_a, w_up_b, w_out, g_ffn, w_router, b_router, w_e_gate, w_e_up, w_e_down, w_s_gate, w_s_up, w_s_down, g_final):
    depth = w_ada.shape[0]
    assert depth == 1, "a single layer is supported"
    bp, lp, d = x_prompt.shape
    bs, ls, _ = x_sample.shape
    dp = w_pool.shape[1] * w_pool.shape[2]
    da = w_up_b.shape[1]
    nh = da // HEAD_DIM
    assert bp <= MOD_ROWS_SAMPLE

    one = lambda a: a.reshape(a.shape[1:])
    c_all = jnp.zeros((MOD_ROWS_SAMPLE + bs, d), F32).at[:bp].set(c_prompt).at[MOD_ROWS_SAMPLE:].set(c_sample)
    mod3 = _ada_mod(c_all, one(w_ada), one(b_ada)).reshape(c_all.shape[0], N_MOD, d)

    n_main = dp + 3 * da
    w_in1 = one(w_in)
    wts = dict(
        g_mix=one(g_mix), g_ffn=one(g_ffn), b_f=one(b_f), s_pool=one(s_pool), b_router=one(b_router),
        w_cat=jnp.concatenate([w_in1[:, :n_main], w_in1[:, n_main + nh:]], axis=1).astype(BF16),
        w_fl_t=w_in1[:, n_main:n_main + nh].T.astype(BF16),
        w_pool=one(w_pool).astype(BF16),
        w_up_a=one(w_up_a).astype(BF16), w_up_b=one(w_up_b).astype(BF16), w_out=one(w_out).astype(BF16),
        w_router_t=one(w_router).T.astype(BF16),
        w_e_gate=one(w_e_gate), w_e_up=one(w_e_up), w_e_down=one(w_e_down),
        w_s_gate=one(w_s_gate).astype(BF16), w_s_up=one(w_s_up).astype(BF16), w_s_down=one(w_s_down).astype(BF16),
    )

    tiles_p = dict(tl=min(512, lp), tq=min(1024, lp), t_route=min(512, bp * lp), tl_out=min(256, lp),
                   tl_final=min(256, lp), nb_in=1)
    tiles_s = dict(tl=ls, t_route=bs * ls, tl_out=ls, tl_final=ls, nb_in=bs, past_chunk=min(512, cache_k.shape[2]))

    hist_p = jnp.zeros((bp, HALO, dp), F32)
    grp_p, caches_p = _group_mixer(x_prompt, mod3, MOD_ROWS_PROMPT, hist_p, 0, None, wts, tiles_p)
    hist_s = jnp.pad(one(cache_pool), ((0, 0), (HALO - POOL_HIST, 0), (0, 0)))
    past = (one(cache_k), one(cache_v), one(cache_logf))
    grp_s, caches_s = _group_mixer(x_sample, mod3, MOD_ROWS_SAMPLE, hist_s, cache_k.shape[2], past, wts, tiles_s)

    n_chunks = MOE_CHUNKS if bp % MOE_CHUNKS == 0 else 1
    nb_c = bp // n_chunks
    chunks = [[(grp_p, c * nb_c, nb_c)] for c in range(n_chunks)]
    chunks[-1].append((grp_s, 0, bs))
    outs = {id(grp_p): None, id(grp_s): None}
    for segments in chunks:
        for (grp, b0, nb), (y_tok, gate) in zip(segments, _moe_chunk(segments, wts)):
            outs[id(grp)] = _final(grp["x1"], grp["h2"], b0, y_tok, gate, mod3, grp["mod_row0"], wts["w_s_gate"],
                                   wts["w_s_up"], wts["w_s_down"], g_final, grp["tiles"]["tl_final"],
                                   y_prev=outs[id(grp)])
    stack = lambda a: a[None]
    return (outs[id(grp_p)], outs[id(grp_s)], *map(stack, caches_p), *map(stack, caches_s))
```

```python
import functools

import jax
import jax.numpy as jnp
from jax import lax
from jax.experimental import pallas as pl
from jax.experimental.pallas import tpu as pltpu
from jax.experimental.pallas import tpu_sc as plsc

F32 = jnp.float32
BF16 = jnp.bfloat16
I32 = jnp.int32
U32 = jnp.uint32

EPS = 1e-6
N_MOD = 6
POOL_WINDOWS = (2, 4, 8, 16)
POOL_HIST = max(POOL_WINDOWS) - 1
HALO = 16
HEAD_DIM = 128
TOP_K = 8
ROUTED_SCALE = 2.5
MOE_ROWS = 1024
MOE_CHAIN_ROWS = 256
MOE_CHUNKS = 2
MOD_ROWS_PROMPT = 0
MOD_ROWS_SAMPLE = 8
VMEM_LIMIT_BYTES = 52 * 1024 * 1024
NEG_INF = float("-inf")
LOG2_E = 1.4426950408889634
LANES = 128
PACK_CHUNKS = 8
SC_CORES = 2
SC_SUBCORES = 16
SC_WORKERS = SC_CORES * SC_SUBCORES
SC_WINDOW = 32
NT_DIMS = (((1,), (1,)), ((), ()))


def _params(*semantics):
    return pltpu.CompilerParams(dimension_semantics=semantics, vmem_limit_bytes=VMEM_LIMIT_BYTES)


def _sigmoid(x):
    return 0.5 * jnp.tanh(0.5 * x) + 0.5


def _silu(x):
    return x * _sigmoid(x)


def _store_packed(ref, x, row0=0):
    rows, width = x.shape
    half = width // 2
    bits = pltpu.bitcast(x.astype(BF16).astype(F32), U32)
    packed = (bits[:, :half] >> 16) | (bits[:, half:] & jnp.uint32(0xFFFF0000))
    for j in range(PACK_CHUNKS):
        ref[pl.ds(row0 * PACK_CHUNKS + j, rows, stride=PACK_CHUNKS), :] = packed[:, j * LANES:(j + 1) * LANES]


def _load_packed(ref, rows, valid=None, row0=0):
    lo, hi = [], []
    for j in range(PACK_CHUNKS):
        p = ref[pl.ds(row0 * PACK_CHUNKS + j, rows, stride=PACK_CHUNKS), :]
        if valid is not None:
            p = jnp.where(row0 + lax.broadcasted_iota(I32, p.shape, 0) < valid, p, jnp.uint32(0))
        lo.append(pltpu.bitcast(p << 16, F32))
        hi.append(pltpu.bitcast(p & jnp.uint32(0xFFFF0000), F32))
    return jnp.concatenate(lo + hi, axis=1)


def _ada_kernel(c_ref, w_ref, b_ref, o_ref):
    a = _silu(c_ref[...]).astype(BF16)
    o_ref[...] = jnp.dot(a, w_ref[...].astype(BF16), preferred_element_type=F32) + b_ref[...]


def _ada_mod(c_all, w_ada, b_ada):
    rows, d = c_all.shape
    n = w_ada.shape[1]
    tn = 1024
    return pl.pallas_call(
        _ada_kernel,
        grid=(n // tn,),
        in_specs=[
            pl.BlockSpec((rows, d), lambda j: (0, 0)),
            pl.BlockSpec((d, tn), lambda j: (0, j)),
            pl.BlockSpec((1, tn), lambda j: (0, j)),
        ],
        out_specs=pl.BlockSpec((rows, tn), lambda j: (0, j)),
        out_shape=jax.ShapeDtypeStruct((rows, n), F32),
        compiler_params=_params("parallel"),
        name="ada_mod",
    )(c_all, w_ada, b_ada.reshape(1, n))


def _modulated_norm(x, g, m, shift_row, scale_row):
    ms = jnp.mean(x * x, axis=-1, keepdims=True)
    return x * lax.rsqrt(ms + EPS) * g * (1.0 + m[scale_row:scale_row + 1, :]) + m[shift_row:shift_row + 1, :]


def _in_proj_kernel(x_ref, g_ref, mod_ref, wfl_ref, bf_ref, w_ref, hist_ref, wp_ref, sp_ref,
                    ya_ref, tail_ref, q_ref, k32_ref, kb_ref, v32_ref, vb_ref, gates_ref, lf_ref,
                    h_ref, full_ref, carry_ref, *, q_scale, start):
    li = pl.program_id(1)
    j = pl.program_id(2)
    nb, tl, _ = x_ref.shape
    nh = wfl_ref.shape[0]
    gw = wp_ref.shape[1]

    @pl.when(j == 0)
    def _():
        for bi in range(nb):
            hb = _modulated_norm(x_ref[bi], g_ref[...], mod_ref[bi], 0, 1).astype(BF16)
            h_ref[bi * tl:(bi + 1) * tl, :] = hb
            fl = lax.dot_general(wfl_ref[...], hb, NT_DIMS, preferred_element_type=F32) + bf_ref[...]
            lf_ref[bi] = jnp.minimum(fl, 0.0) - jnp.log(1.0 + jnp.exp(-jnp.abs(fl)))

    def tile_rows():
        acc = jnp.dot(h_ref[...], w_ref[...], preferred_element_type=F32)
        return [acc[bi * tl:(bi + 1) * tl, :] for bi in range(nb)]

    def store_heads(f32_ref, bf_ref_):
        for bi, r in enumerate(tile_rows()):
            bf_ref_[bi] = r.astype(BF16)
            for hd in range(nh):
                f32_ref[bi, pl.ds(hd, tl, stride=nh), :] = r[:, hd * HEAD_DIM:(hd + 1) * HEAD_DIM]

    @pl.when(j == 0)
    def _():
        pos = start + li * tl + lax.broadcasted_iota(I32, (tl, 1), 0)
        for bi, u in enumerate(tile_rows()):
            full_ref[0:HALO, :] = jnp.where(li == 0, hist_ref[bi], carry_ref[bi])
            full_ref[HALO:HALO + tl, :] = u
            carry_ref[bi] = u[tl - HALO:, :]
            tail_ref[bi] = u[tl - HALO:, :]
            for g, w in enumerate(POOL_WINDOWS):
                c0, c1 = g * gw, (g + 1) * gw
                cur = full_ref[HALO:HALO + tl, c0:c1]
                win = cur
                for d in range(1, w):
                    win = win + full_ref[HALO - d:HALO - d + tl, c0:c1]
                dmean = win / jnp.minimum(pos + 1, w).astype(F32) - cur
                y = jnp.dot(dmean.astype(BF16), wp_ref[g], preferred_element_type=F32) * sp_ref[:, c0:c1]
                ya_ref[bi, :, c0:c1] = y.astype(BF16)

    @pl.when(j == 1)
    def _():
        for bi, r in enumerate(tile_rows()):
            q_ref[bi] = (r * q_scale).astype(BF16)

    @pl.when(j == 2)
    def _():
        store_heads(k32_ref, kb_ref)

    @pl.when(j == 3)
    def _():
        store_heads(v32_ref, vb_ref)

    @pl.when(j >= 4)
    def _():
        for bi, r in enumerate(tile_rows()):
            gates_ref[bi] = _sigmoid(r).astype(BF16)


def _in_proj(x3, g, mod3, mod_row0, w_fl_t, b_f, w_cat, hist, w_pool, s_pool, start, q_scale, nb, tl):
    b, l, d = x3.shape
    nh = w_fl_t.shape[0]
    tn = nh * HEAD_DIM
    n_tiles = w_cat.shape[1] // tn
    n_gate_tiles = n_tiles - 4
    n_win, gw, _ = w_pool.shape
    assert mod_row0 % nb == 0 and b % nb == 0 and n_win * gw == tn and tl >= HALO
    tile = lambda width: pl.BlockSpec((nb, tl, width), lambda bb, li, j: (bb, li, 0))
    heads = pl.BlockSpec((nb, tl * nh, HEAD_DIM), lambda bb, li, j: (bb, li, 0))
    halo = pl.BlockSpec((nb, HALO, tn), lambda bb, li, j: (bb, 0, 0))
    const = lambda shape: pl.BlockSpec(shape, lambda bb, li, j: (0,) * len(shape))
    return pl.pallas_call(
        functools.partial(_in_proj_kernel, q_scale=q_scale, start=start),
        grid=(b // nb, l // tl, n_tiles),
        in_specs=[
            tile(d),
            const((1, d)),
            pl.BlockSpec((nb, N_MOD, d), lambda bb, li, j: (mod_row0 // nb + bb, 0, 0)),
            const((nh, d)),
            const((nh, 1)),
            pl.BlockSpec((d, tn), lambda bb, li, j: (0, j)),
            halo,
            const((n_win, gw, gw)),
            const((1, tn)),
        ],
        out_specs=[
            tile(tn), halo, tile(tn), heads, tile(tn), heads, tile(tn),
            pl.BlockSpec((nb, tl, tn), lambda bb, li, j: (bb, li, jnp.maximum(j - 4, 0))),
            pl.BlockSpec((nb, nh, tl), lambda bb, li, j: (bb, 0, li)),
        ],
        out_shape=[
            jax.ShapeDtypeStruct((b, l, tn), BF16),
            jax.ShapeDtypeStruct((b, HALO, tn), F32),
            jax.ShapeDtypeStruct((b, l, tn), BF16),
            jax.ShapeDtypeStruct((b, l * nh, HEAD_DIM), F32),
            jax.ShapeDtypeStruct((b, l, tn), BF16),
            jax.ShapeDtypeStruct((b, l * nh, HEAD_DIM), F32),
            jax.ShapeDtypeStruct((b, l, tn), BF16),
            jax.ShapeDtypeStruct((b, l, n_gate_tiles * tn), BF16),
            jax.ShapeDtypeStruct((b, nh, l), F32),
        ],
        scratch_shapes=[pltpu.VMEM((nb * tl, d), BF16), pltpu.VMEM((HALO + tl, tn), F32),
                        pltpu.VMEM((nb, HALO, tn), F32)],
        compiler_params=_params("parallel", "arbitrary", "arbitrary"),
        name="in_proj",
    )(x3, g.reshape(1, d), mod3, w_fl_t, b_f.reshape(nh, 1), w_cat, hist, w_pool, s_pool.reshape(1, tn))


def _mixer_out_kernel(ya_ref, yb_ref, ga_ref, gb_ref, x_ref, mod_ref, g_ref, wa_ref, wb_ref, wo_ref,
                      x1_ref, h2_ref, h2p_ref):
    merged = ga_ref[...].astype(F32) * jnp.dot(ya_ref[...], wa_ref[...], preferred_element_type=F32)
    merged = merged + gb_ref[...].astype(F32) * jnp.dot(yb_ref[...], wb_ref[...], preferred_element_type=F32)
    m = mod_ref[...]
    x1 = x_ref[...] + m[2:3, :] * jnp.dot(merged.astype(BF16), wo_ref[...], preferred_element_type=F32)
    x1_ref[...] = x1
    h2 = _modulated_norm(x1, g_ref[...], m, 3, 4)
    h2_ref[...] = h2.astype(BF16)
    _store_packed(h2p_ref, h2)


def _mixer_out(y_a, y_b, gates, x3, mod3, mod_row0, g_ffn, w_up_a, w_up_b, w_out, tl):
    b, l, d = x3.shape
    n_l = l // tl
    dp, da = y_a.shape[2], y_b.shape[2]
    tile = lambda width, col=0: pl.BlockSpec((None, tl, width), lambda bi, li: (bi, li, col))
    resident = lambda shape: pl.BlockSpec(shape, lambda bi, li: (0,) * len(shape), pipeline_mode=pl.Buffered(1))
    return pl.pallas_call(
        _mixer_out_kernel,
        grid=(b, n_l),
        in_specs=[
            tile(dp), tile(da), tile(d, 0), tile(d, 1), tile(d),
            pl.BlockSpec((None, N_MOD, d), lambda bi, li: (mod_row0 + bi, 0, 0)),
            pl.BlockSpec((1, d), lambda bi, li: (0, 0)),
            resident((dp, d)), resident((da, d)), resident((d, d)),
        ],
        out_specs=[
            tile(d), tile(d),
            pl.BlockSpec((tl * PACK_CHUNKS, LANES), lambda bi, li: (bi * n_l + li, 0)),
        ],
        out_shape=[
            jax.ShapeDtypeStruct((b, l, d), F32),
            jax.ShapeDtypeStruct((b, l, d), BF16),
            jax.ShapeDtypeStruct((b * l * PACK_CHUNKS, LANES), U32),
        ],
        compiler_params=_params("parallel", "parallel"),
        name="mixer_out",
    )(y_a, y_b, gates, gates, x3, mod3, g_ffn.reshape(1, d), w_up_a, w_up_b, w_out)


def _cumsum_kernel(x_ref, o_ref):
    x = x_ref[...]
    n = x.shape[1]
    idx = lax.broadcasted_iota(I32, x.shape, 1)
    s = 1
    while s < n:
        x = x + jnp.where(idx >= s, pltpu.roll(x, s, 1), 0.0)
        s *= 2
    o_ref[...] = x


def _cumsum_last(x3):
    b, h, n = x3.shape
    return pl.pallas_call(
        _cumsum_kernel,
        grid=(b,),
        in_specs=[pl.BlockSpec((None, h, n), lambda bi: (bi, 0, 0))],
        out_specs=pl.BlockSpec((None, h, n), lambda bi: (bi, 0, 0)),
        out_shape=jax.ShapeDtypeStruct((b, h, n), F32),
        compiler_params=_params("parallel"),
        name="cumsum_logf",
    )(x3)


def _attn_prompt_kernel(q_ref, k_ref, v_ref, fk_ref, o_ref, s0_ref, s1_ref, m_ref, l_ref, acc_ref, *, tq):
    qi = pl.program_id(2)
    tk = tq // 2
    n_chunks = tk // LANES

    def produce(j, s_ref, r0=0):
        off = pl.multiple_of(j * tk, tk)
        s = lax.dot_general(q_ref[r0:, :], k_ref[pl.ds(off, tk), :], NT_DIMS, preferred_element_type=F32)
        s_ref[r0:, :] = s - fk_ref[j] * LOG2_E

    def consume(j, s_ref, first_col=None, r0=0):
        s = s_ref[r0:, :]
        if first_col is not None:
            row = r0 + lax.broadcasted_iota(I32, s.shape, 0)
            col = first_col + lax.broadcasted_iota(I32, s.shape, 1)
            s = jnp.where(col <= row, s, NEG_INF)
        chunks = [s[:, c * LANES:(c + 1) * LANES] for c in range(n_chunks)]
        m_old = m_ref[r0:, :]
        m_new = jnp.maximum(m_old, jnp.max(functools.reduce(jnp.maximum, chunks), axis=1, keepdims=True))
        alpha = jnp.exp2(m_old - m_new)
        p_chunks = [jnp.exp2(ch - m_new) for ch in chunks]
        l_ref[r0:, :] = alpha * l_ref[r0:, :] + functools.reduce(lambda a, c: a + c, p_chunks)
        p = jnp.concatenate(p_chunks, axis=1).astype(BF16)
        v = v_ref[pl.ds(pl.multiple_of(j * tk, tk), tk), :]
        acc_ref[r0:, :] = alpha * acc_ref[r0:, :] + jnp.dot(p, v, preferred_element_type=F32)
        m_ref[r0:, :] = m_new

    m_ref[...] = jnp.full(m_ref.shape, NEG_INF, F32)
    l_ref[...] = jnp.zeros(l_ref.shape, F32)
    acc_ref[...] = jnp.zeros(acc_ref.shape, F32)
    produce(0, s0_ref)

    def pair(jj, carry):
        j0 = 2 * jj
        produce(j0 + 1, s1_ref)
        consume(j0, s0_ref)
        produce(j0 + 2, s0_ref)
        consume(j0 + 1, s1_ref)
        return carry

    lax.fori_loop(0, qi, pair, 0)
    produce(2 * qi + 1, s1_ref, r0=tk)
    consume(2 * qi, s0_ref, first_col=0)
    consume(2 * qi + 1, s1_ref, first_col=tk, r0=tk)

    l = jnp.sum(l_ref[...], axis=1, keepdims=True)
    o_ref[...] = (acc_ref[...] / l).astype(o_ref.dtype)


def _attn_prompt(q3, k3, v3, f3, tq):
    b, l, da = q3.shape
    nh = da // HEAD_DIM
    tk = tq // 2
    nk = l // tk
    fk = f3.reshape(b * nh, nk, 1, tk)
    return pl.pallas_call(
        functools.partial(_attn_prompt_kernel, tq=tq),
        grid=(b, nh, l // tq),
        in_specs=[
            pl.BlockSpec((None, tq, HEAD_DIM), lambda bi, hi, qi: (bi, qi, hi)),
            pl.BlockSpec((None, l, HEAD_DIM), lambda bi, hi, qi: (bi, 0, hi)),
            pl.BlockSpec((None, l, HEAD_DIM), lambda bi, hi, qi: (bi, 0, hi)),
            pl.BlockSpec((None, nk, 1, tk), lambda bi, hi, qi: (bi * nh + hi, 0, 0, 0)),
        ],
        out_specs=pl.BlockSpec((None, tq, HEAD_DIM), lambda bi, hi, qi: (bi, qi, hi)),
        out_shape=jax.ShapeDtypeStruct((b, l, da), BF16),
        scratch_shapes=[
            pltpu.VMEM((tq, tk), F32),
            pltpu.VMEM((tq, tk), F32),
            pltpu.VMEM((tq, LANES), F32),
            pltpu.VMEM((tq, LANES), F32),
            pltpu.VMEM((tq, HEAD_DIM), F32),
        ],
        compiler_params=_params("parallel", "parallel", "parallel"),
        name="fox_prompt",
    )(q3, k3, v3, fk)


def _attn_sample_kernel(q_ref, kp_ref, vp_ref, kn_ref, vn_ref, fp_ref, fn_ref, o_ref, m_ref, l_ref, acc_ref, *,
                        nh, pc, lq):
    c = pl.program_id(1)

    @pl.when(c == 0)
    def _():
        m_ref[...] = jnp.full(m_ref.shape, NEG_INF, F32)
        l_ref[...] = jnp.zeros(l_ref.shape, F32)
        acc_ref[...] = jnp.zeros(acc_ref.shape, F32)

    scores = []
    for h in range(nh):
        k = kp_ref[pl.ds(h, pc, stride=nh), :].astype(BF16)
        s = lax.dot_general(q_ref[:, h * HEAD_DIM:(h + 1) * HEAD_DIM], k, NT_DIMS, preferred_element_type=F32)
        scores.append(s - fp_ref[h:h + 1, :])
    probs, alphas = [], []
    for h, s in enumerate(scores):
        chunks = [s[:, j * LANES:(j + 1) * LANES] for j in range(pc // LANES)]
        m_old = m_ref[h]
        m_new = jnp.maximum(m_old, jnp.max(functools.reduce(jnp.maximum, chunks), axis=1, keepdims=True))
        alpha = jnp.exp(m_old - m_new)
        p_chunks = [jnp.exp(ch - m_new) for ch in chunks]
        l_ref[h] = alpha * l_ref[h] + functools.reduce(lambda a, b: a + b, p_chunks)
        m_ref[h] = m_new
        probs.append(jnp.concatenate(p_chunks, axis=1).astype(BF16))
        alphas.append(alpha)
    for h in range(nh):
        v = vp_ref[pl.ds(h, pc, stride=nh), :].astype(BF16)
        acc_ref[h] = alphas[h] * acc_ref[h] + jnp.dot(probs[h], v, preferred_element_type=F32)

    @pl.when(c == pl.num_programs(1) - 1)
    def _():
        row = lax.broadcasted_iota(I32, (lq, lq), 0)
        col = lax.broadcasted_iota(I32, (lq, lq), 1)
        for h in range(nh):
            cols = slice(h * HEAD_DIM, (h + 1) * HEAD_DIM)
            s = lax.dot_general(q_ref[:, cols], kn_ref[:, cols], NT_DIMS, preferred_element_type=F32)
            s = jnp.where(col <= row, s - fn_ref[h:h + 1, :lq], NEG_INF)
            m_old = m_ref[h]
            m_new = jnp.maximum(m_old, jnp.max(s, axis=1, keepdims=True))
            alpha = jnp.exp(m_old - m_new)
            p = jnp.exp(s - m_new[:, :lq])
            l = jnp.sum(alpha * l_ref[h], axis=1, keepdims=True) + jnp.sum(p, axis=1, keepdims=True)
            acc = alpha * acc_ref[h] + jnp.dot(p.astype(BF16), vn_ref[:, cols], preferred_element_type=F32)
            o_ref[:, cols] = (acc / l).astype(o_ref.dtype)


def _attn_sample(q3, kn3, vn3, cache_k, cache_v, f_past, f_new, pc):
    b, lq, da = q3.shape
    _, past, nh, dh = cache_k.shape
    nc = past // pc
    new_spec = pl.BlockSpec((None, lq, da), lambda bi, ci: (bi, 0, 0))
    past_spec = pl.BlockSpec((None, pc * nh, dh), lambda bi, ci: (bi, ci, 0))
    return pl.pallas_call(
        functools.partial(_attn_sample_kernel, nh=nh, pc=pc, lq=lq),
        grid=(b, nc),
        in_specs=[new_spec, past_spec, past_spec, new_spec, new_spec,
                  pl.BlockSpec((None, None, nh, pc), lambda bi, ci: (bi, ci, 0, 0)),
                  pl.BlockSpec((None, nh, LANES), lambda bi, ci: (bi, 0, 0))],
        out_specs=new_spec,
        out_shape=jax.ShapeDtypeStruct((b, lq, da), BF16),
        scratch_shapes=[pltpu.VMEM((nh, lq, LANES), F32), pltpu.VMEM((nh, lq, LANES), F32),
                        pltpu.VMEM((nh, lq, HEAD_DIM), F32)],
        compiler_params=_params("parallel", "arbitrary"),
        name="fox_sample",
    )(q3, cache_k.reshape(b, past * nh, dh), cache_v.reshape(b, past * nh, dh), kn3, vn3, f_past, f_new)


def _router_kernel(h_ref, wr_ref, br_ref, tri_ref, cnt0_ref, idx_ref, gate_ref, rank_ref, cnt_ref, carry_ref):
    @pl.when(pl.program_id(0) == 0)
    def _():
        carry_ref[...] = cnt0_ref[...]

    n_exp, tm = wr_ref.shape[0], h_ref.shape[0]
    scores = jax.nn.sigmoid(lax.dot_general(wr_ref[...], h_ref[...], NT_DIMS, preferred_element_type=F32))
    work = scores + br_ref[...]
    eidx = lax.broadcasted_iota(I32, (n_exp, tm), 0)
    picked = jnp.zeros((n_exp, tm), F32)
    onehots, gates = [], []
    for k in range(TOP_K):
        best = jnp.max(work, axis=0, keepdims=True)
        first = jnp.min(jnp.where(work == best, eidx, n_exp), axis=0, keepdims=True)
        onehot = eidx == first
        idx_ref[k:k + 1, :] = first
        gates.append(jnp.sum(jnp.where(onehot, scores, 0.0), axis=0, keepdims=True))
        onehots.append(onehot)
        picked = picked + jnp.where(onehot, 1.0, 0.0)
        work = jnp.where(onehot, NEG_INF, work)
    norm = ROUTED_SCALE / functools.reduce(lambda a, c: a + c, gates)
    cum = jnp.dot(picked.astype(BF16), tri_ref[...], preferred_element_type=F32)
    before = cum - picked + carry_ref[:, 0:1]
    for k in range(TOP_K):
        gate_ref[k:k + 1, :] = gates[k] * norm
        rank_ref[k:k + 1, :] = jnp.sum(jnp.where(onehots[k], before, 0.0), axis=0, keepdims=True).astype(I32)
    carry_ref[...] = carry_ref[...] + cum[:, tm - 1:tm]
    cnt_ref[...] = carry_ref[...]


def _router(h2, tok0, ntok, w_router_t, b_router, cnt0, tm):
    d = h2.shape[1]
    n_exp = w_router_t.shape[0]
    tri = jnp.triu(jnp.ones((tm, tm), BF16))
    tok_spec = pl.BlockSpec((TOP_K, tm), lambda i: (0, i))
    cnt_spec = pl.BlockSpec((n_exp, LANES), lambda i: (0, 0))
    return pl.pallas_call(
        _router_kernel,
        grid=(ntok // tm,),
        in_specs=[
            pl.BlockSpec((tm, d), lambda i: (tok0 // tm + i, 0)),
            pl.BlockSpec((n_exp, d), lambda i: (0, 0)),
            pl.BlockSpec((n_exp, 1), lambda i: (0, 0)),
            pl.BlockSpec((tm, tm), lambda i: (0, 0)),
            cnt_spec,
        ],
        out_specs=[tok_spec, tok_spec, tok_spec, cnt_spec],
        out_shape=[
            jax.ShapeDtypeStruct((TOP_K, ntok), I32),
            jax.ShapeDtypeStruct((TOP_K, ntok), F32),
            jax.ShapeDtypeStruct((TOP_K, ntok), I32),
            jax.ShapeDtypeStruct((n_exp, LANES), F32),
        ],
        scratch_shapes=[pltpu.VMEM((n_exp, LANES), F32)],
        compiler_params=_params("arbitrary"),
        name="moe_router",
    )(h2, w_router_t, b_router.reshape(n_exp, 1), tri, cnt0)


LAYOUT_ROWS = 8


def _layout_kernel(cnt_ref, o_ref):
    n_exp = cnt_ref.shape[0]
    lane = lax.broadcasted_iota(I32, o_ref.shape, 1)
    row0 = lane * MOE_ROWS

    def forward(e, carry):
        start, blk_e, valid, pad_start = carry
        cnt = cnt_ref[e]
        end = start + (cnt + MOE_ROWS - 1) // MOE_ROWS * MOE_ROWS
        in_span = lambda new, old: jnp.where(row0 >= start, jnp.where(row0 < end, new, old), old)
        blk_e = in_span(e, blk_e)
        valid = in_span(jnp.clip(start + cnt - row0, 0, MOE_ROWS), valid)
        return end, blk_e, valid, jnp.where(lane == e, start, pad_start)

    zeros = jnp.zeros(o_ref.shape, I32)
    end, blk_e, valid, pad_start = lax.fori_loop(0, n_exp, forward, (jnp.int32(0), zeros + (n_exp - 1), zeros, zeros))

    def backward(t, carry):
        nxt, blk_next = carry
        e = n_exp - 1 - t
        return jnp.where(cnt_ref[e] > 0, e, nxt), jnp.where(blk_e == e, nxt, blk_next)

    _, blk_next = lax.fori_loop(0, n_exp, backward, (jnp.int32(-1), zeros - 1))
    table = zeros + end // MOE_ROWS
    sub = lax.broadcasted_iota(I32, o_ref.shape, 0)
    for r, row in enumerate((blk_e, valid, blk_next, pad_start)):
        table = jnp.where(sub == r, row, table)
    o_ref[...] = table


def _layout(counts, n_blocks):
    n_exp = counts.shape[0]
    w = -(-max(n_blocks, n_exp) // LANES) * LANES
    table = pl.pallas_call(
        _layout_kernel,
        grid_spec=pltpu.PrefetchScalarGridSpec(
            num_scalar_prefetch=1, grid=(1,), in_specs=[],
            out_specs=pl.BlockSpec((LAYOUT_ROWS, w), lambda i, cnt: (0, 0))),
        out_shape=jax.ShapeDtypeStruct((LAYOUT_ROWS, w), I32),
        compiler_params=_params("arbitrary"),
        name="moe_layout",
    )(counts)
    return table[0, :n_blocks], table[1, :n_blocks], table[2, :n_blocks], table[3, :n_exp], table[4, :1]


def _slot_rows_kernel(ps_ref, idx_ref, rank_ref, o_ref):
    idx = idx_ref[...]
    rows = rank_ref[...]
    for e in range(ps_ref.shape[0]):
        rows = rows + jnp.where(idx == e, ps_ref[e], 0)
    o_ref[...] = rows


def _slot_rows(pad_start, idx_t, rank_t):
    k, ntok = idx_t.shape
    tm = min(2048, ntok)
    spec = pl.BlockSpec((k, tm), lambda i, ps: (0, i))
    return pl.pallas_call(
        _slot_rows_kernel,
        grid_spec=pltpu.PrefetchScalarGridSpec(num_scalar_prefetch=1, grid=(ntok // tm,), in_specs=[spec, spec],
                                               out_specs=spec),
        out_shape=jax.ShapeDtypeStruct((k, ntok), I32),
        compiler_params=_params("parallel"),
        name="moe_slot_rows",
    )(pad_start, idx_t, rank_t)


def _moe_kernel(be_ref, bv_ref, nx_ref, nu_ref, x_ref, wg_hbm, wu_hbm, wd_hbm, y_ref,
                wg_f32, wu_f32, wd_f32, wg_bf, wu_bf, wd_bf, sems):
    i = pl.program_id(0)
    active = i < nu_ref[0]
    expert = be_ref[i]

    def fetch(e):
        pairs = ((wg_hbm, wg_f32), (wu_hbm, wu_f32), (wd_hbm, wd_f32))
        return [pltpu.make_async_copy(w.at[e], buf, sems.at[n]) for n, (w, buf) in enumerate(pairs)]

    @pl.when(active & (i == 0))
    def _():
        for c in fetch(expert):
            c.start()

    @pl.when(active & ((i == 0) | (expert != be_ref[jnp.maximum(i - 1, 0)])))
    def _():
        for c in fetch(expert):
            c.wait()
        wg_bf[...] = wg_f32[...].astype(BF16)
        wu_bf[...] = wu_f32[...].astype(BF16)
        wd_bf[...] = wd_f32[...].astype(BF16)

        @pl.when(nx_ref[i] >= 0)
        def _():
            for c in fetch(nx_ref[i]):
                c.start()

    def chain(row0, rows):
        x = _load_packed(x_ref, rows, valid=bv_ref[i], row0=row0).astype(BF16)
        g = jnp.dot(x, wg_bf[...], preferred_element_type=F32)
        u = jnp.dot(x, wu_bf[...], preferred_element_type=F32)
        a = (_silu(g) * u).astype(BF16)
        _store_packed(y_ref, jnp.dot(a, wd_bf[...], preferred_element_type=F32), row0=row0)

    starts = range(0, MOE_ROWS, MOE_CHAIN_ROWS)
    full = bv_ref[i] > starts[-1]
    pl.when(active & full)(functools.partial(chain, 0, MOE_ROWS))
    for row0 in starts[:-1]:
        pl.when(active & jnp.logical_not(full) & (bv_ref[i] > row0))(functools.partial(chain, row0, MOE_CHAIN_ROWS))


def _moe_experts(x_sorted, blk_e, blk_valid, blk_next, n_used, w_e_gate, w_e_up, w_e_down):
    cap = x_sorted.shape[0] // PACK_CHUNKS
    d, de = w_e_gate.shape[1:]
    n_blocks = cap // MOE_ROWS
    row_map = lambda i, be, bv, nx, nu: (jnp.minimum(i, nu[0] - 1), 0)
    hbm = pl.BlockSpec(memory_space=pl.ANY)
    return pl.pallas_call(
        _moe_kernel,
        grid_spec=pltpu.PrefetchScalarGridSpec(
            num_scalar_prefetch=4,
            grid=(n_blocks,),
            in_specs=[pl.BlockSpec((MOE_ROWS * PACK_CHUNKS, LANES), row_map), hbm, hbm, hbm],
            out_specs=pl.BlockSpec((MOE_ROWS * PACK_CHUNKS, LANES), row_map),
            scratch_shapes=[
                pltpu.VMEM((d, de), F32), pltpu.VMEM((d, de), F32), pltpu.VMEM((de, d), F32),
                pltpu.VMEM((d, de), BF16), pltpu.VMEM((d, de), BF16), pltpu.VMEM((de, d), BF16),
                pltpu.SemaphoreType.DMA((3,)),
            ],
        ),
        out_shape=jax.ShapeDtypeStruct(x_sorted.shape, U32),
        compiler_params=_params("arbitrary"),
        name="moe_experts",
    )(blk_e, blk_valid, blk_next, n_used, x_sorted, w_e_gate, w_e_up, w_e_down)


def _sc_mesh():
    return plsc.VectorSubcoreMesh(core_axis_name="c", subcore_axis_name="s")


def _sc_worker_id():
    return lax.axis_index("s") * SC_CORES + lax.axis_index("c")


def _sc_scatter_rows(segments, n_out):
    row_shape = segments[0][0].shape[1:]
    dtype = segments[0][0].dtype
    n_seg = len(segments)
    geom = [(tok0, dw.shape[0] // SC_WORKERS, dw.shape[2]) for _, tok0, dw in segments]
    scratch = []
    for _, n_win, win in geom:
        scratch += [pltpu.VMEM((n_win, TOP_K, win), I32), pltpu.VMEM((win,) + row_shape, dtype)]

    @functools.partial(
        pl.kernel, mesh=_sc_mesh(),
        out_type=jax.ShapeDtypeStruct((n_out,) + row_shape, dtype),
        scratch_types=scratch + [pltpu.SemaphoreType.DMA],
    )
    def scatter_kernel(*refs):
        out_hbm = refs[2 * n_seg]
        sem = refs[-1]
        for s, (tok0, n_win, win) in enumerate(geom):
            rows_hbm, idx_hbm = refs[2 * s], refs[2 * s + 1]
            idx_v, rows_v = refs[2 * n_seg + 1 + 2 * s], refs[2 * n_seg + 2 + 2 * s]
            first = _sc_worker_id() * n_win
            pltpu.sync_copy(idx_hbm.at[pl.ds(first, n_win)], idx_v)

            @pl.loop(0, n_win)
            def _(w):
                pltpu.sync_copy(rows_hbm.at[pl.ds(tok0 + (first + w) * win, win)], rows_v)
                copies = [pltpu.make_async_copy(rows_v, out_hbm.at[idx_v.at[w, k]], sem) for k in range(TOP_K)]
                for c in copies:
                    c.start()
                for c in copies:
                    c.wait()

    args = []
    for rows, _, dest_win in segments:
        args += [rows, dest_win]
    return scatter_kernel(*args)


def _sc_gather_rows(table, idx_wins):
    row_shape = table.shape[1:]
    n_seg = len(idx_wins)
    geom = [(iw.shape[0] // SC_WORKERS, iw.shape[1]) for iw in idx_wins]
    scratch = []
    for n_win, win in geom:
        scratch += [pltpu.VMEM((n_win, win), I32), pltpu.VMEM((win,) + row_shape, table.dtype)]

    @functools.partial(
        pl.kernel, mesh=_sc_mesh(),
        out_type=[jax.ShapeDtypeStruct((iw.shape[0] * iw.shape[1],) + row_shape, table.dtype) for iw in idx_wins],
        scratch_types=scratch + [pltpu.SemaphoreType.DMA],
    )
    def gather_kernel(*refs):
        table_hbm = refs[0]
        sem = refs[-1]
        for s, (n_win, win) in enumerate(geom):
            idx_hbm, out_hbm = refs[1 + s], refs[1 + n_seg + s]
            idx_v, rows_v = refs[1 + 2 * n_seg + 2 * s], refs[2 + 2 * n_seg + 2 * s]
            first = _sc_worker_id() * n_win
            pltpu.sync_copy(idx_hbm.at[pl.ds(first, n_win)], idx_v)

            @pl.loop(0, n_win)
            def _(w):
                pltpu.async_copy(table_hbm.at[idx_v.at[w]], rows_v, sem).wait()
                pltpu.sync_copy(rows_v, out_hbm.at[pl.ds((first + w) * win, win)])

    return gather_kernel(table, *idx_wins)


def _final_kernel(x1_ref, h2_ref, yt_ref, gate_ref, mod_ref, wsg_ref, wsu_ref, wsd_ref, gf_ref, *rest):
    o_ref = rest[-1]
    h2 = h2_ref[...]
    g = jnp.dot(h2, wsg_ref[...], preferred_element_type=F32)
    u = jnp.dot(h2, wsu_ref[...], preferred_element_type=F32)
    f = jnp.dot((_silu(g) * u).astype(BF16), wsd_ref[...], preferred_element_type=F32)
    gate = gate_ref[...]
    tl = h2.shape[0]
    for k in range(TOP_K):
        f = f + gate[:, k:k + 1] * _load_packed(yt_ref.at[k], tl)
    x2 = x1_ref[...] + mod_ref[N_MOD - 1:N_MOD, :] * f
    ms = jnp.mean(x2 * x2, axis=-1, keepdims=True)
    o_ref[...] = x2 * lax.rsqrt(ms + EPS) * gf_ref[...]


def _final(x1, h2, b0, y_tok, gate, mod3, mod_row0, w_s_gate, w_s_up, w_s_down, g_final, tl, y_prev=None):
    b, l, d = x1.shape
    nb = gate.shape[0]
    ds = w_s_gate.shape[1]
    tile = pl.BlockSpec((None, tl, d), lambda bi, li: (b0 + bi, li, 0))
    const = lambda shape: pl.BlockSpec(shape, lambda bi, li: (0,) * len(shape), pipeline_mode=pl.Buffered(1))
    in_specs = [
        tile,
        tile,
        pl.BlockSpec((TOP_K, None, tl * PACK_CHUNKS, LANES), lambda bi, li: (0, bi, li, 0)),
        pl.BlockSpec((None, tl, TOP_K), lambda bi, li: (bi, li, 0)),
        pl.BlockSpec((None, N_MOD, d), lambda bi, li: (mod_row0 + b0 + bi, 0, 0)),
        const((d, ds)),
        const((d, ds)),
        const((ds, d)),
        const((1, d)),
    ]
    args = [x1, h2, y_tok, gate, mod3, w_s_gate, w_s_up, w_s_down, g_final.reshape(1, d)]
    aliases = {}
    if y_prev is not None:
        in_specs.append(pl.BlockSpec(memory_space=pl.ANY))
        aliases = {len(args): 0}
        args.append(y_prev)
    return pl.pallas_call(
        _final_kernel,
        grid=(nb, l // tl),
        in_specs=in_specs,
        out_specs=tile,
        out_shape=jax.ShapeDtypeStruct((b, l, d), F32),
        input_output_aliases=aliases,
        compiler_params=_params("parallel", "parallel"),
        name="combine_final",
    )(*args)


def _group_mixer(x3, mod3, mod_row0, hist, start, past, wts, tiles):
    b, l, d = x3.shape
    tl = tiles["tl"]
    dp = wts["w_pool"].shape[0] * wts["w_pool"].shape[1]
    da = wts["w_up_b"].shape[0]
    nh = da // HEAD_DIM
    t = b * l

    assert dp == da, "u, q, k, v must each be one column tile of the input projection"
    scale = HEAD_DIM ** -0.5 * (LOG2_E if past is None else 1.0)
    y_a, u_tail, qb, k, kb, v, vb, gates, logf_t = _in_proj(
        x3, wts["g_mix"], mod3, mod_row0, wts["w_fl_t"], wts["b_f"], wts["w_cat"], hist, wts["w_pool"],
        wts["s_pool"], start, scale, tiles["nb_in"], tl)

    if past is None:
        y_b = _attn_prompt(qb, kb, vb, _cumsum_last(logf_t), tiles["tq"])
    else:
        cache_k, cache_v, cache_logf = past
        p = cache_k.shape[1]
        pc = tiles["past_chunk"]
        assert l <= LANES and p % pc == 0 and p % LANES == 0
        lf_all = jnp.concatenate([jnp.swapaxes(cache_logf.astype(F32), 1, 2), logf_t,
                                  jnp.zeros((b, nh, LANES - l), F32)], axis=2)
        f_all = _cumsum_last(lf_all)
        f_past = f_all[:, :, :p].reshape(b, nh, p // pc, pc).transpose(0, 2, 1, 3)
        y_b = _attn_sample(qb, kb, vb, cache_k, cache_v, f_past, f_all[:, :, p:], pc)

    x1, h2, h2_packed = _mixer_out(y_a, y_b, gates, x3, mod3, mod_row0, wts["g_ffn"], wts["w_up_a"], wts["w_up_b"],
                                   wts["w_out"], tiles["tl_out"])
    caches = (u_tail[:, HALO - POOL_HIST:, :], k.reshape(b, l, nh, HEAD_DIM), v.reshape(b, l, nh, HEAD_DIM),
              jnp.swapaxes(logf_t, 1, 2))
    return dict(x1=x1, h2=h2, h2_packed=h2_packed, mod_row0=mod_row0, tiles=tiles), caches


def _moe_chunk(segments, wts):
    n_exp = wts["w_router_t"].shape[0]
    row_shape = (PACK_CHUNKS, LANES)
    counts_f = jnp.zeros((n_exp, LANES), F32)
    routed = []
    for grp, b0, nb in segments:
        b, l, d = grp["h2"].shape
        ntok = nb * l
        idx_t, gate_t, rank_t, counts_f = _router(grp["h2"].reshape(b * l, d), b0 * l, ntok, wts["w_router_t"],
                                                  wts["b_router"], counts_f, min(grp["tiles"]["t_route"], ntok))
        routed.append((idx_t, gate_t, rank_t, ntok))
    counts = counts_f[:, 0].astype(I32)

    n_slots = sum(r[3] for r in routed) * TOP_K
    n_blocks = -(-n_slots // MOE_ROWS) + n_exp
    cap = n_blocks * MOE_ROWS
    blk_e, blk_valid, blk_next, pad_start, n_used = _layout(counts, n_blocks)

    scatter_segs, gather_idx = [], []
    for (grp, b0, nb), (idx_t, _, rank_t, ntok) in zip(segments, routed):
        b, l, _ = grp["h2"].shape
        dest = _slot_rows(pad_start, idx_t, rank_t)
        win_d = min(SC_WINDOW, ntok // SC_WORKERS)
        dest_win = dest.reshape(TOP_K, ntok // win_d, win_d).transpose(1, 0, 2)
        scatter_segs.append((grp["h2_packed"].reshape((b * l,) + row_shape), b0 * l, dest_win))
        win_c = min(SC_WINDOW, ntok * TOP_K // SC_WORKERS)
        gather_idx.append(dest.reshape(ntok * TOP_K // win_c, win_c))

    x_sorted = _sc_scatter_rows(scatter_segs, cap)
    y_sorted = _moe_experts(x_sorted.reshape(cap * PACK_CHUNKS, LANES), blk_e, blk_valid, blk_next, n_used,
                            wts["w_e_gate"], wts["w_e_up"], wts["w_e_down"])
    y_toks = _sc_gather_rows(y_sorted.reshape((cap,) + row_shape), gather_idx)
    out = []
    for (grp, b0, nb), (_, gate_t, _, _), y_tok in zip(segments, routed, y_toks):
        l = grp["h2"].shape[1]
        out.append((y_tok.reshape(TOP_K, nb, l * PACK_CHUNKS, LANES), gate_t.T.reshape(nb, l, TOP_K)))
    return out


def kernel(x_prompt, x_sample, cache_pool, cache_k, cache_v, cache_logf, c_prompt, c_sample, w_ada, b_ada, g_mix, w_in, b_f, w_pool, s_pool, w_up_a, w_up_b, w_out, g_ffn, w_router, b_router, w_e_gate, w_e_up, w_e_down, w_s_gate, w_s_up, w_s_down, g_final):
    depth = w_ada.shape[0]
    assert depth == 1, "a single layer is supported"
    bp, lp, d = x_prompt.shape
    bs, ls, _ = x_sample.shape
    dp = w_pool.shape[1] * w_pool.shape[2]
    da = w_up_b.shape[1]
    nh = da // HEAD_DIM
    assert bp <= MOD_ROWS_SAMPLE

    one = lambda a: a.reshape(a.shape[1:])
    c_all = jnp.zeros((MOD_ROWS_SAMPLE + bs, d), F32).at[:bp].set(c_prompt).at[MOD_ROWS_SAMPLE:].set(c_sample)
    mod3 = _ada_mod(c_all, one(w_ada), one(b_ada)).reshape(c_all.shape[0], N_MOD, d)

    n_main = dp + 3 * da
    w_in1 = one(w_in)
    wts = dict(
        g_mix=one(g_mix), g_ffn=one(g_ffn), b_f=one(b_f), s_pool=one(s_pool), b_router=one(b_router),
        w_cat=jnp.concatenate([w_in1[:, :n_main], w_in1[:, n_main + nh:]], axis=1).astype(BF16),
        w_fl_t=w_in1[:, n_main:n_main + nh].T.astype(BF16),
        w_pool=one(w_pool).astype(BF16),
        w_up_a=one(w_up_a).astype(BF16), w_up_b=one(w_up_b).astype(BF16), w_out=one(w_out).astype(BF16),
        w_router_t=one(w_router).T.astype(BF16),
        w_e_gate=one(w_e_gate), w_e_up=one(w_e_up), w_e_down=one(w_e_down),
        w_s_gate=one(w_s_gate).astype(BF16), w_s_up=one(w_s_up).astype(BF16), w_s_down=one(w_s_down).astype(BF16),
    )

    tiles_p = dict(tl=min(512, lp), tq=min(1024, lp), t_route=min(512, bp * lp), tl_out=min(256, lp),
                   tl_final=min(256, lp), nb_in=1)
    tiles_s = dict(tl=ls, t_route=bs * ls, tl_out=ls, tl_final=ls, nb_in=bs, past_chunk=min(512, cache_k.shape[2]))

    hist_p = jnp.zeros((bp, HALO, dp), F32)
    grp_p, caches_p = _group_mixer(x_prompt, mod3, MOD_ROWS_PROMPT, hist_p, 0, None, wts, tiles_p)
    hist_s = jnp.pad(one(cache_pool), ((0, 0), (HALO - POOL_HIST, 0), (0, 0)))
    past = (one(cache_k), one(cache_v), one(cache_logf))
    grp_s, caches_s = _group_mixer(x_sample, mod3, MOD_ROWS_SAMPLE, hist_s, cache_k.shape[2], past, wts, tiles_s)

    n_chunks = MOE_CHUNKS if bp % MOE_CHUNKS == 0 else 1
    nb_c = bp // n_chunks
    chunks = [[(grp_p, c * nb_c, nb_c)] for c in range(n_chunks)]
    chunks[-1].append((grp_s, 0, bs))
    outs = {id(grp_p): None, id(grp_s): None}
    for segments in chunks:
        for (grp, b0, nb), (y_tok, gate) in zip(segments, _moe_chunk(segments, wts)):
            outs[id(grp)] = _final(grp["x1"], grp["h2"], b0, y_tok, gate, mod3, grp["mod_row0"], wts["w_s_gate"],
                                   wts["w_s_up"], wts["w_s_down"], g_final, grp["tiles"]["tl_final"],
                                   y_prev=outs[id(grp)])
    stack = lambda a: a[None]
    return (outs[id(grp_p)], outs[id(grp_s)], *map(stack, caches_p), *map(stack, caches_s))
```

```python
import functools

import jax
import jax.numpy as jnp
from jax import lax
from jax.experimental import pallas as pl
from jax.experimental.pallas import tpu as pltpu
from jax.experimental.pallas import tpu_sc as plsc

F32 = jnp.float32
BF16 = jnp.bfloat16
I32 = jnp.int32
U32 = jnp.uint32

EPS = 1e-6
N_MOD = 6
POOL_WINDOWS = (2, 4, 8, 16)
POOL_HIST = max(POOL_WINDOWS) - 1
HALO = 16
HEAD_DIM = 128
TOP_K = 8
ROUTED_SCALE = 2.5
MOE_ROWS = 1024
MOE_CHAIN_ROWS = 256
MOE_CHUNKS = 2
MOD_ROWS_PROMPT = 0
MOD_ROWS_SAMPLE = 8
VMEM_LIMIT_BYTES = 52 * 1024 * 1024
NEG_INF = float("-inf")
LOG2_E = 1.4426950408889634
LANES = 128
PACK_CHUNKS = 8
SC_CORES = 2
SC_SUBCORES = 16
SC_WORKERS = SC_CORES * SC_SUBCORES
SC_WINDOW = 32
NT_DIMS = (((1,), (1,)), ((), ()))


def _params(*semantics):
    return pltpu.CompilerParams(dimension_semantics=semantics, vmem_limit_bytes=VMEM_LIMIT_BYTES)


def _sigmoid(x):
    return 0.5 * jnp.tanh(0.5 * x) + 0.5


def _silu(x):
    return x * _sigmoid(x)


def _store_packed(ref, x, row0=0):
    rows, width = x.shape
    half = width // 2
    bits = pltpu.bitcast(x.astype(BF16).astype(F32), U32)
    packed = (bits[:, :half] >> 16) | (bits[:, half:] & jnp.uint32(0xFFFF0000))
    for j in range(PACK_CHUNKS):
        ref[pl.ds(row0 * PACK_CHUNKS + j, rows, stride=PACK_CHUNKS), :] = packed[:, j * LANES:(j + 1) * LANES]


def _load_packed(ref, rows, valid=None, row0=0):
    lo, hi = [], []
    for j in range(PACK_CHUNKS):
        p = ref[pl.ds(row0 * PACK_CHUNKS + j, rows, stride=PACK_CHUNKS), :]
        if valid is not None:
            p = jnp.where(row0 + lax.broadcasted_iota(I32, p.shape, 0) < valid, p, jnp.uint32(0))
        lo.append(pltpu.bitcast(p << 16, F32))
        hi.append(pltpu.bitcast(p & jnp.uint32(0xFFFF0000), F32))
    return jnp.concatenate(lo + hi, axis=1)


def _ada_kernel(c_ref, w_ref, b_ref, o_ref):
    a = _silu(c_ref[...]).astype(BF16)
    o_ref[...] = jnp.dot(a, w_ref[...].astype(BF16), preferred_element_type=F32) + b_ref[...]


def _ada_mod(c_all, w_ada, b_ada):
    rows, d = c_all.shape
    n = w_ada.shape[1]
    tn = 1024
    return pl.pallas_call(
        _ada_kernel,
        grid=(n // tn,),
        in_specs=[
            pl.BlockSpec((rows, d), lambda j: (0, 0)),
            pl.BlockSpec((d, tn), lambda j: (0, j)),
            pl.BlockSpec((1, tn), lambda j: (0, j)),
        ],
        out_specs=pl.BlockSpec((rows, tn), lambda j: (0, j)),
        out_shape=jax.ShapeDtypeStruct((rows, n), F32),
        compiler_params=_params("parallel"),
        name="ada_mod",
    )(c_all, w_ada, b_ada.reshape(1, n))


def _modulated_norm(x, g, m, shift_row, scale_row):
    ms = jnp.mean(x * x, axis=-1, keepdims=True)
    return x * lax.rsqrt(ms + EPS) * g * (1.0 + m[scale_row:scale_row + 1, :]) + m[shift_row:shift_row + 1, :]


def _in_proj_kernel(x_ref, g_ref, mod_ref, wfl_ref, bf_ref, wm_ref, wg_ref, hist_ref, wp_ref, sp_ref,
                    ya_ref, tail_ref, q_ref, k32_ref, kb_ref, v32_ref, vb_ref, gates_ref, lf_ref,
                    h_ref, full_ref, carry_ref, *, q_scale, start):
    li = pl.program_id(1)
    j = pl.program_id(2)
    nb, tl, _ = x_ref.shape
    nh = wfl_ref.shape[0]
    gw = wp_ref.shape[1]

    @pl.when(j == 0)
    def _():
        for bi in range(nb):
            hb = _modulated_norm(x_ref[bi], g_ref[...], mod_ref[bi], 0, 1).astype(BF16)
            h_ref[bi * tl:(bi + 1) * tl, :] = hb
            fl = lax.dot_general(wfl_ref[...], hb, NT_DIMS, preferred_element_type=F32) + bf_ref[...]
            lf_ref[bi] = jnp.minimum(fl, 0.0) - jnp.log(1.0 + jnp.exp(-jnp.abs(fl)))

    def tile_rows(w_ref=wm_ref):
        acc = jnp.dot(h_ref[...], w_ref[...], preferred_element_type=F32)
        return [acc[bi * tl:(bi + 1) * tl, :] for bi in range(nb)]

    def store_heads(f32_ref, bf_ref_):
        for bi, r in enumerate(tile_rows()):
            bf_ref_[bi] = r.astype(BF16)
            for hd in range(nh):
                f32_ref[bi, pl.ds(hd, tl, stride=nh), :] = r[:, hd * HEAD_DIM:(hd + 1) * HEAD_DIM]

    @pl.when(j == 0)
    def _():
        pos = start + li * tl + lax.broadcasted_iota(I32, (tl, 1), 0)
        for bi, u in enumerate(tile_rows()):
            full_ref[0:HALO, :] = jnp.where(li == 0, hist_ref[bi], carry_ref[bi])
            full_ref[HALO:HALO + tl, :] = u
            carry_ref[bi] = u[tl - HALO:, :]
            tail_ref[bi] = u[tl - HALO:, :]
            for g, w in enumerate(POOL_WINDOWS):
                c0, c1 = g * gw, (g + 1) * gw
                cur = full_ref[HALO:HALO + tl, c0:c1]
                win = cur
                for d in range(1, w):
                    win = win + full_ref[HALO - d:HALO - d + tl, c0:c1]
                dmean = win / jnp.minimum(pos + 1, w).astype(F32) - cur
                y = jnp.dot(dmean.astype(BF16), wp_ref[g], preferred_element_type=F32) * sp_ref[:, c0:c1]
                ya_ref[bi, :, c0:c1] = y.astype(BF16)

    @pl.when(j == 1)
    def _():
        for bi, r in enumerate(tile_rows()):
            q_ref[bi] = (r * q_scale).astype(BF16)

    @pl.when(j == 2)
    def _():
        store_heads(k32_ref, kb_ref)

    @pl.when(j == 3)
    def _():
        store_heads(v32_ref, vb_ref)

    @pl.when(j >= 4)
    def _():
        for bi, r in enumerate(tile_rows(wg_ref)):
            gates_ref[bi] = _sigmoid(r).astype(BF16)


def _in_proj(x3, g, mod3, mod_row0, w_fl_t, b_f, w_main, w_gates, hist, w_pool, s_pool, start, q_scale, nb, tl):
    b, l, d = x3.shape
    nh = w_fl_t.shape[0]
    tn = nh * HEAD_DIM
    n_main_tiles = w_main.shape[1] // tn
    n_gate_tiles = w_gates.shape[1] // tn
    n_tiles = n_main_tiles + n_gate_tiles
    assert n_main_tiles == 4
    n_win, gw, _ = w_pool.shape
    assert mod_row0 % nb == 0 and b % nb == 0 and n_win * gw == tn and tl >= HALO
    tile = lambda width: pl.BlockSpec((nb, tl, width), lambda bb, li, j: (bb, li, 0))
    heads = pl.BlockSpec((nb, tl * nh, HEAD_DIM), lambda bb, li, j: (bb, li, 0))
    halo = pl.BlockSpec((nb, HALO, tn), lambda bb, li, j: (bb, 0, 0))
    const = lambda shape: pl.BlockSpec(shape, lambda bb, li, j: (0,) * len(shape))
    return pl.pallas_call(
        functools.partial(_in_proj_kernel, q_scale=q_scale, start=start),
        grid=(b // nb, l // tl, n_tiles),
        in_specs=[
            tile(d),
            const((1, d)),
            pl.BlockSpec((nb, N_MOD, d), lambda bb, li, j: (mod_row0 // nb + bb, 0, 0)),
            const((nh, d)),
            const((nh, 1)),
            pl.BlockSpec((d, tn), lambda bb, li, j: (0, jnp.minimum(j, n_main_tiles - 1))),
            pl.BlockSpec((d, tn), lambda bb, li, j: (0, jnp.maximum(j - n_main_tiles, 0))),
            halo,
            const((n_win, gw, gw)),
            const((1, tn)),
        ],
        out_specs=[
            tile(tn), halo, tile(tn), heads, tile(tn), heads, tile(tn),
            pl.BlockSpec((nb, tl, tn), lambda bb, li, j: (bb, li, jnp.maximum(j - 4, 0))),
            pl.BlockSpec((nb, nh, tl), lambda bb, li, j: (bb, 0, li)),
        ],
        out_shape=[
            jax.ShapeDtypeStruct((b, l, tn), BF16),
            jax.ShapeDtypeStruct((b, HALO, tn), F32),
            jax.ShapeDtypeStruct((b, l, tn), BF16),
            jax.ShapeDtypeStruct((b, l * nh, HEAD_DIM), F32),
            jax.ShapeDtypeStruct((b, l, tn), BF16),
            jax.ShapeDtypeStruct((b, l * nh, HEAD_DIM), F32),
            jax.ShapeDtypeStruct((b, l, tn), BF16),
            jax.ShapeDtypeStruct((b, l, n_gate_tiles * tn), BF16),
            jax.ShapeDtypeStruct((b, nh, l), F32),
        ],
        scratch_shapes=[pltpu.VMEM((nb * tl, d), BF16), pltpu.VMEM((HALO + tl, tn), F32),
                        pltpu.VMEM((nb, HALO, tn), F32)],
        compiler_params=_params("parallel", "arbitrary", "arbitrary"),
        name="in_proj",
    )(x3, g.reshape(1, d), mod3, w_fl_t, b_f.reshape(nh, 1), w_main, w_gates, hist, w_pool, s_pool.reshape(1, tn))


def _mixer_out_kernel(ya_ref, yb_ref, ga_ref, gb_ref, x_ref, mod_ref, g_ref, wa_ref, wb_ref, wo_ref,
                      x1_ref, h2_ref, h2p_ref):
    merged = ga_ref[...].astype(F32) * jnp.dot(ya_ref[...], wa_ref[...], preferred_element_type=F32)
    merged = merged + gb_ref[...].astype(F32) * jnp.dot(yb_ref[...], wb_ref[...], preferred_element_type=F32)
    m = mod_ref[...]
    x1 = x_ref[...] + m[2:3, :] * jnp.dot(merged.astype(BF16), wo_ref[...], preferred_element_type=F32)
    x1_ref[...] = x1
    h2 = _modulated_norm(x1, g_ref[...], m, 3, 4)
    h2_ref[...] = h2.astype(BF16)
    _store_packed(h2p_ref, h2)


def _mixer_out(y_a, y_b, gates, x3, mod3, mod_row0, g_ffn, w_up_a, w_up_b, w_out, tl):
    b, l, d = x3.shape
    n_l = l // tl
    dp, da = y_a.shape[2], y_b.shape[2]
    tile = lambda width, col=0: pl.BlockSpec((None, tl, width), lambda bi, li: (bi, li, col))
    resident = lambda shape: pl.BlockSpec(shape, lambda bi, li: (0,) * len(shape), pipeline_mode=pl.Buffered(1))
    return pl.pallas_call(
        _mixer_out_kernel,
        grid=(b, n_l),
        in_specs=[
            tile(dp), tile(da), tile(d, 0), tile(d, 1), tile(d),
            pl.BlockSpec((None, N_MOD, d), lambda bi, li: (mod_row0 + bi, 0, 0)),
            pl.BlockSpec((1, d), lambda bi, li: (0, 0)),
            resident((dp, d)), resident((da, d)), resident((d, d)),
        ],
        out_specs=[
            tile(d), tile(d),
            pl.BlockSpec((tl * PACK_CHUNKS, LANES), lambda bi, li: (bi * n_l + li, 0)),
        ],
        out_shape=[
            jax.ShapeDtypeStruct((b, l, d), F32),
            jax.ShapeDtypeStruct((b, l, d), BF16),
            jax.ShapeDtypeStruct((b * l * PACK_CHUNKS, LANES), U32),
        ],
        compiler_params=_params("parallel", "parallel"),
        name="mixer_out",
    )(y_a, y_b, gates, gates, x3, mod3, g_ffn.reshape(1, d), w_up_a, w_up_b, w_out)


def _cumsum_kernel(x_ref, o_ref):
    x = x_ref[...]
    n = x.shape[1]
    idx = lax.broadcasted_iota(I32, x.shape, 1)
    s = 1
    while s < n:
        x = x + jnp.where(idx >= s, pltpu.roll(x, s, 1), 0.0)
        s *= 2
    o_ref[...] = x


def _cumsum_last(x3):
    b, h, n = x3.shape
    return pl.pallas_call(
        _cumsum_kernel,
        grid=(b,),
        in_specs=[pl.BlockSpec((None, h, n), lambda bi: (bi, 0, 0))],
        out_specs=pl.BlockSpec((None, h, n), lambda bi: (bi, 0, 0)),
        out_shape=jax.ShapeDtypeStruct((b, h, n), F32),
        compiler_params=_params("parallel"),
        name="cumsum_logf",
    )(x3)


def _attn_prompt_kernel(q_ref, k_ref, v_ref, fk_ref, o_ref, s0_ref, s1_ref, m_ref, l_ref, acc_ref, *, tq):
    qi = pl.program_id(2)
    tk = tq // 2
    n_chunks = tk // LANES

    def produce(j, s_ref, r0=0):
        off = pl.multiple_of(j * tk, tk)
        s = lax.dot_general(q_ref[r0:, :], k_ref[pl.ds(off, tk), :], NT_DIMS, preferred_element_type=F32)
        s_ref[r0:, :] = s - fk_ref[j] * LOG2_E

    def consume(j, s_ref, first_col=None, r0=0):
        s = s_ref[r0:, :]
        if first_col is not None:
            row = r0 + lax.broadcasted_iota(I32, s.shape, 0)
            col = first_col + lax.broadcasted_iota(I32, s.shape, 1)
            s = jnp.where(col <= row, s, NEG_INF)
        chunks = [s[:, c * LANES:(c + 1) * LANES] for c in range(n_chunks)]
        m_old = m_ref[r0:, :]
        m_new = jnp.maximum(m_old, jnp.max(functools.reduce(jnp.maximum, chunks), axis=1, keepdims=True))
        alpha = jnp.exp2(m_old - m_new)
        p_chunks = [jnp.exp2(ch - m_new) for ch in chunks]
        l_ref[r0:, :] = alpha * l_ref[r0:, :] + functools.reduce(lambda a, c: a + c, p_chunks)
        p = jnp.concatenate(p_chunks, axis=1).astype(BF16)
        v = v_ref[pl.ds(pl.multiple_of(j * tk, tk), tk), :]
        acc_ref[r0:, :] = alpha * acc_ref[r0:, :] + jnp.dot(p, v, preferred_element_type=F32)
        m_ref[r0:, :] = m_new

    m_ref[...] = jnp.full(m_ref.shape, NEG_INF, F32)
    l_ref[...] = jnp.zeros(l_ref.shape, F32)
    acc_ref[...] = jnp.zeros(acc_ref.shape, F32)
    produce(0, s0_ref)

    def pair(jj, carry):
        j0 = 2 * jj
        produce(j0 + 1, s1_ref)
        consume(j0, s0_ref)
        produce(j0 + 2, s0_ref)
        consume(j0 + 1, s1_ref)
        return carry

    lax.fori_loop(0, qi, pair, 0)
    produce(2 * qi + 1, s1_ref, r0=tk)
    consume(2 * qi, s0_ref, first_col=0)
    consume(2 * qi + 1, s1_ref, first_col=tk, r0=tk)

    l = jnp.sum(l_ref[...], axis=1, keepdims=True)
    o_ref[...] = (acc_ref[...] / l).astype(o_ref.dtype)


def _attn_prompt(q3, k3, v3, f3, tq):
    b, l, da = q3.shape
    nh = da // HEAD_DIM
    tk = tq // 2
    nk = l // tk
    fk = f3.reshape(b * nh, nk, 1, tk)
    return pl.pallas_call(
        functools.partial(_attn_prompt_kernel, tq=tq),
        grid=(b, nh, l // tq),
        in_specs=[
            pl.BlockSpec((None, tq, HEAD_DIM), lambda bi, hi, qi: (bi, qi, hi)),
            pl.BlockSpec((None, l, HEAD_DIM), lambda bi, hi, qi: (bi, 0, hi)),
            pl.BlockSpec((None, l, HEAD_DIM), lambda bi, hi, qi: (bi, 0, hi)),
            pl.BlockSpec((None, nk, 1, tk), lambda bi, hi, qi: (bi * nh + hi, 0, 0, 0)),
        ],
        out_specs=pl.BlockSpec((None, tq, HEAD_DIM), lambda bi, hi, qi: (bi, qi, hi)),
        out_shape=jax.ShapeDtypeStruct((b, l, da), BF16),
        scratch_shapes=[
            pltpu.VMEM((tq, tk), F32),
            pltpu.VMEM((tq, tk), F32),
            pltpu.VMEM((tq, LANES), F32),
            pltpu.VMEM((tq, LANES), F32),
            pltpu.VMEM((tq, HEAD_DIM), F32),
        ],
        compiler_params=_params("parallel", "parallel", "parallel"),
        name="fox_prompt",
    )(q3, k3, v3, fk)


def _attn_sample_kernel(q_ref, kp_ref, vp_ref, kn_ref, vn_ref, fp_ref, fn_ref, o_ref, m_ref, l_ref, acc_ref, *,
                        nh, pc, lq):
    c = pl.program_id(1)

    @pl.when(c == 0)
    def _():
        m_ref[...] = jnp.full(m_ref.shape, NEG_INF, F32)
        l_ref[...] = jnp.zeros(l_ref.shape, F32)
        acc_ref[...] = jnp.zeros(acc_ref.shape, F32)

    scores = []
    for h in range(nh):
        k = kp_ref[pl.ds(h, pc, stride=nh), :].astype(BF16)
        s = lax.dot_general(q_ref[:, h * HEAD_DIM:(h + 1) * HEAD_DIM], k, NT_DIMS, preferred_element_type=F32)
        scores.append(s - fp_ref[h:h + 1, :])
    probs, alphas = [], []
    for h, s in enumerate(scores):
        chunks = [s[:, j * LANES:(j + 1) * LANES] for j in range(pc // LANES)]
        m_old = m_ref[h]
        m_new = jnp.maximum(m_old, jnp.max(functools.reduce(jnp.maximum, chunks), axis=1, keepdims=True))
        alpha = jnp.exp(m_old - m_new)
        p_chunks = [jnp.exp(ch - m_new) for ch in chunks]
        l_ref[h] = alpha * l_ref[h] + functools.reduce(lambda a, b: a + b, p_chunks)
        m_ref[h] = m_new
        probs.append(jnp.concatenate(p_chunks, axis=1).astype(BF16))
        alphas.append(alpha)
    for h in range(nh):
        v = vp_ref[pl.ds(h, pc, stride=nh), :].astype(BF16)
        acc_ref[h] = alphas[h] * acc_ref[h] + jnp.dot(probs[h], v, preferred_element_type=F32)

    @pl.when(c == pl.num_programs(1) - 1)
    def _():
        row = lax.broadcasted_iota(I32, (lq, lq), 0)
        col = lax.broadcasted_iota(I32, (lq, lq), 1)
        for h in range(nh):
            cols = slice(h * HEAD_DIM, (h + 1) * HEAD_DIM)
            s = lax.dot_general(q_ref[:, cols], kn_ref[:, cols], NT_DIMS, preferred_element_type=F32)
            s = jnp.where(col <= row, s - fn_ref[h:h + 1, :lq], NEG_INF)
            m_old = m_ref[h]
            m_new = jnp.maximum(m_old, jnp.max(s, axis=1, keepdims=True))
            alpha = jnp.exp(m_old - m_new)
            p = jnp.exp(s - m_new[:, :lq])
            l = jnp.sum(alpha * l_ref[h], axis=1, keepdims=True) + jnp.sum(p, axis=1, keepdims=True)
            acc = alpha * acc_ref[h] + jnp.dot(p.astype(BF16), vn_ref[:, cols], preferred_element_type=F32)
            o_ref[:, cols] = (acc / l).astype(o_ref.dtype)


def _attn_sample(q3, kn3, vn3, cache_k, cache_v, f_past, f_new, pc):
    b, lq, da = q3.shape
    _, past, nh, dh = cache_k.shape
    nc = past // pc
    new_spec = pl.BlockSpec((None, lq, da), lambda bi, ci: (bi, 0, 0))
    past_spec = pl.BlockSpec((None, pc * nh, dh), lambda bi, ci: (bi, ci, 0))
    return pl.pallas_call(
        functools.partial(_attn_sample_kernel, nh=nh, pc=pc, lq=lq),
        grid=(b, nc),
        in_specs=[new_spec, past_spec, past_spec, new_spec, new_spec,
                  pl.BlockSpec((None, None, nh, pc), lambda bi, ci: (bi, ci, 0, 0)),
                  pl.BlockSpec((None, nh, LANES), lambda bi, ci: (bi, 0, 0))],
        out_specs=new_spec,
        out_shape=jax.ShapeDtypeStruct((b, lq, da), BF16),
        scratch_shapes=[pltpu.VMEM((nh, lq, LANES), F32), pltpu.VMEM((nh, lq, LANES), F32),
                        pltpu.VMEM((nh, lq, HEAD_DIM), F32)],
        compiler_params=_params("parallel", "arbitrary"),
        name="fox_sample",
    )(q3, cache_k.reshape(b, past * nh, dh), cache_v.reshape(b, past * nh, dh), kn3, vn3, f_past, f_new)


def _router_kernel(h_ref, wr_ref, br_ref, tri_ref, cnt0_ref, idx_ref, gate_ref, rank_ref, cnt_ref, carry_ref):
    @pl.when(pl.program_id(0) == 0)
    def _():
        carry_ref[...] = cnt0_ref[...]

    n_exp, tm = wr_ref.shape[0], h_ref.shape[0]
    scores = jax.nn.sigmoid(lax.dot_general(wr_ref[...], h_ref[...], NT_DIMS, preferred_element_type=F32))
    work = scores + br_ref[...]
    eidx = lax.broadcasted_iota(I32, (n_exp, tm), 0)
    picked = jnp.zeros((n_exp, tm), F32)
    onehots, gates = [], []
    for k in range(TOP_K):
        best = jnp.max(work, axis=0, keepdims=True)
        first = jnp.min(jnp.where(work == best, eidx, n_exp), axis=0, keepdims=True)
        onehot = eidx == first
        idx_ref[k:k + 1, :] = first
        gates.append(jnp.sum(jnp.where(onehot, scores, 0.0), axis=0, keepdims=True))
        onehots.append(onehot)
        picked = picked + jnp.where(onehot, 1.0, 0.0)
        work = jnp.where(onehot, NEG_INF, work)
    norm = ROUTED_SCALE / functools.reduce(lambda a, c: a + c, gates)
    cum = jnp.dot(picked.astype(BF16), tri_ref[...], preferred_element_type=F32)
    before = cum - picked + carry_ref[:, 0:1]
    for k in range(TOP_K):
        gate_ref[k:k + 1, :] = gates[k] * norm
        rank_ref[k:k + 1, :] = jnp.sum(jnp.where(onehots[k], before, 0.0), axis=0, keepdims=True).astype(I32)
    carry_ref[...] = carry_ref[...] + cum[:, tm - 1:tm]
    cnt_ref[...] = carry_ref[...]


def _router(h2, tok0, ntok, w_router_t, b_router, cnt0, tm):
    d = h2.shape[1]
    n_exp = w_router_t.shape[0]
    tri = jnp.triu(jnp.ones((tm, tm), BF16))
    tok_spec = pl.BlockSpec((TOP_K, tm), lambda i: (0, i))
    cnt_spec = pl.BlockSpec((n_exp, LANES), lambda i: (0, 0))
    return pl.pallas_call(
        _router_kernel,
        grid=(ntok // tm,),
        in_specs=[
            pl.BlockSpec((tm, d), lambda i: (tok0 // tm + i, 0)),
            pl.BlockSpec((n_exp, d), lambda i: (0, 0)),
            pl.BlockSpec((n_exp, 1), lambda i: (0, 0)),
            pl.BlockSpec((tm, tm), lambda i: (0, 0)),
            cnt_spec,
        ],
        out_specs=[tok_spec, tok_spec, tok_spec, cnt_spec],
        out_shape=[
            jax.ShapeDtypeStruct((TOP_K, ntok), I32),
            jax.ShapeDtypeStruct((TOP_K, ntok), F32),
            jax.ShapeDtypeStruct((TOP_K, ntok), I32),
            jax.ShapeDtypeStruct((n_exp, LANES), F32),
        ],
        scratch_shapes=[pltpu.VMEM((n_exp, LANES), F32)],
        compiler_params=_params("arbitrary"),
        name="moe_router",
    )(h2, w_router_t, b_router.reshape(n_exp, 1), tri, cnt0)


LAYOUT_ROWS = 8


def _layout_kernel(cnt_ref, o_ref):
    n_exp = cnt_ref.shape[0]
    lane = lax.broadcasted_iota(I32, o_ref.shape, 1)
    row0 = lane * MOE_ROWS

    def forward(e, carry):
        start, blk_e, valid, pad_start = carry
        cnt = cnt_ref[e]
        end = start + (cnt + MOE_ROWS - 1) // MOE_ROWS * MOE_ROWS
        in_span = lambda new, old: jnp.where(row0 >= start, jnp.where(row0 < end, new, old), old)
        blk_e = in_span(e, blk_e)
        valid = in_span(jnp.clip(start + cnt - row0, 0, MOE_ROWS), valid)
        return end, blk_e, valid, jnp.where(lane == e, start, pad_start)

    zeros = jnp.zeros(o_ref.shape, I32)
    end, blk_e, valid, pad_start = lax.fori_loop(0, n_exp, forward, (jnp.int32(0), zeros + (n_exp - 1), zeros, zeros))

    def backward(t, carry):
        nxt, blk_next = carry
        e = n_exp - 1 - t
        return jnp.where(cnt_ref[e] > 0, e, nxt), jnp.where(blk_e == e, nxt, blk_next)

    _, blk_next = lax.fori_loop(0, n_exp, backward, (jnp.int32(-1), zeros - 1))
    table = zeros + end // MOE_ROWS
    sub = lax.broadcasted_iota(I32, o_ref.shape, 0)
    for r, row in enumerate((blk_e, valid, blk_next, pad_start)):
        table = jnp.where(sub == r, row, table)
    o_ref[...] = table


def _layout(counts, n_blocks):
    n_exp = counts.shape[0]
    w = -(-max(n_blocks, n_exp) // LANES) * LANES
    table = pl.pallas_call(
        _layout_kernel,
        grid_spec=pltpu.PrefetchScalarGridSpec(
            num_scalar_prefetch=1, grid=(1,), in_specs=[],
            out_specs=pl.BlockSpec((LAYOUT_ROWS, w), lambda i, cnt: (0, 0))),
        out_shape=jax.ShapeDtypeStruct((LAYOUT_ROWS, w), I32),
        compiler_params=_params("arbitrary"),
        name="moe_layout",
    )(counts)
    return table[0, :n_blocks], table[1, :n_blocks], table[2, :n_blocks], table[3, :n_exp], table[4, :1]


def _slot_rows_kernel(ps_ref, idx_ref, rank_ref, o_ref):
    idx = idx_ref[...]
    rows = rank_ref[...]
    for e in range(ps_ref.shape[0]):
        rows = rows + jnp.where(idx == e, ps_ref[e], 0)
    o_ref[...] = rows


def _slot_rows(pad_start, idx_t, rank_t):
    k, ntok = idx_t.shape
    tm = min(2048, ntok)
    spec = pl.BlockSpec((k, tm), lambda i, ps: (0, i))
    return pl.pallas_call(
        _slot_rows_kernel,
        grid_spec=pltpu.PrefetchScalarGridSpec(num_scalar_prefetch=1, grid=(ntok // tm,), in_specs=[spec, spec],
                                               out_specs=spec),
        out_shape=jax.ShapeDtypeStruct((k, ntok), I32),
        compiler_params=_params("parallel"),
        name="moe_slot_rows",
    )(pad_start, idx_t, rank_t)


def _moe_kernel(be_ref, bv_ref, nx_ref, nu_ref, x_ref, wg_hbm, wu_hbm, wd_hbm, y_ref,
                wg_f32, wu_f32, wd_f32, wg_bf, wu_bf, wd_bf, sems):
    i = pl.program_id(0)
    active = i < nu_ref[0]
    expert = be_ref[i]

    def fetch(e):
        pairs = ((wg_hbm, wg_f32), (wu_hbm, wu_f32), (wd_hbm, wd_f32))
        return [pltpu.make_async_copy(w.at[e], buf, sems.at[n]) for n, (w, buf) in enumerate(pairs)]

    @pl.when(active & (i == 0))
    def _():
        for c in fetch(expert):
            c.start()

    @pl.when(active & ((i == 0) | (expert != be_ref[jnp.maximum(i - 1, 0)])))
    def _():
        for c in fetch(expert):
            c.wait()
        wg_bf[...] = wg_f32[...].astype(BF16)
        wu_bf[...] = wu_f32[...].astype(BF16)
        wd_bf[...] = wd_f32[...].astype(BF16)

        @pl.when(nx_ref[i] >= 0)
        def _():
            for c in fetch(nx_ref[i]):
                c.start()

    def chain(row0, rows):
        x = _load_packed(x_ref, rows, valid=bv_ref[i], row0=row0).astype(BF16)
        g = jnp.dot(x, wg_bf[...], preferred_element_type=F32)
        u = jnp.dot(x, wu_bf[...], preferred_element_type=F32)
        a = (_silu(g) * u).astype(BF16)
        _store_packed(y_ref, jnp.dot(a, wd_bf[...], preferred_element_type=F32), row0=row0)

    starts = range(0, MOE_ROWS, MOE_CHAIN_ROWS)
    full = bv_ref[i] > starts[-1]
    pl.when(active & full)(functools.partial(chain, 0, MOE_ROWS))
    for row0 in starts[:-1]:
        pl.when(active & jnp.logical_not(full) & (bv_ref[i] > row0))(functools.partial(chain, row0, MOE_CHAIN_ROWS))


def _moe_experts(x_sorted, blk_e, blk_valid, blk_next, n_used, w_e_gate, w_e_up, w_e_down):
    cap = x_sorted.shape[0] // PACK_CHUNKS
    d, de = w_e_gate.shape[1:]
    n_blocks = cap // MOE_ROWS
    row_map = lambda i, be, bv, nx, nu: (jnp.minimum(i, nu[0] - 1), 0)
    hbm = pl.BlockSpec(memory_space=pl.ANY)
    return pl.pallas_call(
        _moe_kernel,
        grid_spec=pltpu.PrefetchScalarGridSpec(
            num_scalar_prefetch=4,
            grid=(n_blocks,),
            in_specs=[pl.BlockSpec((MOE_ROWS * PACK_CHUNKS, LANES), row_map), hbm, hbm, hbm],
            out_specs=pl.BlockSpec((MOE_ROWS * PACK_CHUNKS, LANES), row_map),
            scratch_shapes=[
                pltpu.VMEM((d, de), F32), pltpu.VMEM((d, de), F32), pltpu.VMEM((de, d), F32),
                pltpu.VMEM((d, de), BF16), pltpu.VMEM((d, de), BF16), pltpu.VMEM((de, d), BF16),
                pltpu.SemaphoreType.DMA((3,)),
            ],
        ),
        out_shape=jax.ShapeDtypeStruct(x_sorted.shape, U32),
        compiler_params=_params("arbitrary"),
        name="moe_experts",
    )(blk_e, blk_valid, blk_next, n_used, x_sorted, w_e_gate, w_e_up, w_e_down)


def _sc_mesh():
    return plsc.VectorSubcoreMesh(core_axis_name="c", subcore_axis_name="s")


def _sc_worker_id():
    return lax.axis_index("s") * SC_CORES + lax.axis_index("c")


def _sc_scatter_rows(segments, n_out):
    row_shape = segments[0][0].shape[1:]
    dtype = segments[0][0].dtype
    n_seg = len(segments)
    geom = [(tok0, dw.shape[0] // SC_WORKERS, dw.shape[2]) for _, tok0, dw in segments]
    scratch = []
    for _, n_win, win in geom:
        scratch += [pltpu.VMEM((n_win, TOP_K, win), I32), pltpu.VMEM((win,) + row_shape, dtype)]

    @functools.partial(
        pl.kernel, mesh=_sc_mesh(),
        out_type=jax.ShapeDtypeStruct((n_out,) + row_shape, dtype),
        scratch_types=scratch + [pltpu.SemaphoreType.DMA],
    )
    def scatter_kernel(*refs):
        out_hbm = refs[2 * n_seg]
        sem = refs[-1]
        for s, (tok0, n_win, win) in enumerate(geom):
            rows_hbm, idx_hbm = refs[2 * s], refs[2 * s + 1]
            idx_v, rows_v = refs[2 * n_seg + 1 + 2 * s], refs[2 * n_seg + 2 + 2 * s]
            first = _sc_worker_id() * n_win
            pltpu.sync_copy(idx_hbm.at[pl.ds(first, n_win)], idx_v)

            @pl.loop(0, n_win)
            def _(w):
                pltpu.sync_copy(rows_hbm.at[pl.ds(tok0 + (first + w) * win, win)], rows_v)
                copies = [pltpu.make_async_copy(rows_v, out_hbm.at[idx_v.at[w, k]], sem) for k in range(TOP_K)]
                for c in copies:
                    c.start()
                for c in copies:
                    c.wait()

    args = []
    for rows, _, dest_win in segments:
        args += [rows, dest_win]
    return scatter_kernel(*args)


def _sc_gather_rows(table, idx_wins):
    row_shape = table.shape[1:]
    n_seg = len(idx_wins)
    geom = [(iw.shape[0] // SC_WORKERS, iw.shape[1]) for iw in idx_wins]
    scratch = []
    for n_win, win in geom:
        scratch += [pltpu.VMEM((n_win, win), I32), pltpu.VMEM((win,) + row_shape, table.dtype)]

    @functools.partial(
        pl.kernel, mesh=_sc_mesh(),
        out_type=[jax.ShapeDtypeStruct((iw.shape[0] * iw.shape[1],) + row_shape, table.dtype) for iw in idx_wins],
        scratch_types=scratch + [pltpu.SemaphoreType.DMA],
    )
    def gather_kernel(*refs):
        table_hbm = refs[0]
        sem = refs[-1]
        for s, (n_win, win) in enumerate(geom):
            idx_hbm, out_hbm = refs[1 + s], refs[1 + n_seg + s]
            idx_v, rows_v = refs[1 + 2 * n_seg + 2 * s], refs[2 + 2 * n_seg + 2 * s]
            first = _sc_worker_id() * n_win
            pltpu.sync_copy(idx_hbm.at[pl.ds(first, n_win)], idx_v)

            @pl.loop(0, n_win)
            def _(w):
                pltpu.async_copy(table_hbm.at[idx_v.at[w]], rows_v, sem).wait()
                pltpu.sync_copy(rows_v, out_hbm.at[pl.ds((first + w) * win, win)])

    return gather_kernel(table, *idx_wins)


def _final_kernel(x1_ref, h2_ref, yt_ref, gate_ref, mod_ref, wsg_ref, wsu_ref, wsd_ref, gf_ref, *rest):
    o_ref = rest[-1]
    h2 = h2_ref[...]
    g = jnp.dot(h2, wsg_ref[...], preferred_element_type=F32)
    u = jnp.dot(h2, wsu_ref[...], preferred_element_type=F32)
    f = jnp.dot((_silu(g) * u).astype(BF16), wsd_ref[...], preferred_element_type=F32)
    gate = gate_ref[...]
    tl = h2.shape[0]
    for k in range(TOP_K):
        f = f + gate[:, k:k + 1] * _load_packed(yt_ref.at[k], tl)
    x2 = x1_ref[...] + mod_ref[N_MOD - 1:N_MOD, :] * f
    ms = jnp.mean(x2 * x2, axis=-1, keepdims=True)
    o_ref[...] = x2 * lax.rsqrt(ms + EPS) * gf_ref[...]


def _final(x1, h2, b0, y_tok, gate, mod3, mod_row0, w_s_gate, w_s_up, w_s_down, g_final, tl, y_prev=None):
    b, l, d = x1.shape
    nb = gate.shape[0]
    ds = w_s_gate.shape[1]
    tile = pl.BlockSpec((None, tl, d), lambda bi, li: (b0 + bi, li, 0))
    const = lambda shape: pl.BlockSpec(shape, lambda bi, li: (0,) * len(shape), pipeline_mode=pl.Buffered(1))
    in_specs = [
        tile,
        tile,
        pl.BlockSpec((TOP_K, None, tl * PACK_CHUNKS, LANES), lambda bi, li: (0, bi, li, 0)),
        pl.BlockSpec((None, tl, TOP_K), lambda bi, li: (bi, li, 0)),
        pl.BlockSpec((None, N_MOD, d), lambda bi, li: (mod_row0 + b0 + bi, 0, 0)),
        const((d, ds)),
        const((d, ds)),
        const((ds, d)),
        const((1, d)),
    ]
    args = [x1, h2, y_tok, gate, mod3, w_s_gate, w_s_up, w_s_down, g_final.reshape(1, d)]
    aliases = {}
    if y_prev is not None:
        in_specs.append(pl.BlockSpec(memory_space=pl.ANY))
        aliases = {len(args): 0}
        args.append(y_prev)
    return pl.pallas_call(
        _final_kernel,
        grid=(nb, l // tl),
        in_specs=in_specs,
        out_specs=tile,
        out_shape=jax.ShapeDtypeStruct((b, l, d), F32),
        input_output_aliases=aliases,
        compiler_params=_params("parallel", "parallel"),
        name="combine_final",
    )(*args)


def _group_mixer(x3, mod3, mod_row0, hist, start, past, wts, tiles):
    b, l, d = x3.shape
    tl = tiles["tl"]
    dp = wts["w_pool"].shape[0] * wts["w_pool"].shape[1]
    da = wts["w_up_b"].shape[0]
    nh = da // HEAD_DIM
    t = b * l

    assert dp == da, "u, q, k, v must each be one column tile of the input projection"
    scale = HEAD_DIM ** -0.5 * (LOG2_E if past is None else 1.0)
    y_a, u_tail, qb, k, kb, v, vb, gates, logf_t = _in_proj(
        x3, wts["g_mix"], mod3, mod_row0, wts["w_fl_t"], wts["b_f"], wts["w_main"], wts["w_gates"], hist,
        wts["w_pool"],
        wts["s_pool"], start, scale, tiles["nb_in"], tl)

    if past is None:
        y_b = _attn_prompt(qb, kb, vb, _cumsum_last(logf_t), tiles["tq"])
    else:
        cache_k, cache_v, cache_logf = past
        p = cache_k.shape[1]
        pc = tiles["past_chunk"]
        assert l <= LANES and p % pc == 0 and p % LANES == 0
        lf_all = jnp.concatenate([jnp.swapaxes(cache_logf.astype(F32), 1, 2), logf_t,
                                  jnp.zeros((b, nh, LANES - l), F32)], axis=2)
        f_all = _cumsum_last(lf_all)
        f_past = f_all[:, :, :p].reshape(b, nh, p // pc, pc).transpose(0, 2, 1, 3)
        y_b = _attn_sample(qb, kb, vb, cache_k, cache_v, f_past, f_all[:, :, p:], pc)

    x1, h2, h2_packed = _mixer_out(y_a, y_b, gates, x3, mod3, mod_row0, wts["g_ffn"], wts["w_up_a"], wts["w_up_b"],
                                   wts["w_out"], tiles["tl_out"])
    caches = (u_tail[:, HALO - POOL_HIST:, :], k.reshape(b, l, nh, HEAD_DIM), v.reshape(b, l, nh, HEAD_DIM),
              jnp.swapaxes(logf_t, 1, 2))
    return dict(x1=x1, h2=h2, h2_packed=h2_packed, mod_row0=mod_row0, tiles=tiles), caches


def _moe_chunk(segments, wts):
    n_exp = wts["w_router_t"].shape[0]
    row_shape = (PACK_CHUNKS, LANES)
    counts_f = jnp.zeros((n_exp, LANES), F32)
    routed = []
    for grp, b0, nb in segments:
        b, l, d = grp["h2"].shape
        ntok = nb * l
        idx_t, gate_t, rank_t, counts_f = _router(grp["h2"].reshape(b * l, d), b0 * l, ntok, wts["w_router_t"],
                                                  wts["b_router"], counts_f, min(grp["tiles"]["t_route"], ntok))
        routed.append((idx_t, gate_t, rank_t, ntok))
    counts = counts_f[:, 0].astype(I32)

    n_slots = sum(r[3] for r in routed) * TOP_K
    n_blocks = -(-n_slots // MOE_ROWS) + n_exp
    cap = n_blocks * MOE_ROWS
    blk_e, blk_valid, blk_next, pad_start, n_used = _layout(counts, n_blocks)

    scatter_segs, gather_idx = [], []
    for (grp, b0, nb), (idx_t, _, rank_t, ntok) in zip(segments, routed):
        b, l, _ = grp["h2"].shape
        dest = _slot_rows(pad_start, idx_t, rank_t)
        win_d = min(SC_WINDOW, ntok // SC_WORKERS)
        dest_win = dest.reshape(TOP_K, ntok // win_d, win_d).transpose(1, 0, 2)
        scatter_segs.append((grp["h2_packed"].reshape((b * l,) + row_shape), b0 * l, dest_win))
        win_c = min(SC_WINDOW, ntok * TOP_K // SC_WORKERS)
        gather_idx.append(dest.reshape(ntok * TOP_K // win_c, win_c))

    x_sorted = _sc_scatter_rows(scatter_segs, cap)
    y_sorted = _moe_experts(x_sorted.reshape(cap * PACK_CHUNKS, LANES), blk_e, blk_valid, blk_next, n_used,
                            wts["w_e_gate"], wts["w_e_up"], wts["w_e_down"])
    y_toks = _sc_gather_rows(y_sorted.reshape((cap,) + row_shape), gather_idx)
    out = []
    for (grp, b0, nb), (_, gate_t, _, _), y_tok in zip(segments, routed, y_toks):
        l = grp["h2"].shape[1]
        out.append((y_tok.reshape(TOP_K, nb, l * PACK_CHUNKS, LANES), gate_t.T.reshape(nb, l, TOP_K)))
    return out


def kernel(x_prompt, x_sample, cache_pool, cache_k, cache_v, cache_logf, c_prompt, c_sample, w_ada, b_ada, g_mix, w_in, b_f, w_pool, s_pool, w_up_a, w_up_b, w_out, g_ffn, w_router, b_router, w_e_gate, w_e_up, w_e_down, w_s_gate, w_s_up, w_s_down, g_final):
    depth = w_ada.shape[0]
    assert depth == 1, "a single layer is supported"
    bp, lp, d = x_prompt.shape
    bs, ls, _ = x_sample.shape
    dp = w_pool.shape[1] * w_pool.shape[2]
    da = w_up_b.shape[1]
    nh = da // HEAD_DIM
    assert bp <= MOD_ROWS_SAMPLE

    one = lambda a: a.reshape(a.shape[1:])
    c_all = jnp.zeros((MOD_ROWS_SAMPLE + bs, d), F32).at[:bp].set(c_prompt).at[MOD_ROWS_SAMPLE:].set(c_sample)
    mod3 = _ada_mod(c_all, one(w_ada), one(b_ada)).reshape(c_all.shape[0], N_MOD, d)

    n_main = dp + 3 * da
    w_in1 = one(w_in)
    wts = dict(
        g_mix=one(g_mix), g_ffn=one(g_ffn), b_f=one(b_f), s_pool=one(s_pool), b_router=one(b_router),
        w_main=w_in1[:, :n_main].astype(BF16), w_gates=w_in1[:, n_main + nh:].astype(BF16),
        w_fl_t=w_in1[:, n_main:n_main + nh].T.astype(BF16),
        w_pool=one(w_pool).astype(BF16),
        w_up_a=one(w_up_a).astype(BF16), w_up_b=one(w_up_b).astype(BF16), w_out=one(w_out).astype(BF16),
        w_router_t=one(w_router).T.astype(BF16),
        w_e_gate=one(w_e_gate), w_e_up=one(w_e_up), w_e_down=one(w_e_down),
        w_s_gate=one(w_s_gate).astype(BF16), w_s_up=one(w_s_up).astype(BF16), w_s_down=one(w_s_down).astype(BF16),
    )

    tiles_p = dict(tl=min(512, lp), tq=min(1024, lp), t_route=min(512, bp * lp), tl_out=min(256, lp),
                   tl_final=min(256, lp), nb_in=1)
    tiles_s = dict(tl=ls, t_route=bs * ls, tl_out=ls, tl_final=ls, nb_in=bs, past_chunk=min(512, cache_k.shape[2]))

    hist_p = jnp.zeros((bp, HALO, dp), F32)
    grp_p, caches_p = _group_mixer(x_prompt, mod3, MOD_ROWS_PROMPT, hist_p, 0, None, wts, tiles_p)
    hist_s = jnp.pad(one(cache_pool), ((0, 0), (HALO - POOL_HIST, 0), (0, 0)))
    past = (one(cache_k), one(cache_v), one(cache_logf))
    grp_s, caches_s = _group_mixer(x_sample, mod3, MOD_ROWS_SAMPLE, hist_s, cache_k.shape[2], past, wts, tiles_s)

    n_chunks = MOE_CHUNKS if bp % MOE_CHUNKS == 0 else 1
    nb_c = bp // n_chunks
    chunks = [[(grp_p, c * nb_c, nb_c)] for c in range(n_chunks)]
    chunks[-1].append((grp_s, 0, bs))
    outs = {id(grp_p): None, id(grp_s): None}
    for segments in chunks:
        for (grp, b0, nb), (y_tok, gate) in zip(segments, _moe_chunk(segments, wts)):
            outs[id(grp)] = _final(grp["x1"], grp["h2"], b0, y_tok, gate, mod3, grp["mod_row0"], wts["w_s_gate"],
                                   wts["w_s_up"], wts["w_s_down"], g_final, grp["tiles"]["tl_final"],
                                   y_prev=outs[id(grp)])
    stack = lambda a: a[None]
    return (outs[id(grp_p)], outs[id(grp_s)], *map(stack, caches_p), *map(stack, caches_s))
```
